```python
import math
import jax
import jax.numpy as jnp
from jax import lax
import numpy as np

D_MODEL = 2048
BATCH = 4
SEQ = 4096
DEPTH = 2

CTX_LEN = 256
GRID_W = 64
HEAD_DIM = 64
ROPE_THETA = 10000.0
EPS = 1e-6
NEG_INF = -1e30
BLOCK = 128

WIN_HEADS = D_MODEL // 256
WIN_KV_HEADS = WIN_HEADS // 4
WINDOW = 128
DIFF_HEADS = D_MODEL // 512
HY_WIDTH = WIN_HEADS * HEAD_DIM
HY_SHORT = 3
HY_EMB = 33
HY_BANDS = (HY_EMB - 1) // 2
HY_ORDER = 64
HY_TARGET = 1e-2
HY_FAST_PCT = 0.3
HY_SLOW_PCT = 1.5
POOL_WIDTH = WIN_HEADS * HEAD_DIM
POOL_WINDOWS = (2, 4, 8, 16)
POOL_GROUP = POOL_WIDTH // len(POOL_WINDOWS)
N_BRANCH = 4
BRANCH_W = WIN_HEADS * HEAD_DIM

WIN_Q = WIN_HEADS * HEAD_DIM
WIN_KV = WIN_KV_HEADS * HEAD_DIM
DIFF_QK = DIFF_HEADS * 2 * HEAD_DIM
DIFF_V = DIFF_HEADS * 2 * HEAD_DIM
OFF_WQ = 0
OFF_WK = OFF_WQ + WIN_Q
OFF_WV = OFF_WK + WIN_KV
OFF_DQ = OFF_WV + WIN_KV
OFF_DK = OFF_DQ + DIFF_QK
OFF_DV = OFF_DK + DIFF_QK
OFF_HY = OFF_DV + DIFF_V
OFF_PL = OFF_HY + 3 * HY_WIDTH
OFF_GT = OFF_PL + POOL_WIDTH
IN_COLS = OFF_GT + N_BRANCH * D_MODEL

N_GROUPS = 4
EXPERTS_PER_GROUP = 8
N_EXPERTS = N_GROUPS * EXPERTS_PER_GROUP
TOP_K = 2
EXPERT_FF = D_MODEL // 2
MOE_BLOCK = 128

kernel_name = 'hybrid_prefix_dit_block'

F32 = jnp.float32


def _rms(x, g):
    xf = x.astype(F32)
    y = xf * lax.rsqrt(jnp.mean(jnp.square(xf), axis=-1, keepdims=True) + EPS)
    return (y * g.astype(F32)).astype(x.dtype)


def _grid_rope(n_tok):
    rows = n_tok // GRID_W
    row = jnp.repeat(jnp.arange(rows), GRID_W).astype(F32)
    col = jnp.tile(jnp.arange(GRID_W), rows).astype(F32)
    quarter = HEAD_DIM // 4
    inv = ROPE_THETA ** (-jnp.arange(quarter, dtype=F32) / quarter)
    ang_r = row[:, None] * inv
    ang_c = col[:, None] * inv
    return (jnp.cos(ang_r), jnp.sin(ang_r), jnp.cos(ang_c), jnp.sin(ang_c))


def _rot_half(x, cos, sin):
    x1, x2 = jnp.split(x, 2, axis=-1)
    cs = cos[:, None, :].astype(x.dtype)
    sn = sin[:, None, :].astype(x.dtype)
    return jnp.concatenate([x1 * cs - x2 * sn, x2 * cs + x1 * sn], axis=-1)


def _rope2d(x, tabs):
    cr, sr, cc, sc = tabs
    xr, xc = jnp.split(x, 2, axis=-1)
    return jnp.concatenate([_rot_half(xr, cr, sr), _rot_half(xc, cc, sc)], axis=-1)


def _window_attn(q, k, v, kc, vc, sink):
    B, S, H, d = q.shape
    Hkv = k.shape[2]
    G = H // Hkv
    nb = S // BLOCK
    C = kc.shape[1]
    qb = q.reshape(B, nb, BLOCK, Hkv, G, d)

    def band(t):
        tp = jnp.pad(t, ((0, 0), (BLOCK, BLOCK), (0, 0), (0, 0))).reshape(B, nb + 2, BLOCK, Hkv, d)
        return jnp.concatenate([tp[:, :-2], tp[:, 1:-1], tp[:, 2:]], axis=2)

    kw = band(k)
    vw = band(v)
    scale = d ** -0.5
    s_loc = jnp.einsum('bnqhgd,bnkhd->bnhgqk', qb, kw).astype(F32) * scale
    qi = jnp.arange(BLOCK)[:, None]
    kj = jnp.arange(3 * BLOCK)[None, :]
    band_ok = jnp.abs(kj - BLOCK - qi) <= WINDOW
    kpos = (jnp.arange(nb)[:, None] - 1) * BLOCK + jnp.arange(3 * BLOCK)[None, :]
    valid = band_ok[None] & ((kpos >= 0) & (kpos < S))[:, None, :]
    s_loc = jnp.where(valid[None, :, None, None], s_loc, NEG_INF)
    s_ctx = jnp.einsum('bnqhgd,bchd->bnhgqc', qb, kc).astype(F32) * scale
    s_sink = jnp.broadcast_to(sink.astype(F32).reshape(Hkv, G)[None, None, :, :, None, None], s_loc.shape[:-1] + (1,))
    p = jax.nn.softmax(jnp.concatenate([s_loc, s_ctx, s_sink], axis=-1), axis=-1).astype(v.dtype)
    nl = 3 * BLOCK
    out = (jnp.einsum('bnhgqk,bnkhd->bnqhgd', p[..., :nl], vw)
           + jnp.einsum('bnhgqc,bchd->bnqhgd', p[..., nl:nl + C], vc))
    return out.reshape(B, S, H * d)


def _ctx_attn_sink(qc, kc, vc, sink):
    B, C, H, d = qc.shape
    Hkv = kc.shape[2]
    G = H // Hkv
    q = qc.reshape(B, C, Hkv, G, d)
    s = jnp.einsum('bqhgd,bkhd->bhgqk', q, kc).astype(F32) * d ** -0.5
    sk = jnp.broadcast_to(sink.astype(F32).reshape(Hkv, G)[None, :, :, None, None], s.shape[:-1] + (1,))
    p = jax.nn.softmax(jnp.concatenate([s, sk], axis=-1), axis=-1)[..., :C].astype(vc.dtype)
    return jnp.einsum('bhgqk,bkhd->bqhgd', p, vc).reshape(B, C, H * d)


def _diff_core(q1, q2, k1, k2, v, lam):
    scale = q1.shape[-1] ** -0.5
    p1 = jax.nn.softmax(jnp.einsum('bqhd,bkhd->bhqk', q1, k1).astype(F32) * scale, axis=-1)
    p2 = jax.nn.softmax(jnp.einsum('bqhd,bkhd->bhqk', q2, k2).astype(F32) * scale, axis=-1)
    a = (p1 - lam * p2).astype(v.dtype)
    return jnp.einsum('bhqk,bkhe->bqhe', a, v)


def _diff_latent(q1, q2, k1, k2, v, lam):
    B, S, H, d = q1.shape
    nb = S // BLOCK

    def blocks(t):
        return t.reshape(B, nb, BLOCK, H, d).swapaxes(0, 1)

    out = lax.map(lambda qq: _diff_core(qq[0], qq[1], k1, k2, v, lam), (blocks(q1), blocks(q2)))
    return out.swapaxes(0, 1).reshape(B, S, H, v.shape[-1])


def _diff_post(y, gain, lam_init):
    B, L, H, E = y.shape
    return (_rms(y, gain) * (1.0 - lam_init)).reshape(B, L, H * E)


def _short_conv(u, w, b):
    L = u.shape[1]
    half = HY_SHORT // 2
    up = jnp.pad(u, ((0, 0), (half, HY_SHORT - 1 - half), (0, 0)))
    out = b
    for j in range(HY_SHORT):
        out = out + up[:, j:j + L] * w[j]
    return out


def _hyena_filter(L, p):
    n = jnp.arange(L, dtype=F32)
    t = n / max(L - 1, 1)
    w = 2.0 * math.pi * n / L
    f = jnp.linspace(1e-4, HY_BANDS - 1, HY_BANDS, dtype=F32)
    z = jnp.concatenate([t[:, None], jnp.cos(w[:, None] * f), -jnp.sin(w[:, None] * f)], axis=-1)
    freq = p['hy_freq'].astype(F32)
    h = jnp.sin(freq[0] * (z @ p['hy_w1'].astype(F32) + p['hy_b1'].astype(F32)))
    h = jnp.sin(freq[1] * (h @ p['hy_w2'].astype(F32) + p['hy_b2'].astype(F32)))
    h = (h @ p['hy_w3'].astype(F32)) * jnp.exp(-t[:, None] * jnp.abs(p['hy_decay'].astype(F32)))
    hf = h[:, :HY_WIDTH]
    hb = h[:, HY_WIDTH:]
    k = jnp.concatenate([hf, jnp.zeros((1, HY_WIDTH), F32), hb[:0:-1]], axis=0)
    return k / (jnp.sum(jnp.abs(k), axis=0, keepdims=True) + EPS)


def _hyena(u, p):
    L = u.shape[1]
    uc = _short_conv(u, p['hy_conv_w'], p['hy_conv_b'])
    x0, x1, v = jnp.split(uc, 3, axis=-1)
    z = (v * x1).astype(F32)
    k = _hyena_filter(L, p)
    y = jnp.fft.irfft(jnp.fft.rfft(z, n=2 * L, axis=1) * jnp.fft.rfft(k, n=2 * L, axis=0)[None], n=2 * L, axis=1)[:, :L]
    y = y + p['hy_skip'].astype(F32) * z
    return y.astype(u.dtype) * x0


def _pool_mix(u, w_grp, scale):
    B, L, Ch = u.shape
    uf = u.astype(F32)
    cs = jnp.concatenate([jnp.zeros((B, 1, Ch), F32), jnp.cumsum(uf, axis=1)], axis=1)
    t = jnp.arange(L)
    groups = []
    for gi, w in enumerate(POOL_WINDOWS):
        lo = w // 2
        hi = w - 1 - lo
        start = jnp.clip(t - lo, 0, L)
        end = jnp.clip(t + hi + 1, 0, L)
        cg = cs[..., gi * POOL_GROUP:(gi + 1) * POOL_GROUP]
        groups.append((cg[:, end] - cg[:, start]) / (end - start).astype(F32)[None, :, None])
    d = jnp.concatenate(groups, axis=-1) - uf
    d = jnp.einsum('blgc,gce->blge', d.reshape(B, L, len(POOL_WINDOWS), POOL_GROUP), w_grp.astype(F32)).reshape(B, L, Ch)
    return (d * scale.astype(F32)).astype(u.dtype)


def _merge(ys, gates, w_branch, w_out):
    acc = gates[..., 0, :] * (ys[0] @ w_branch[0])
    for i in range(1, N_BRANCH):
        acc = acc + gates[..., i, :] * (ys[i] @ w_branch[i])
    return acc @ w_out


def _route(xt, wg, bg, we, be):
    N = xt.shape[0]
    hf = xt.astype(F32)
    pg = jax.nn.softmax(hf @ wg.astype(F32) + bg.astype(F32), axis=-1)
    pg_top, g_idx = lax.top_k(pg, 1)
    le = (hf @ we.astype(F32) + be.astype(F32)).reshape(N, N_GROUPS, EXPERTS_PER_GROUP)
    idx = jnp.broadcast_to(g_idx[:, :, None], (N, 1, EXPERTS_PER_GROUP))
    le_g = jnp.take_along_axis(le, idx, axis=1)[:, 0]
    vals, j = lax.top_k(le_g, TOP_K)
    wts = jax.nn.softmax(vals, axis=-1) * pg_top
    return g_idx * EXPERTS_PER_GROUP + j, wts


def _moe(xt, p):
    N, D = xt.shape
    e, w = _route(xt, p['r_group_w'], p['r_group_b'], p['r_expert_w'], p['r_expert_b'])
    M = N * TOP_K
    e_flat = e.reshape(M)
    w_flat = w.reshape(M)
    tok = jnp.repeat(jnp.arange(N, dtype=jnp.int32), TOP_K)
    order = jnp.argsort(e_flat)
    e_s = e_flat[order]
    counts = jnp.bincount(e_flat, length=N_EXPERTS)
    padded = (counts + MOE_BLOCK - 1) // MOE_BLOCK * MOE_BLOCK
    pad_end = jnp.cumsum(padded)
    pad_start = pad_end - padded
    cnt_start = jnp.cumsum(counts) - counts
    dest = pad_start[e_s] + jnp.arange(M) - cnt_start[e_s]
    n_blocks = -(-(M + N_EXPERTS * (MOE_BLOCK - 1)) // MOE_BLOCK)
    P = n_blocks * MOE_BLOCK
    slot_tok = jnp.full((P,), N, jnp.int32).at[dest].set(tok[order])
    slot_w = jnp.zeros((P,), F32).at[dest].set(w_flat[order])
    blk_e = jnp.minimum(jnp.searchsorted(pad_end, jnp.arange(n_blocks) * MOE_BLOCK, side='right'), N_EXPERTS - 1)
    xb = jnp.concatenate([xt, jnp.zeros((1, D), xt.dtype)], axis=0)[slot_tok].reshape(n_blocks, MOE_BLOCK, D)
    w1, w3, w2 = p['e_w1'], p['e_w3'], p['e_w2']

    def expert(args):
        xblk, ei = args
        return (jax.nn.silu(xblk @ w1[ei]) * (xblk @ w3[ei])) @ w2[ei]

    yb = lax.map(expert, (xb, blk_e)).reshape(P, D)
    out = jnp.zeros((N + 1, D), yb.dtype).at[slot_tok].add(yb * slot_w[:, None].astype(yb.dtype))
    return out[:N]


def _layer(x, ctx, c, c_ctx, p, layer_idx, update_ctx):
    B, S, D = x.shape
    C = ctx.shape[1]
    mod = jax.nn.silu(c) @ p['w_mod'] + p['b_mod']
    mod_c = jax.nn.silu(c_ctx) @ p['w_mod'] + p['b_mod']
    sh1, sc1, g1, sh2, sc2, g2 = jnp.split(mod[:, None, :], 6, axis=-1)
    csh1, csc1, cg1, csh2, csc2, cg2 = jnp.split(mod_c, 6, axis=-1)
    hx = _rms(x, p['norm1']) * (1 + sc1) + sh1
    hc = _rms(ctx, p['norm1']) * (1 + csc1) + csh1
    ux = hx @ p['w_in']
    if update_ctx:
        uc = hc @ p['w_in']
        uc_wkv = uc[..., OFF_WK:OFF_DQ]
        uc_dkv = uc[..., OFF_DK:OFF_HY]
    else:
        uc_wkv = hc @ p['w_in'][:, OFF_WK:OFF_DQ]
        uc_dkv = hc @ p['w_in'][:, OFF_DK:OFF_HY]
    tabs = _grid_rope(S)

    qw = _rope2d(_rms(ux[..., OFF_WQ:OFF_WK].reshape(B, S, WIN_HEADS, HEAD_DIM), p['win_qnorm']), tabs)
    kw = _rope2d(_rms(ux[..., OFF_WK:OFF_WV].reshape(B, S, WIN_KV_HEADS, HEAD_DIM), p['win_knorm']), tabs)
    vw = ux[..., OFF_WV:OFF_DQ].reshape(B, S, WIN_KV_HEADS, HEAD_DIM)
    kcw = _rms(uc_wkv[..., :WIN_KV].reshape(B, C, WIN_KV_HEADS, HEAD_DIM), p['win_knorm'])
    vcw = uc_wkv[..., WIN_KV:].reshape(B, C, WIN_KV_HEADS, HEAD_DIM)
    y_win = _window_attn(qw, kw, vw, kcw, vcw, p['win_sink'])

    lam_init = 0.8 - 0.6 * math.exp(-0.3 * layer_idx)
    lq1, lk1, lq2, lk2 = p['diff_lambda']
    lam = (jnp.exp(jnp.sum(lq1 * lk1).astype(F32)) - jnp.exp(jnp.sum(lq2 * lk2).astype(F32)) + lam_init)
    qd = _rope2d(_rms(ux[..., OFF_DQ:OFF_DK].reshape(B, S, 2 * DIFF_HEADS, HEAD_DIM), p['diff_qnorm']), tabs).reshape(B, S, DIFF_HEADS, 2, HEAD_DIM)
    kd = _rope2d(_rms(ux[..., OFF_DK:OFF_DV].reshape(B, S, 2 * DIFF_HEADS, HEAD_DIM), p['diff_knorm']), tabs).reshape(B, S, DIFF_HEADS, 2, HEAD_DIM)
    vd = ux[..., OFF_DV:OFF_HY].reshape(B, S, DIFF_HEADS, 2 * HEAD_DIM)
    kcd = _rms(uc_dkv[..., :DIFF_QK].reshape(B, C, 2 * DIFF_HEADS, HEAD_DIM), p['diff_knorm']).reshape(B, C, DIFF_HEADS, 2, HEAD_DIM)
    vcd = uc_dkv[..., DIFF_QK:].reshape(B, C, DIFF_HEADS, 2 * HEAD_DIM)
    k1 = jnp.concatenate([kd[..., 0, :], kcd[..., 0, :]], axis=1)
    k2 = jnp.concatenate([kd[..., 1, :], kcd[..., 1, :]], axis=1)
    vall = jnp.concatenate([vd, vcd], axis=1)
    y_diff = _diff_post(_diff_latent(qd[..., 0, :], qd[..., 1, :], k1, k2, vall, lam), p['diff_subln'], lam_init)

    y_hy = _hyena(ux[..., OFF_HY:OFF_PL], p)
    y_pool = _pool_mix(ux[..., OFF_PL:OFF_GT], p['pool_proj'], p['pool_scale'])
    gates = jax.nn.sigmoid(ux[..., OFF_GT:].reshape(B, S, N_BRANCH, D))
    x = x + g1 * _merge((y_win, y_diff, y_hy, y_pool), gates, p['w_branch'], p['w_out'])

    if update_ctx:
        qcw = _rms(uc[..., OFF_WQ:OFF_WK].reshape(B, C, WIN_HEADS, HEAD_DIM), p['win_qnorm'])
        yc_win = _ctx_attn_sink(qcw, kcw, vcw, p['win_sink'])
        qcd = _rms(uc[..., OFF_DQ:OFF_DK].reshape(B, C, 2 * DIFF_HEADS, HEAD_DIM), p['diff_qnorm']).reshape(B, C, DIFF_HEADS, 2, HEAD_DIM)
        yc_diff = _diff_post(_diff_core(qcd[..., 0, :], qcd[..., 1, :], kcd[..., 0, :], kcd[..., 1, :], vcd, lam), p['diff_subln'], lam_init)
        yc_hy = _hyena(uc[..., OFF_HY:OFF_PL], p)
        yc_pool = _pool_mix(uc[..., OFF_PL:OFF_GT], p['pool_proj'], p['pool_scale'])
        gates_c = jax.nn.sigmoid(uc[..., OFF_GT:].reshape(B, C, N_BRANCH, D))
        ctx = ctx + cg1 * _merge((yc_win, yc_diff, yc_hy, yc_pool), gates_c, p['w_branch'], p['w_out'])

    h2 = (_rms(x, p['norm2']) * (1 + sc2) + sh2).reshape(B * S, D)
    if update_ctx:
        h2c = (_rms(ctx, p['norm2']) * (1 + csc2) + csh2).reshape(B * C, D)
        y = _moe(jnp.concatenate([h2, h2c], axis=0), p)
        x = x + g2 * y[:B * S].reshape(B, S, D)
        ctx = ctx + cg2 * y[B * S:].reshape(B, C, D)
    else:
        x = x + g2 * _moe(h2, p).reshape(B, S, D)
    return x, ctx


def setup_inputs(seed: int = 0) -> dict:
    key = jax.random.key(seed)
    ks = iter(jax.random.split(key, 40))

    def nrm(shape, s):
        return jax.random.normal(next(ks), shape, F32) * s

    Lr = DEPTH
    D = D_MODEL
    decay0 = jnp.tile(jnp.linspace(-math.log(HY_TARGET) / HY_SLOW_PCT, -math.log(HY_TARGET) / HY_FAST_PCT, HY_WIDTH, dtype=F32), 2)
    return {
        'x': nrm((BATCH, SEQ, D), 1.0),
        'c': nrm((BATCH, D), 1.0),
        'ctx': nrm((BATCH, CTX_LEN, D), 1.0),
        'c_ctx': nrm((D,), 1.0),
        'w_mod': nrm((Lr, D, 6 * D), 0.5 * D ** -0.5),
        'b_mod': nrm((Lr, 6 * D), 0.02),
        'norm1': 1.0 + nrm((Lr, D), 0.05),
        'norm2': 1.0 + nrm((Lr, D), 0.05),
        'w_in': nrm((Lr, D, IN_COLS), D ** -0.5),
        'win_sink': nrm((Lr, WIN_HEADS), 0.5),
        'win_qnorm': 1.0 + nrm((Lr, HEAD_DIM), 0.05),
        'win_knorm': 1.0 + nrm((Lr, HEAD_DIM), 0.05),
        'diff_qnorm': 1.0 + nrm((Lr, HEAD_DIM), 0.05),
        'diff_knorm': 1.0 + nrm((Lr, HEAD_DIM), 0.05),
        'diff_lambda': nrm((Lr, 4, HEAD_DIM), 0.1),
        'diff_subln': 1.0 + nrm((Lr, 2 * HEAD_DIM), 0.05),
        'hy_conv_w': nrm((Lr, HY_SHORT, 3 * HY_WIDTH), HY_SHORT ** -0.5),
        'hy_conv_b': nrm((Lr, 3 * HY_WIDTH), 0.02),
        'hy_w1': nrm((Lr, HY_EMB, HY_ORDER), HY_EMB ** -0.5),
        'hy_b1': nrm((Lr, HY_ORDER), 0.1),
        'hy_w2': nrm((Lr, HY_ORDER, HY_ORDER), HY_ORDER ** -0.5),
        'hy_b2': nrm((Lr, HY_ORDER), 0.1),
        'hy_w3': nrm((Lr, HY_ORDER, 2 * HY_WIDTH), HY_ORDER ** -0.5),
        'hy_freq': 1.0 + nrm((Lr, 2, HY_ORDER), 0.05),
        'hy_decay': decay0[None] * (1.0 + nrm((Lr, 2 * HY_WIDTH), 0.05)),
        'hy_skip': nrm((Lr, HY_WIDTH), 1.0),
        'pool_proj': nrm((Lr, len(POOL_WINDOWS), POOL_GROUP, POOL_GROUP), POOL_GROUP ** -0.5),
        'pool_scale': 1.0 + nrm((Lr, POOL_WIDTH), 0.05),
        'w_branch': nrm((Lr, N_BRANCH, BRANCH_W, D), BRANCH_W ** -0.5),
        'w_out': nrm((Lr, D, D), D ** -0.5),
        'r_group_w': nrm((Lr, D, N_GROUPS), D ** -0.5),
        'r_group_b': nrm((Lr, N_GROUPS), 0.01),
        'r_expert_w': nrm((Lr, D, N_EXPERTS), D ** -0.5),
        'r_expert_b': nrm((Lr, N_EXPERTS), 0.01),
        'e_w1': nrm((Lr, N_EXPERTS, D, EXPERT_FF), D ** -0.5),
        'e_w3': nrm((Lr, N_EXPERTS, D, EXPERT_FF), D ** -0.5),
        'e_w2': nrm((Lr, N_EXPERTS, EXPERT_FF, D), EXPERT_FF ** -0.5),
    }


def reference(x, c, ctx, c_ctx, w_mod, b_mod, norm1, norm2, w_in, win_sink, win_qnorm, win_knorm,
              diff_qnorm, diff_knorm, diff_lambda, diff_subln, hy_conv_w, hy_conv_b, hy_w1, hy_b1,
              hy_w2, hy_b2, hy_w3, hy_freq, hy_decay, hy_skip, pool_proj, pool_scale, w_branch, w_out,
              r_group_w, r_group_b, r_expert_w, r_expert_b, e_w1, e_w3, e_w2):
    for l in range(DEPTH):
        p = dict(w_mod=w_mod[l], b_mod=b_mod[l], norm1=norm1[l], norm2=norm2[l], w_in=w_in[l],
                 win_sink=win_sink[l], win_qnorm=win_qnorm[l], win_knorm=win_knorm[l],
                 diff_qnorm=diff_qnorm[l], diff_knorm=diff_knorm[l], diff_lambda=diff_lambda[l],
                 diff_subln=diff_subln[l], hy_conv_w=hy_conv_w[l], hy_conv_b=hy_conv_b[l],
                 hy_w1=hy_w1[l], hy_b1=hy_b1[l], hy_w2=hy_w2[l], hy_b2=hy_b2[l], hy_w3=hy_w3[l],
                 hy_freq=hy_freq[l], hy_decay=hy_decay[l], hy_skip=hy_skip[l],
                 pool_proj=pool_proj[l], pool_scale=pool_scale[l], w_branch=w_branch[l], w_out=w_out[l],
                 r_group_w=r_group_w[l], r_group_b=r_group_b[l], r_expert_w=r_expert_w[l],
                 r_expert_b=r_expert_b[l], e_w1=e_w1[l], e_w3=e_w3[l], e_w2=e_w2[l])
        x, ctx = _layer(x, ctx, c, c_ctx, p, l, l < DEPTH - 1)
    return x
```

```python
import functools
import math

import jax
import jax.numpy as jnp
import numpy as np
from jax import lax
from jax.experimental import pallas as pl
from jax.experimental.pallas import tpu as pltpu

F32 = jnp.float32
BF16 = jnp.bfloat16

GRID_W = 64
HEAD_DIM = 64
ROPE_THETA = 10000.0
EPS = 1e-6
NEG_INF = -1e30
BLOCK = 128
WINDOW = 128
LANES = 128

WIN_HEADS = 8
WIN_KV_HEADS = 2
WIN_GROUP = WIN_HEADS // WIN_KV_HEADS
DIFF_HEADS = 4
HY_WIDTH = 512
HY_EMB = 33
HY_BANDS = 16
POOL_WINDOWS = (2, 4, 8, 16)
N_BRANCH = 4
BRANCH_W = 512

OFF_WQ = 0
OFF_WK = 512
OFF_WV = 640
OFF_DQ = 768
OFF_DK = 1280
OFF_DV = 1792
OFF_HY = 2304
OFF_PL = 3840
OFF_GT = 4352

N_GROUPS = 4
EXPERTS_PER_GROUP = 8
N_EXPERTS = 32
TOP_K = 2
MOE_TM = 256

FFT_NA = 64
FFT_NB = 128
FFT_N = FFT_NA * FFT_NB


def _params(sem, mib=48):
    return pltpu.CompilerParams(dimension_semantics=sem, vmem_limit_bytes=mib * 2**20)


def _dot(a, b):
    return jnp.dot(a, b, preferred_element_type=F32)


def _split(a):
    hi = a.astype(BF16)
    lo = (a - hi.astype(F32)).astype(BF16)
    return hi, lo


def _dot3(a, b):
    ah, al = _split(a)
    bh, bl = _split(b)
    return _dot(ah, bh) + (_dot(ah, bl) + _dot(al, bh))


def _mod_body(c_ref, w_ref, b_ref, o_ref):
    c = c_ref[...]
    a = (c * jax.nn.sigmoid(c)).astype(BF16)
    o_ref[...] = _dot(a, w_ref[...].astype(BF16)) + b_ref[...]


def _modulation(cs, w_mod, b_mod):
    d, n = w_mod.shape
    tn = 1024
    return pl.pallas_call(
        _mod_body,
        grid=(n // tn,),
        in_specs=[pl.BlockSpec((8, d), lambda j: (0, 0)),
                  pl.BlockSpec((d, tn), lambda j: (0, j)),
                  pl.BlockSpec((1, tn), lambda j: (0, j))],
        out_specs=pl.BlockSpec((8, tn), lambda j: (0, j)),
        out_shape=jax.ShapeDtypeStruct((8, n), F32),
        compiler_params=_params(("parallel",)),
        name="modulation",
    )(cs, w_mod, b_mod.reshape(1, n))


def _prenorm_body(x_ref, g_ref, sc_ref, sh_ref, o_ref):
    x = x_ref[0]
    ms = jnp.mean(x * x, axis=-1, keepdims=True)
    y = x * lax.rsqrt(ms + EPS) * g_ref[...]
    o_ref[0] = (y * (1.0 + sc_ref[0]) + sh_ref[0]).astype(o_ref.dtype)


def _prenorm(x, gain, scale, shift, out_dtype=BF16):
    b, l, d = x.shape
    ts = min(l, 512)
    return pl.pallas_call(
        _prenorm_body,
        grid=(b, l // ts),
        in_specs=[pl.BlockSpec((1, ts, d), lambda bi, i: (bi, i, 0)),
                  pl.BlockSpec((1, d), lambda bi, i: (0, 0)),
                  pl.BlockSpec((1, 1, d), lambda bi, i: (bi, 0, 0)),
                  pl.BlockSpec((1, 1, d), lambda bi, i: (bi, 0, 0))],
        out_specs=pl.BlockSpec((1, ts, d), lambda bi, i: (bi, i, 0)),
        out_shape=jax.ShapeDtypeStruct((b, l, d), out_dtype),
        compiler_params=_params(("parallel", "parallel")),
        name="prenorm",
    )(x, gain.reshape(1, d), scale, shift)


def _mm_body(a_ref, w_ref, o_ref):
    o_ref[...] = _dot(a_ref[...], w_ref[...]).astype(o_ref.dtype)


def _matmul(a, w, tm, tn, out_dtype=F32):
    m, k = a.shape
    n = w.shape[1]
    tm = min(tm, m)
    return pl.pallas_call(
        _mm_body,
        grid=(n // tn, m // tm),
        in_specs=[pl.BlockSpec((tm, k), lambda j, i: (i, 0)),
                  pl.BlockSpec((k, tn), lambda j, i: (0, j))],
        out_specs=pl.BlockSpec((tm, tn), lambda j, i: (i, j)),
        out_shape=jax.ShapeDtypeStruct((m, n), out_dtype),
        compiler_params=_params(("parallel", "parallel")),
        name="in_proj",
    )(a, w)


def _rope_tables(n_tok, rope):
    if not rope:
        return jnp.ones((n_tok, LANES), F32), jnp.zeros((n_tok, LANES), F32)
    rows = n_tok // GRID_W
    row = jnp.repeat(jnp.arange(rows), GRID_W).astype(F32)
    col = jnp.tile(jnp.arange(GRID_W), rows).astype(F32)
    quarter = HEAD_DIM // 4
    inv = ROPE_THETA ** (-jnp.arange(quarter, dtype=F32) / quarter)
    ar = row[:, None] * inv
    ac = col[:, None] * inv
    cos64 = jnp.concatenate([jnp.cos(ar), jnp.cos(ar), jnp.cos(ac), jnp.cos(ac)], axis=-1)
    sin64 = jnp.concatenate([-jnp.sin(ar), jnp.sin(ar), -jnp.sin(ac), jnp.sin(ac)], axis=-1)
    return jnp.tile(cos64, (1, 2)), jnp.tile(sin64, (1, 2))


def _head_ones():
    i = np.arange(LANES)
    return jnp.asarray((i[:, None] // HEAD_DIM) == (i[None, :] // HEAD_DIM), BF16)


def _qk_prep_body(x_ref, g_ref, cos_ref, sin_ref, ones_ref, o_ref, *, scale):
    x = x_ref[0]
    hi, lo = _split(x * x)
    ssum = _dot(hi, ones_ref[...]) + _dot(lo, ones_ref[...])
    y = x * lax.rsqrt(ssum * (1.0 / HEAD_DIM) + EPS) * g_ref[...]
    lane = lax.broadcasted_iota(jnp.int32, y.shape, 1)
    quarter = HEAD_DIM // 4
    partner = jnp.where((lane & quarter) == 0, pltpu.roll(y, LANES - quarter, 1), pltpu.roll(y, quarter, 1))
    o_ref[0] = ((y * cos_ref[...] + partner * sin_ref[...]) * scale).astype(o_ref.dtype)


def _qk_prep(ux, col_off, width, gain, tables, scale):
    b, l, _ = ux.shape
    cos_t, sin_t = tables
    c0 = col_off // LANES
    g = jnp.tile(gain.reshape(1, HEAD_DIM), (1, LANES // HEAD_DIM))
    return pl.pallas_call(
        functools.partial(_qk_prep_body, scale=scale),
        grid=(b, width // LANES),
        in_specs=[pl.BlockSpec((1, l, LANES), lambda bi, j: (bi, 0, c0 + j)),
                  pl.BlockSpec((1, LANES), lambda bi, j: (0, 0)),
                  pl.BlockSpec((l, LANES), lambda bi, j: (0, 0)),
                  pl.BlockSpec((l, LANES), lambda bi, j: (0, 0)),
                  pl.BlockSpec((LANES, LANES), lambda bi, j: (0, 0))],
        out_specs=pl.BlockSpec((1, l, LANES), lambda bi, j: (bi, 0, j)),
        out_shape=jax.ShapeDtypeStruct((b, l, width), BF16),
        compiler_params=_params(("parallel", "parallel")),
        name="qk_prep",
    )(ux, g, cos_t, sin_t, _head_ones())


def _cast_body(x_ref, o_ref):
    o_ref[...] = x_ref[...].astype(o_ref.dtype)


def _cast_cols(ux, col_off, width):
    b, l, _ = ux.shape
    c0 = col_off // LANES
    return pl.pallas_call(
        _cast_body,
        grid=(b, width // LANES),
        in_specs=[pl.BlockSpec((1, l, LANES), lambda bi, j: (bi, 0, c0 + j))],
        out_specs=pl.BlockSpec((1, l, LANES), lambda bi, j: (bi, 0, j)),
        out_shape=jax.ShapeDtypeStruct((b, l, width), BF16),
        compiler_params=_params(("parallel", "parallel")),
        name="cast_cols",
    )(ux)


def _nt_dot(a, b):
    return lax.dot_general(a, b, (((1,), (1,)), ((), ())), preferred_element_type=F32)


def _win_attn_body(sink_ref, q_ref, km_ref, k0_ref, kp_ref, vm_ref, v0_ref, vp_ref, kc_ref, vc_ref, o_ref, *, seq):
    i = pl.program_id(1)
    q = q_ref[0]
    kwin = jnp.concatenate([km_ref[0], k0_ref[0], kp_ref[0]], axis=0)
    vwin = jnp.concatenate([vm_ref[0], v0_ref[0], vp_ref[0]], axis=0)
    kc = kc_ref[0]
    vc = vc_ref[0]
    rows = WIN_GROUP * BLOCK
    qi = lax.broadcasted_iota(jnp.int32, (rows, 3 * BLOCK), 0) & (BLOCK - 1)
    kj = lax.broadcasted_iota(jnp.int32, (rows, 3 * BLOCK), 1)
    kpos = (i - 1) * BLOCK + kj
    valid = (jnp.abs(kj - BLOCK - qi) <= WINDOW) & (kpos >= 0) & (kpos < seq)
    outs = []
    for h in range(WIN_KV_HEADS):
        hs = slice(h * HEAD_DIM, (h + 1) * HEAD_DIM)
        heads = [h * WIN_GROUP + g for g in range(WIN_GROUP)]
        qh = jnp.concatenate([q[:, a * HEAD_DIM:(a + 1) * HEAD_DIM] for a in heads], axis=0)
        sk = jnp.concatenate([jnp.full((BLOCK, 1), sink_ref[a], F32) for a in heads], axis=0)
        s_loc = jnp.where(valid, _nt_dot(qh, kwin[:, hs]), NEG_INF)
        s_ctx = _nt_dot(qh, kc[:, hs])
        m = jnp.maximum(jnp.maximum(jnp.max(s_loc, axis=-1, keepdims=True),
                                    jnp.max(s_ctx, axis=-1, keepdims=True)), sk)
        p_loc = jnp.exp(s_loc - m)
        p_ctx = jnp.exp(s_ctx - m)
        denom = (jnp.sum(p_loc, axis=-1, keepdims=True) + jnp.sum(p_ctx, axis=-1, keepdims=True)
                 + jnp.exp(sk - m))
        o = (_dot(p_loc.astype(BF16), vwin[:, hs]) + _dot(p_ctx.astype(BF16), vc[:, hs])) / denom
        outs.extend(o[g * BLOCK:(g + 1) * BLOCK] for g in range(WIN_GROUP))
    o_ref[0] = jnp.concatenate(outs, axis=-1).astype(o_ref.dtype)


def _win_attn(q, k, v, kc, vc, sink):
    b, s, _ = q.shape
    c = kc.shape[1]
    nb = s // BLOCK
    kvw = WIN_KV_HEADS * HEAD_DIM
    prev = lambda bi, i: (bi, jnp.maximum(i - 1, 0), 0)
    cur = lambda bi, i: (bi, i, 0)
    nxt = lambda bi, i: (bi, jnp.minimum(i + 1, nb - 1), 0)
    kvspec = lambda f: pl.BlockSpec((1, BLOCK, kvw), f)
    return pl.pallas_call(
        functools.partial(_win_attn_body, seq=s),
        grid=(b, nb),
        in_specs=[pl.BlockSpec(memory_space=pltpu.SMEM),
                  pl.BlockSpec((1, BLOCK, WIN_HEADS * HEAD_DIM), cur),
                  kvspec(prev), kvspec(cur), kvspec(nxt),
                  kvspec(prev), kvspec(cur), kvspec(nxt),
                  pl.BlockSpec((1, c, kvw), lambda bi, i: (bi, 0, 0)),
                  pl.BlockSpec((1, c, kvw), lambda bi, i: (bi, 0, 0))],
        out_specs=pl.BlockSpec((1, BLOCK, WIN_HEADS * HEAD_DIM), cur),
        out_shape=jax.ShapeDtypeStruct((b, s, WIN_HEADS * HEAD_DIM), BF16),
        compiler_params=_params(("parallel", "parallel")),
        name="win_attn",
    )(sink.astype(F32), q, k, k, k, v, v, v, kc, vc)


def _ctx_win_attn_body(sink_ref, q_ref, k_ref, v_ref, o_ref):
    q = q_ref[0]
    k = k_ref[0]
    v = v_ref[0]
    c = q.shape[0]
    outs = []
    for h in range(WIN_KV_HEADS):
        hs = slice(h * HEAD_DIM, (h + 1) * HEAD_DIM)
        heads = [h * WIN_GROUP + g for g in range(WIN_GROUP)]
        qh = jnp.concatenate([q[:, a * HEAD_DIM:(a + 1) * HEAD_DIM] for a in heads], axis=0)
        sk = jnp.concatenate([jnp.full((c, 1), sink_ref[a], F32) for a in heads], axis=0)
        s = _nt_dot(qh, k[:, hs])
        m = jnp.maximum(jnp.max(s, axis=-1, keepdims=True), sk)
        p = jnp.exp(s - m)
        denom = jnp.sum(p, axis=-1, keepdims=True) + jnp.exp(sk - m)
        o = _dot(p.astype(BF16), v[:, hs]) / denom
        outs.extend(o[g * c:(g + 1) * c] for g in range(WIN_GROUP))
    o_ref[0] = jnp.concatenate(outs, axis=-1).astype(o_ref.dtype)


def _ctx_win_attn(q, k, v, sink):
    b, c, _ = q.shape
    kvw = WIN_KV_HEADS * HEAD_DIM
    return pl.pallas_call(
        _ctx_win_attn_body,
        grid=(b,),
        in_specs=[pl.BlockSpec(memory_space=pltpu.SMEM),
                  pl.BlockSpec((1, c, WIN_HEADS * HEAD_DIM), lambda bi: (bi, 0, 0)),
                  pl.BlockSpec((1, c, kvw), lambda bi: (bi, 0, 0)),
                  pl.BlockSpec((1, c, kvw), lambda bi: (bi, 0, 0))],
        out_specs=pl.BlockSpec((1, c, WIN_HEADS * HEAD_DIM), lambda bi: (bi, 0, 0)),
        out_shape=jax.ShapeDtypeStruct((b, c, WIN_HEADS * HEAD_DIM), BF16),
        compiler_params=_params(("parallel",)),
        name="ctx_win_attn",
    )(sink.astype(F32), q, k, v)


def _diff_attn_body(lam_ref, q_ref, k_ref, v_ref, g_ref, o_ref, *, lam_init):
    q = q_ref[0]
    k = k_ref[0]
    v = v_ref[0]
    lp = lam_ref[...]
    lam = (jnp.exp(jnp.sum(lp[0:1] * lp[1:2], axis=-1, keepdims=True))
           - jnp.exp(jnp.sum(lp[2:3] * lp[3:4], axis=-1, keepdims=True)) + lam_init)
    lane = lax.broadcasted_iota(jnp.int32, q.shape, 1)
    zero = jnp.zeros_like(q)
    s1 = _nt_dot(jnp.where(lane < HEAD_DIM, q, zero), k)
    s2 = _nt_dot(jnp.where(lane < HEAD_DIM, zero, q), k)
    p1 = jnp.exp(s1 - jnp.max(s1, axis=-1, keepdims=True))
    p2 = jnp.exp(s2 - jnp.max(s2, axis=-1, keepdims=True))
    r1 = 1.0 / jnp.sum(p1, axis=-1, keepdims=True)
    r2 = lam / jnp.sum(p2, axis=-1, keepdims=True)
    a = (p1 * r1 - p2 * r2).astype(BF16)
    o = _dot(a, v)
    ms = jnp.mean(o * o, axis=-1, keepdims=True)
    o_ref[0] = (o * lax.rsqrt(ms + EPS) * g_ref[...] * (1.0 - lam_init)).astype(o_ref.dtype)


def _diff_attn(q, k, v, lam_params, subln, lam_init, tq):
    b, nq, w = q.shape
    nk = k.shape[1]
    tq = min(tq, nq)
    hw = 2 * HEAD_DIM
    return pl.pallas_call(
        functools.partial(_diff_attn_body, lam_init=lam_init),
        grid=(b, DIFF_HEADS, nq // tq),
        in_specs=[pl.BlockSpec((4, HEAD_DIM), lambda bi, h, t: (0, 0)),
                  pl.BlockSpec((1, tq, hw), lambda bi, h, t: (bi, t, h)),
                  pl.BlockSpec((1, nk, hw), lambda bi, h, t: (bi, 0, h)),
                  pl.BlockSpec((1, nk, hw), lambda bi, h, t: (bi, 0, h)),
                  pl.BlockSpec((1, hw), lambda bi, h, t: (0, 0))],
        out_specs=pl.BlockSpec((1, tq, hw), lambda bi, h, t: (bi, t, h)),
        out_shape=jax.ShapeDtypeStruct((b, nq, w), BF16),
        compiler_params=_params(("parallel", "parallel", "parallel")),
        name="diff_attn",
    )(lam_params, q, k, v, subln.reshape(1, hw))


def _hy_pre_body(x0_ref, x1_ref, v_ref, w0_ref, w1_ref, wv_ref, b0_ref, b1_ref, bv_ref, z_ref, x0c_ref):
    n = x0_ref.shape[1]
    row = lax.broadcasted_iota(jnp.int32, (n, 1), 0)

    def sconv(u_ref, w_ref, b_ref):
        u = u_ref[0]
        w = w_ref[...]
        up = jnp.where(row == 0, 0.0, pltpu.roll(u, 1, 0))
        un = jnp.where(row == n - 1, 0.0, pltpu.roll(u, n - 1, 0))
        return ((b_ref[...] + up * w[0:1]) + u * w[1:2]) + un * w[2:3]

    x0c_ref[0] = sconv(x0_ref, w0_ref, b0_ref)
    z_ref[0] = sconv(v_ref, wv_ref, bv_ref) * sconv(x1_ref, w1_ref, b1_ref)


def _hy_pre(ux, conv_w, conv_b):
    b, l, _ = ux.shape
    nc = HY_WIDTH // LANES
    c0 = OFF_HY // LANES
    ub = lambda seg: pl.BlockSpec((1, l, LANES), lambda bi, j: (bi, 0, c0 + seg * nc + j))
    wb = lambda seg: pl.BlockSpec((3, LANES), lambda bi, j: (0, seg * nc + j))
    bb = lambda seg: pl.BlockSpec((1, LANES), lambda bi, j: (0, seg * nc + j))
    out = pl.BlockSpec((1, l, LANES), lambda bi, j: (bi, 0, j))
    shp = jax.ShapeDtypeStruct((b, l, HY_WIDTH), F32)
    cb = conv_b.reshape(1, 3 * HY_WIDTH)
    return pl.pallas_call(
        _hy_pre_body,
        grid=(b, nc),
        in_specs=[ub(0), ub(1), ub(2), wb(0), wb(1), wb(2), bb(0), bb(1), bb(2)],
        out_specs=[out, out],
        out_shape=[shp, shp],
        compiler_params=_params(("parallel", "parallel")),
        name="hy_pre",
    )(ux, ux, ux, conv_w, conv_w, conv_w, cb, cb, cb)


def _hy_features(l):
    n = np.arange(l, dtype=np.float64)
    pos = np.concatenate([n, l - n])
    t = pos / max(l - 1, 1)
    w = 2.0 * math.pi * pos / l
    f = np.linspace(1e-4, HY_BANDS - 1, HY_BANDS)
    feat = np.zeros((2 * l, LANES), np.float64)
    feat[:, 0] = t
    feat[:, 1:1 + HY_BANDS] = np.cos(w[:, None] * f)
    feat[:, 1 + HY_BANDS:HY_EMB] = -np.sin(w[:, None] * f)
    feat[:, HY_EMB] = 1.0
    feat[l, HY_EMB] = 0.0
    return jnp.asarray(feat, F32)


def _hy_filter_body(f_ref, w1_ref, b1_ref, w2_ref, b2_ref, w3_ref, fr_ref, dec_ref, k_ref, sum_ref):
    first = (pl.program_id(0) == 0) & (pl.program_id(1) == 0)

    @pl.when(first)
    def _():
        sum_ref[...] = jnp.zeros_like(sum_ref)

    f = f_ref[...]
    fr = fr_ref[...]
    h = jnp.sin(fr[0:1] * (_dot3(f, w1_ref[...]) + b1_ref[...]))
    h = jnp.sin(fr[1:2] * (_dot3(h, w2_ref[...]) + b2_ref[...]))
    h = _dot3(h, w3_ref[...]) * jnp.exp(-f[:, 0:1] * jnp.abs(dec_ref[...]))
    h = h * f[:, HY_EMB:HY_EMB + 1]
    k_ref[...] = h
    sum_ref[...] += jnp.sum(jnp.abs(h), axis=0, keepdims=True)


def _hy_filter(l, p):
    tl = min(l, 1024)
    nt = l // tl
    order = p['hy_w1'].shape[1]
    w1 = jnp.zeros((LANES, order), F32).at[:HY_EMB].set(p['hy_w1'])
    const = lambda d, i: (0, 0)
    return pl.pallas_call(
        _hy_filter_body,
        grid=(2, nt),
        in_specs=[pl.BlockSpec((tl, LANES), lambda d, i: (d * nt + i, 0)),
                  pl.BlockSpec((LANES, order), const),
                  pl.BlockSpec((1, order), const),
                  pl.BlockSpec((order, order), const),
                  pl.BlockSpec((1, order), const),
                  pl.BlockSpec((order, HY_WIDTH), lambda d, i: (0, d)),
                  pl.BlockSpec((2, order), const),
                  pl.BlockSpec((1, HY_WIDTH), lambda d, i: (0, d))],
        out_specs=[pl.BlockSpec((tl, HY_WIDTH), lambda d, i: (d * nt + i, 0)),
                   pl.BlockSpec((1, HY_WIDTH), const)],
        out_shape=[jax.ShapeDtypeStruct((2 * l, HY_WIDTH), F32), jax.ShapeDtypeStruct((1, HY_WIDTH), F32)],
        compiler_params=_params(("arbitrary", "arbitrary")),
        name="hy_filter",
    )(_hy_features(l), w1, p['hy_b1'].reshape(1, order), p['hy_w2'], p['hy_b2'].reshape(1, order),
      p['hy_w3'], p['hy_freq'], p['hy_decay'].reshape(1, 2 * HY_WIDTH))


def _fft_consts(n_la):
    ka = np.arange(FFT_NA)[:, None]
    la = np.arange(n_la)[None, :]
    ang = 2.0 * math.pi * ka * la / FFT_NA
    f1 = np.concatenate([np.cos(ang), -np.sin(ang)], axis=0)
    lb = np.arange(FFT_NB)[None, :]
    th = 2.0 * math.pi * ka * lb / FFT_N
    tc, ts = np.cos(th), np.sin(th)
    a2 = 2.0 * math.pi * np.arange(FFT_NB)[:, None] * np.arange(FFT_NB)[None, :] / FFT_NB
    gre, gim = np.cos(a2), -np.sin(a2)
    g2 = np.block([[gre, gim], [-gim, gre]])
    g2i = np.block([[gre, -gim], [gim, gre]])
    lo = np.arange(FFT_NA // 2)[:, None]
    ph = 2.0 * math.pi * lo * np.arange(FFT_NA)[None, :] / FFT_NA
    f1i = np.concatenate([np.cos(ph), -np.sin(ph)], axis=1) / FFT_N
    return f1, tc, ts, g2, g2i, f1i


def _fft_fwd(x_ref, f1_ref, tc_ref, ts_ref, g2_ref, p2_ref, cc, dot):
    rhs = jnp.concatenate([x_ref[0, c] for c in range(cc)], axis=1)
    a = dot(f1_ref[...], rhs.astype(f1_ref.dtype))
    tc = tc_ref[...]
    ts = ts_ref[...]
    for c in range(cc):
        are = a[0:FFT_NA, c * FFT_NB:(c + 1) * FFT_NB]
        aim = a[FFT_NA:2 * FFT_NA, c * FFT_NB:(c + 1) * FFT_NB]
        p2_ref[c * FFT_NA:(c + 1) * FFT_NA, 0:FFT_NB] = (are * tc + aim * ts).astype(p2_ref.dtype)
        p2_ref[c * FFT_NA:(c + 1) * FFT_NA, FFT_NB:2 * FFT_NB] = (aim * tc - are * ts).astype(p2_ref.dtype)
    return dot(p2_ref[...], g2_ref[...])


def _fft_spec_body(x_ref, f1_ref, tc_ref, ts_ref, g2_ref, o_ref, p2_ref, *, cc):
    o_ref[...] = _fft_fwd(x_ref, f1_ref, tc_ref, ts_ref, g2_ref, p2_ref, cc, _dot3)


def _fft_conv_body(x_ref, kf_ref, f1_ref, tc_ref, ts_ref, g2_ref, g2i_ref, f1i_ref, o_ref, p2_ref, r2_ref, *, cc):
    z = _fft_fwd(x_ref, f1_ref, tc_ref, ts_ref, g2_ref, p2_ref, cc, _dot)
    kf = kf_ref[...]
    zre, zim = z[:, :FFT_NB], z[:, FFT_NB:]
    kre, kim = kf[:, :FFT_NB], kf[:, FFT_NB:]
    y = jnp.concatenate([zre * kre - zim * kim, zre * kim + zim * kre], axis=1).astype(BF16)
    u = _dot(y, g2i_ref[...])
    tc = tc_ref[...]
    ts = ts_ref[...]
    for c in range(cc):
        ure = u[c * FFT_NA:(c + 1) * FFT_NA, 0:FFT_NB]
        uim = u[c * FFT_NA:(c + 1) * FFT_NA, FFT_NB:2 * FFT_NB]
        r2_ref[0:FFT_NA, c * FFT_NB:(c + 1) * FFT_NB] = (ure * tc - uim * ts).astype(BF16)
        r2_ref[FFT_NA:2 * FFT_NA, c * FFT_NB:(c + 1) * FFT_NB] = (ure * ts + uim * tc).astype(BF16)
    out = _dot(f1i_ref[...], r2_ref[...])
    for c in range(cc):
        o_ref[0, c] = out[:, c * FFT_NB:(c + 1) * FFT_NB]


FFT_CC = 32


def _fft_spectrum(kt):
    nch, n_la, _ = kt.shape
    cc = FFT_CC
    f1, tc, ts, g2, _, _ = _fft_consts(n_la)
    const = lambda j: (0, 0)
    return pl.pallas_call(
        functools.partial(_fft_spec_body, cc=cc),
        grid=(nch // cc,),
        in_specs=[pl.BlockSpec((1, cc, n_la, FFT_NB), lambda j: (0, j, 0, 0)),
                  pl.BlockSpec((2 * FFT_NA, n_la), const),
                  pl.BlockSpec((FFT_NA, FFT_NB), const),
                  pl.BlockSpec((FFT_NA, FFT_NB), const),
                  pl.BlockSpec((2 * FFT_NB, 2 * FFT_NB), const)],
        out_specs=pl.BlockSpec((cc * FFT_NA, 2 * FFT_NB), lambda j: (j, 0)),
        out_shape=jax.ShapeDtypeStruct((nch * FFT_NA, 2 * FFT_NB), F32),
        scratch_shapes=[pltpu.VMEM((cc * FFT_NA, 2 * FFT_NB), F32)],
        compiler_params=_params(("parallel",)),
        name="fft_spectrum",
    )(kt[None], jnp.asarray(f1, F32), jnp.asarray(tc, F32), jnp.asarray(ts, F32), jnp.asarray(g2, F32))


def _fft_conv(zt, kf):
    b, nch, n_la, _ = zt.shape
    cc = FFT_CC
    f1, tc, ts, g2, g2i, f1i = _fft_consts(n_la)
    const = lambda j, bi: (0, 0)
    return pl.pallas_call(
        functools.partial(_fft_conv_body, cc=cc),
        grid=(nch // cc, b),
        in_specs=[pl.BlockSpec((1, cc, n_la, FFT_NB), lambda j, bi: (bi, j, 0, 0)),
                  pl.BlockSpec((cc * FFT_NA, 2 * FFT_NB), lambda j, bi: (j, 0)),
                  pl.BlockSpec((2 * FFT_NA, n_la), const),
                  pl.BlockSpec((FFT_NA, FFT_NB), const),
                  pl.BlockSpec((FFT_NA, FFT_NB), const),
                  pl.BlockSpec((2 * FFT_NB, 2 * FFT_NB), const),
                  pl.BlockSpec((2 * FFT_NB, 2 * FFT_NB), const),
                  pl.BlockSpec((FFT_NA // 2, 2 * FFT_NA), const)],
        out_specs=pl.BlockSpec((1, cc, n_la, FFT_NB), lambda j, bi: (bi, j, 0, 0)),
        out_shape=jax.ShapeDtypeStruct((b, nch, n_la, FFT_NB), F32),
        scratch_shapes=[pltpu.VMEM((cc * FFT_NA, 2 * FFT_NB), BF16),
                        pltpu.VMEM((2 * FFT_NA, cc * FFT_NB), BF16)],
        compiler_params=_params(("parallel", "parallel")),
        name="fft_conv",
    )(zt, kf, jnp.asarray(f1, BF16), jnp.asarray(tc, F32), jnp.asarray(ts, F32),
      jnp.asarray(g2, BF16), jnp.asarray(g2i, BF16), jnp.asarray(f1i, BF16))


def _dft_conv_body(z_ref, k_ref, fw_ref, fi_ref, o_ref):
    l = z_ref.shape[1]
    fw = fw_ref[...]
    n = fw.shape[1]
    kf = _dot3(fw, k_ref[...])
    zf = _dot3(fw[:, :l], z_ref[0])
    kre, kim = kf[:n], kf[n:]
    zre, zim = zf[:n], zf[n:]
    y = jnp.concatenate([zre * kre - zim * kim, zre * kim + zim * kre], axis=0)
    o_ref[0] = _dot3(fi_ref[...], y)


def _dft_conv(z, k):
    b, l, w = z.shape
    n = 2 * l
    ang = 2.0 * math.pi * np.arange(n)[:, None] * np.arange(n)[None, :] / n
    fw = np.concatenate([np.cos(ang), -np.sin(ang)], axis=0)
    ai = ang[:l]
    fi = np.concatenate([np.cos(ai), -np.sin(ai)], axis=1) / n
    return pl.pallas_call(
        _dft_conv_body,
        grid=(b,),
        in_specs=[pl.BlockSpec((1, l, w), lambda bi: (bi, 0, 0)),
                  pl.BlockSpec((n, w), lambda bi: (0, 0)),
                  pl.BlockSpec((2 * n, n), lambda bi: (0, 0)),
                  pl.BlockSpec((l, 2 * n), lambda bi: (0, 0))],
        out_specs=pl.BlockSpec((1, l, w), lambda bi: (bi, 0, 0)),
        out_shape=jax.ShapeDtypeStruct((b, l, w), F32),
        compiler_params=_params(("parallel",)),
        name="dft_conv",
    )(z, k, jnp.asarray(fw, F32), jnp.asarray(fi, F32))


def _hy_post_body(y_ref, z_ref, x0_ref, ks_ref, sk_ref, o_ref):
    y = y_ref[0] / (ks_ref[...] + EPS) + sk_ref[...] * z_ref[0]
    o_ref[0] = (y * x0_ref[0]).astype(o_ref.dtype)


def _hy_post(yraw, z, x0c, ksum, skip):
    b, l, w = z.shape
    ts = min(l, 1024)
    blk = pl.BlockSpec((1, ts, w), lambda bi, i: (bi, i, 0))
    vec = pl.BlockSpec((1, w), lambda bi, i: (0, 0))
    return pl.pallas_call(
        _hy_post_body,
        grid=(b, l // ts),
        in_specs=[blk, blk, blk, vec, vec],
        out_specs=blk,
        out_shape=jax.ShapeDtypeStruct((b, l, w), BF16),
        compiler_params=_params(("parallel", "parallel")),
        name="hy_post",
    )(yraw, z, x0c, ksum, skip.reshape(1, w))


def _hyena(ux, p):
    b, l, _ = ux.shape
    z, x0c = _hy_pre(ux, p['hy_conv_w'], p['hy_conv_b'])
    k_raw, ksum = _hy_filter(l, p)
    if 2 * l == FFT_N:
        kt = k_raw.T.reshape(HY_WIDTH, FFT_NA, FFT_NB)
        kf = _fft_spectrum(kt)
        zt = jnp.swapaxes(z, 1, 2).reshape(b, HY_WIDTH, l // FFT_NB, FFT_NB)
        yt = _fft_conv(zt, kf)
        yraw = jnp.swapaxes(yt.reshape(b, HY_WIDTH, l), 1, 2)
    else:
        yraw = _dft_conv(z, k_raw)
    return _hy_post(yraw, z, x0c, ksum, p['hy_skip'])


def _pool_body(u_ref, w_ref, sc_ref, o_ref, *, win):
    u = u_ref[0]
    n = u.shape[0]
    lo = win // 2
    hi = win - 1 - lo
    row = lax.broadcasted_iota(jnp.int32, (n, 1), 0)
    acc = u
    for j in range(1, lo + 1):
        acc = acc + jnp.where(row >= j, pltpu.roll(u, j, 0), 0.0)
    for j in range(1, hi + 1):
        acc = acc + jnp.where(row < n - j, pltpu.roll(u, n - j, 0), 0.0)
    cnt = (jnp.minimum(row + hi + 1, n) - jnp.maximum(row - lo, 0)).astype(F32)
    d = acc / cnt - u
    o_ref[0] = (_dot(d.astype(BF16), w_ref[0].astype(BF16)) * sc_ref[...]).astype(o_ref.dtype)


def _pool_mix(ux, w_grp, scale):
    b, l, _ = ux.shape
    c0 = OFF_PL // LANES
    outs = []
    for gi, win in enumerate(POOL_WINDOWS):
        outs.append(pl.pallas_call(
            functools.partial(_pool_body, win=win),
            grid=(b,),
            in_specs=[pl.BlockSpec((1, l, LANES), lambda bi, gi=gi: (bi, 0, c0 + gi)),
                      pl.BlockSpec((1, LANES, LANES), lambda bi, gi=gi: (gi, 0, 0)),
                      pl.BlockSpec((1, LANES), lambda bi, gi=gi: (0, gi))],
            out_specs=pl.BlockSpec((1, l, LANES), lambda bi: (bi, 0, 0)),
            out_shape=jax.ShapeDtypeStruct((b, l, LANES), BF16),
            compiler_params=_params(("parallel",)),
            name=f"pool_mix{win}",
        )(ux, w_grp, scale.reshape(1, -1)))
    return jnp.concatenate(outs, axis=-1)


def _merge1_body(hx_ref, y0, y1, y2, y3, g0, g1, g2, g3, b0, b1, b2, b3, o_ref):
    hx = hx_ref[...]
    acc = None
    for y_ref, g_ref, b_ref in ((y0, g0, b0), (y1, g1, b1), (y2, g2, b2), (y3, g3, b3)):
        t = jax.nn.sigmoid(_dot(hx, g_ref[...])) * _dot(y_ref[...], b_ref[0])
        acc = t if acc is None else acc + t
    o_ref[...] = acc.astype(o_ref.dtype)


def _merge1(hx, ys, w_gate, w_branch, tm=512, tn=512):
    m, d = hx.shape
    tm = min(tm, m)
    nj = d // tn
    row = lambda i, j: (i, 0)
    gspec = lambda br: pl.BlockSpec((d, tn), lambda i, j, br=br: (0, br * nj + j))
    bspec = lambda br: pl.BlockSpec((1, BRANCH_W, tn), lambda i, j, br=br: (br, 0, j))
    return pl.pallas_call(
        _merge1_body,
        grid=(m // tm, nj),
        in_specs=([pl.BlockSpec((tm, d), row)] + [pl.BlockSpec((tm, BRANCH_W), row)] * N_BRANCH
                  + [gspec(br) for br in range(N_BRANCH)] + [bspec(br) for br in range(N_BRANCH)]),
        out_specs=pl.BlockSpec((tm, tn), lambda i, j: (i, j)),
        out_shape=jax.ShapeDtypeStruct((m, d), BF16),
        compiler_params=_params(("parallel", "parallel")),
        name="merge_gate",
    )(hx, *ys, w_gate, w_gate, w_gate, w_gate, w_branch, w_branch, w_branch, w_branch)


def _merge2_body(a_ref, w_ref, x_ref, g_ref, o_ref):
    o_ref[0] = x_ref[0] + g_ref[0] * _dot(a_ref[0], w_ref[...])


def _merge2(acc, w_out, x, gate, tm=512):
    b, l, d = x.shape
    tm = min(tm, l)
    blk = pl.BlockSpec((1, tm, d), lambda bi, i: (bi, i, 0))
    return pl.pallas_call(
        _merge2_body,
        grid=(b, l // tm),
        in_specs=[blk, pl.BlockSpec((d, d), lambda bi, i: (0, 0)), blk,
                  pl.BlockSpec((1, 1, d), lambda bi, i: (bi, 0, 0))],
        out_specs=blk,
        out_shape=jax.ShapeDtypeStruct((b, l, d), F32),
        compiler_params=_params(("parallel", "parallel")),
        name="merge_out",
    )(acc, w_out, x, gate)


def _route_body(x_ref, g_ref, sc_ref, sh_ref, wr_ref, br_ref, h_ref, e_ref, w_ref):
    x = x_ref[0]
    ms = jnp.mean(x * x, axis=-1, keepdims=True)
    h = (x * lax.rsqrt(ms + EPS) * g_ref[...]) * (1.0 + sc_ref[0]) + sh_ref[0]
    h_ref[0] = h
    logits = _dot3(h, wr_ref[...]) + br_ref[...]
    lane = lax.broadcasted_iota(jnp.int32, logits.shape, 1)
    big = jnp.int32(LANES)
    gl = jnp.where(lane < N_GROUPS, logits, -jnp.inf)
    ge = jnp.exp(gl - jnp.max(gl, axis=-1, keepdims=True))
    pg = ge / jnp.sum(ge, axis=-1, keepdims=True)
    pg_top = jnp.max(pg, axis=-1, keepdims=True)
    g_idx = jnp.min(jnp.where((pg == pg_top) & (lane < N_GROUPS), lane, big), axis=-1, keepdims=True)
    first = N_GROUPS + EXPERTS_PER_GROUP * g_idx
    le = jnp.where((lane >= first) & (lane < first + EXPERTS_PER_GROUP), logits, -jnp.inf)
    v1 = jnp.max(le, axis=-1, keepdims=True)
    i1 = jnp.min(jnp.where(le == v1, lane, big), axis=-1, keepdims=True)
    le2 = jnp.where(lane == i1, -jnp.inf, le)
    v2 = jnp.max(le2, axis=-1, keepdims=True)
    i2 = jnp.min(jnp.where(le2 == v2, lane, big), axis=-1, keepdims=True)
    e2 = jnp.exp(v2 - v1)
    den = 1.0 + e2
    e_ref[0] = jnp.where(lane == 0, i1 - N_GROUPS, jnp.where(lane == 1, i2 - N_GROUPS, 0))
    w_ref[0] = jnp.where(lane == 0, (1.0 / den) * pg_top, jnp.where(lane == 1, (e2 / den) * pg_top, 0.0))


def _route(x, gain, scale, shift, w_router, b_router, tm=512):
    b, l, d = x.shape
    tm = min(tm, l)
    blk = pl.BlockSpec((1, tm, d), lambda bi, i: (bi, i, 0))
    mod = pl.BlockSpec((1, 1, d), lambda bi, i: (bi, 0, 0))
    sm = pl.BlockSpec((1, tm, LANES), lambda bi, i: (bi, i, 0))
    return pl.pallas_call(
        _route_body,
        grid=(b, l // tm),
        in_specs=[blk, pl.BlockSpec((1, d), lambda bi, i: (0, 0)), mod, mod,
                  pl.BlockSpec((d, LANES), lambda bi, i: (0, 0)),
                  pl.BlockSpec((1, LANES), lambda bi, i: (0, 0))],
        out_specs=[blk, sm, sm],
        out_shape=[jax.ShapeDtypeStruct((b, l, d), F32),
                   jax.ShapeDtypeStruct((b, l, LANES), jnp.int32),
                   jax.ShapeDtypeStruct((b, l, LANES), F32)],
        compiler_params=_params(("parallel", "parallel")),
        name="route",
    )(x, gain.reshape(1, d), scale, shift, w_router, b_router)


def _moe_body(blk_e_ref, nused_ref, gi_ref, gn_ref, si_ref, sw_ref, w1_ref, w3_ref, w2_ref, h_hbm, y_hbm,
              xbuf, obuf, gsem, ssem):
    i = pl.program_id(0)
    n = pl.num_programs(0)
    tm = xbuf.shape[1]
    slot = i % 2

    def gather(idx_ref, s):
        for r in range(tm):
            pltpu.make_async_copy(h_hbm.at[pl.ds(idx_ref[0, 0, r], 1)], xbuf.at[s, pl.ds(r, 1)], gsem.at[s]).start()

    def wait_scatter(s):
        pltpu.make_async_copy(obuf.at[s], y_hbm.at[pl.ds(0, tm)], ssem.at[s]).wait()

    @pl.when(i == 0)
    def _():
        gather(gi_ref, 0)

    @pl.when(i + 1 < n)
    def _():
        gather(gn_ref, 1 - slot)

    pltpu.make_async_copy(h_hbm.at[pl.ds(0, tm)], xbuf.at[slot], gsem.at[slot]).wait()

    @pl.when(i >= 2)
    def _():
        wait_scatter(slot)

    @pl.when(i < nused_ref[0])
    def _():
        x = xbuf[slot].astype(BF16)
        a = _dot(x, w1_ref[0])
        h = (a * jax.nn.sigmoid(a)) * _dot(x, w3_ref[0])
        obuf[slot] = _dot(h.astype(BF16), w2_ref[0]) * sw_ref[...]

    @pl.when(i >= nused_ref[0])
    def _():
        obuf[slot] = jnp.zeros(obuf.shape[1:], F32)

    for r in range(tm):
        pltpu.make_async_copy(obuf.at[slot, pl.ds(r, 1)], y_hbm.at[pl.ds(si_ref[0, 0, r], 1)], ssem.at[slot]).start()

    @pl.when(i == n - 1)
    def _():
        wait_scatter(slot)

        @pl.when(n >= 2)
        def _():
            wait_scatter(1 - slot)


def _moe_experts(h2, blk_e, nused, gidx, sidx, slot_w, w1, w3, w2, n_out):
    nblk, tm = gidx.shape
    d = h2.shape[1]
    ff = w1.shape[2]
    gidx3 = gidx.reshape(nblk, 1, tm)
    sidx3 = sidx.reshape(nblk, 1, tm)
    smem_blk = lambda f: pl.BlockSpec((1, 1, tm), f, memory_space=pltpu.SMEM)
    grid_spec = pltpu.PrefetchScalarGridSpec(
        num_scalar_prefetch=2,
        grid=(nblk,),
        in_specs=[smem_blk(lambda i, be, nu: (i, 0, 0)),
                  smem_blk(lambda i, be, nu: (jnp.minimum(i + 1, nblk - 1), 0, 0)),
                  smem_blk(lambda i, be, nu: (i, 0, 0)),
                  pl.BlockSpec((tm, 1), lambda i, be, nu: (i, 0)),
                  pl.BlockSpec((1, d, ff), lambda i, be, nu: (be[i], 0, 0)),
                  pl.BlockSpec((1, d, ff), lambda i, be, nu: (be[i], 0, 0)),
                  pl.BlockSpec((1, ff, d), lambda i, be, nu: (be[i], 0, 0)),
                  pl.BlockSpec(memory_space=pl.ANY)],
        out_specs=pl.BlockSpec(memory_space=pl.ANY),
        scratch_shapes=[pltpu.VMEM((2, tm, d), F32), pltpu.VMEM((2, tm, d), F32),
                        pltpu.SemaphoreType.DMA((2,)), pltpu.SemaphoreType.DMA((2,))],
    )
    return pl.pallas_call(
        _moe_body,
        grid_spec=grid_spec,
        out_shape=jax.ShapeDtypeStruct((n_out, d), F32),
        compiler_params=_params(("arbitrary",), 56),
        name="moe_experts",
    )(blk_e, nused, gidx3, gidx3, sidx3, slot_w, w1, w3, w2, h2)


def _moe_plan(e_idx, e_w, tm):
    n = e_idx.shape[0]
    m = n * TOP_K
    e_flat = e_idx.reshape(m)
    w_flat = e_w.reshape(m)
    order = jnp.argsort(e_flat).astype(jnp.int32)
    e_s = e_flat[order]
    counts = jnp.bincount(e_flat, length=N_EXPERTS)
    padded = (counts + tm - 1) // tm * tm
    pad_end = jnp.cumsum(padded)
    pad_start = pad_end - padded
    cnt_start = jnp.cumsum(counts) - counts
    dest = pad_start[e_s] + jnp.arange(m) - cnt_start[e_s]
    nblk = -(-(m + N_EXPERTS * (tm - 1)) // tm)
    p = nblk * tm
    slot_src = jnp.full((p,), -1, jnp.int32).at[dest].set(order)
    slot_w = jnp.zeros((p,), F32).at[dest].set(w_flat[order])
    blk_e = jnp.minimum(jnp.searchsorted(pad_end, jnp.arange(nblk) * tm, side='right'), N_EXPERTS - 1)
    valid = slot_src >= 0
    tok = jnp.where(valid, slot_src // TOP_K, 0)
    pad_rank = jnp.cumsum(jnp.logical_not(valid).astype(jnp.int32)) - 1
    dst = jnp.where(valid, (slot_src % TOP_K) * n + slot_src // TOP_K, m + pad_rank)
    nused = (pad_end[-1] // tm).astype(jnp.int32).reshape(1)
    return (blk_e.astype(jnp.int32), nused, tok.reshape(nblk, tm).astype(jnp.int32),
            dst.reshape(nblk, tm).astype(jnp.int32), slot_w.reshape(p, 1))


def _combine_body(x_ref, y0_ref, y1_ref, g_ref, o_ref):
    o_ref[0] = x_ref[0] + g_ref[0] * (y0_ref[...] + y1_ref[...])


def _moe_combine(x, yk, gate, row_off, n_tok, tm=256):
    b, l, d = x.shape
    tm = min(tm, l)
    assert l % tm == 0 and row_off % tm == 0 and n_tok % tm == 0
    nl = l // tm
    blk = pl.BlockSpec((1, tm, d), lambda bi, i: (bi, i, 0))
    o0 = row_off // tm
    o1 = (n_tok + row_off) // tm
    return pl.pallas_call(
        _combine_body,
        grid=(b, nl),
        in_specs=[blk,
                  pl.BlockSpec((tm, d), lambda bi, i: (o0 + bi * nl + i, 0)),
                  pl.BlockSpec((tm, d), lambda bi, i: (o1 + bi * nl + i, 0)),
                  pl.BlockSpec((1, 1, d), lambda bi, i: (bi, 0, 0))],
        out_specs=blk,
        out_shape=jax.ShapeDtypeStruct((b, l, d), F32),
        compiler_params=_params(("parallel", "parallel")),
        name="moe_combine",
    )(x, yk, yk, gate)


def _mixers(ux, ucx, p, lam_init, latent):
    b, l, _ = ux.shape
    rope_q = _rope_tables(l, latent)
    no_rope = _rope_tables(ucx.shape[1], False)
    scale = HEAD_DIM ** -0.5
    qw = _qk_prep(ux, OFF_WQ, 512, p['win_qnorm'], rope_q, scale)
    qd = _qk_prep(ux, OFF_DQ, 512, p['diff_qnorm'], rope_q, scale)
    kcw = _qk_prep(ucx, OFF_WK, 128, p['win_knorm'], no_rope, 1.0)
    vcw = _cast_cols(ucx, OFF_WV, 128)
    kcd = _qk_prep(ucx, OFF_DK, 512, p['diff_knorm'], no_rope, 1.0)
    vcd = _cast_cols(ucx, OFF_DV, 512)
    if latent:
        kw = _qk_prep(ux, OFF_WK, 128, p['win_knorm'], rope_q, 1.0)
        vw = _cast_cols(ux, OFF_WV, 128)
        kd = _qk_prep(ux, OFF_DK, 512, p['diff_knorm'], rope_q, 1.0)
        vd = _cast_cols(ux, OFF_DV, 512)
        y_win = _win_attn(qw, kw, vw, kcw, vcw, p['win_sink'])
        k_all = jnp.concatenate([kd, kcd], axis=1)
        v_all = jnp.concatenate([vd, vcd], axis=1)
    else:
        y_win = _ctx_win_attn(qw, kcw, vcw, p['win_sink'])
        k_all, v_all = kcd, vcd
    y_diff = _diff_attn(qd, k_all, v_all, p['diff_lambda'], p['diff_subln'], lam_init, 256)
    y_hy = _hyena(ux, p)
    y_pool = _pool_mix(ux, p['pool_proj'], p['pool_scale'])
    return y_win, y_diff, y_hy, y_pool


def _layer(x, ctx, mod, p, layer_idx, update_ctx):
    b, s, d = x.shape
    c = ctx.shape[1]
    lam_init = 0.8 - 0.6 * math.exp(-0.3 * layer_idx)
    chunks = [mod[:, i * d:(i + 1) * d] for i in range(6)]
    sh1, sc1, g1, sh2, sc2, g2 = [t[:b].reshape(b, 1, d) for t in chunks]
    csh1, csc1, cg1, csh2, csc2, cg2 = [jnp.broadcast_to(t[b:b + 1].reshape(1, 1, d), (b, 1, d)) for t in chunks]

    w_small = p['w_in'][:, :OFF_GT].astype(BF16)
    w_gate = p['w_in'][:, OFF_GT:].astype(BF16)
    w_branch = p['w_branch'].astype(BF16)
    w_out = p['w_out'].astype(BF16)
    tn_in = OFF_GT // 2

    hx = _prenorm(x, p['norm1'], sc1, sh1)
    hc = _prenorm(ctx, p['norm1'], csc1, csh1)
    ux = _matmul(hx.reshape(b * s, d), w_small, 512, tn_in).reshape(b, s, OFF_GT)
    ucx = _matmul(hc.reshape(b * c, d), w_small, 512, tn_in).reshape(b, c, OFF_GT)

    ys = _mixers(ux, ucx, p, lam_init, True)
    acc = _merge1(hx.reshape(b * s, d), [y.reshape(b * s, BRANCH_W) for y in ys], w_gate, w_branch)
    x = _merge2(acc.reshape(b, s, d), w_out, x, g1)
    if update_ctx:
        ycs = _mixers(ucx, ucx, p, lam_init, False)
        acc_c = _merge1(hc.reshape(b * c, d), [y.reshape(b * c, BRANCH_W) for y in ycs], w_gate, w_branch)
        ctx = _merge2(acc_c.reshape(b, c, d), w_out, ctx, cg1)

    w_router = jnp.zeros((d, LANES), F32).at[:, :N_GROUPS].set(p['r_group_w'])
    w_router = w_router.at[:, N_GROUPS:N_GROUPS + N_EXPERTS].set(p['r_expert_w'])
    b_router = jnp.zeros((1, LANES), F32).at[0, :N_GROUPS].set(p['r_group_b'])
    b_router = b_router.at[0, N_GROUPS:N_GROUPS + N_EXPERTS].set(p['r_expert_b'])
    h2, e_idx, e_w = _route(x, p['norm2'], sc2, sh2, w_router, b_router)
    h2 = h2.reshape(b * s, d)
    e_idx = e_idx.reshape(b * s, LANES)[:, :TOP_K]
    e_w = e_w.reshape(b * s, LANES)[:, :TOP_K]
    if update_ctx:
        h2c, ec_idx, ec_w = _route(ctx, p['norm2'], csc2, csh2, w_router, b_router)
        h2 = jnp.concatenate([h2, h2c.reshape(b * c, d)], axis=0)
        e_idx = jnp.concatenate([e_idx, ec_idx.reshape(b * c, LANES)[:, :TOP_K]], axis=0)
        e_w = jnp.concatenate([e_w, ec_w.reshape(b * c, LANES)[:, :TOP_K]], axis=0)
    n_tok = h2.shape[0]
    tm = MOE_TM
    blk_e, nused, gidx, sidx, slot_w = _moe_plan(e_idx, e_w, tm)
    yk = _moe_experts(h2, blk_e, nused, gidx, sidx, slot_w, p['e_w1'].astype(BF16), p['e_w3'].astype(BF16),
                      p['e_w2'].astype(BF16), gidx.size)
    x = _moe_combine(x, yk, g2, 0, n_tok)
    if update_ctx:
        ctx = _moe_combine(ctx, yk, cg2, b * s, n_tok)
    return x, ctx


def kernel(x, c, ctx, c_ctx, w_mod, b_mod, norm1, norm2, w_in, win_sink, win_qnorm, win_knorm, diff_qnorm, diff_knorm, diff_lambda, diff_subln, hy_conv_w, hy_conv_b, hy_w1, hy_b1, hy_w2, hy_b2, hy_w3, hy_freq, hy_decay, hy_skip, pool_proj, pool_scale, w_branch, w_out, r_group_w, r_group_b, r_expert_w, r_expert_b, e_w1, e_w3, e_w2):
    depth = w_mod.shape[0]
    b, _, d = x.shape
    cs = jnp.zeros((8, d), F32).at[:b].set(c).at[b].set(c_ctx)
    for l in range(depth):
        p = dict(w_mod=w_mod[l], b_mod=b_mod[l], norm1=norm1[l], norm2=norm2[l], w_in=w_in[l],
                 win_sink=win_sink[l], win_qnorm=win_qnorm[l], win_knorm=win_knorm[l],
                 diff_qnorm=diff_qnorm[l], diff_knorm=diff_knorm[l], diff_lambda=diff_lambda[l],
                 diff_subln=diff_subln[l], hy_conv_w=hy_conv_w[l], hy_conv_b=hy_conv_b[l],
                 hy_w1=hy_w1[l], hy_b1=hy_b1[l], hy_w2=hy_w2[l], hy_b2=hy_b2[l], hy_w3=hy_w3[l],
                 hy_freq=hy_freq[l], hy_decay=hy_decay[l], hy_skip=hy_skip[l],
                 pool_proj=pool_proj[l], pool_scale=pool_scale[l], w_branch=w_branch[l], w_out=w_out[l],
                 r_group_w=r_group_w[l], r_group_b=r_group_b[l], r_expert_w=r_expert_w[l],
                 r_expert_b=r_expert_b[l], e_w1=e_w1[l], e_w3=e_w3[l], e_w2=e_w2[l])
        mod = _modulation(cs, p['w_mod'], p['b_mod'])
        x, ctx = _layer(x, ctx, mod, p, l, l < depth - 1)
    return x
```

```python
import functools
import math

import jax
import jax.numpy as jnp
import numpy as np
from jax import lax
from jax.experimental import pallas as pl
from jax.experimental.pallas import tpu as pltpu

F32 = jnp.float32
BF16 = jnp.bfloat16

GRID_W = 64
HEAD_DIM = 64
ROPE_THETA = 10000.0
EPS = 1e-6
NEG_INF = -1e30
BLOCK = 128
WINDOW = 128
LANES = 128

WIN_HEADS = 8
WIN_KV_HEADS = 2
WIN_GROUP = WIN_HEADS // WIN_KV_HEADS
DIFF_HEADS = 4
HY_WIDTH = 512
HY_EMB = 33
HY_BANDS = 16
POOL_WINDOWS = (2, 4, 8, 16)
N_BRANCH = 4
BRANCH_W = 512

OFF_WQ = 0
OFF_WK = 512
OFF_WV = 640
OFF_DQ = 768
OFF_DK = 1280
OFF_DV = 1792
OFF_HY = 2304
OFF_PL = 3840
OFF_GT = 4352

N_GROUPS = 4
EXPERTS_PER_GROUP = 8
N_EXPERTS = 32
TOP_K = 2
MOE_TM = 256

FFT_NA = 64
FFT_NB = 128
FFT_N = FFT_NA * FFT_NB


def _params(sem, mib=48):
    return pltpu.CompilerParams(dimension_semantics=sem, vmem_limit_bytes=mib * 2**20)


def _dot(a, b):
    return jnp.dot(a, b, preferred_element_type=F32)


def _split(a):
    hi = a.astype(BF16)
    lo = (a - hi.astype(F32)).astype(BF16)
    return hi, lo


def _dot3(a, b):
    ah, al = _split(a)
    bh, bl = _split(b)
    return _dot(ah, bh) + (_dot(ah, bl) + _dot(al, bh))


def _mod_body(c_ref, w_ref, b_ref, o_ref):
    c = c_ref[...]
    a = (c * jax.nn.sigmoid(c)).astype(BF16)
    o_ref[...] = _dot(a, w_ref[0].astype(BF16)) + b_ref[0]


def _modulation(cs, w_mod, b_mod, layer):
    depth, d, n = w_mod.shape
    tn = 1024
    return pl.pallas_call(
        _mod_body,
        grid=(n // tn,),
        in_specs=[pl.BlockSpec((8, d), lambda j: (0, 0)),
                  pl.BlockSpec((1, d, tn), lambda j: (layer, 0, j)),
                  pl.BlockSpec((1, 1, tn), lambda j: (layer, 0, j))],
        out_specs=pl.BlockSpec((8, tn), lambda j: (0, j)),
        out_shape=jax.ShapeDtypeStruct((8, n), F32),
        compiler_params=_params(("parallel",)),
        name="modulation",
    )(cs, w_mod, b_mod.reshape(depth, 1, n))


def _prenorm_body(x_ref, g_ref, sc_ref, sh_ref, o_ref):
    x = x_ref[0]
    ms = jnp.mean(x * x, axis=-1, keepdims=True)
    y = x * lax.rsqrt(ms + EPS) * g_ref[...]
    o_ref[0] = (y * (1.0 + sc_ref[0]) + sh_ref[0]).astype(o_ref.dtype)


def _prenorm(x, gain, scale, shift, out_dtype=BF16):
    b, l, d = x.shape
    ts = min(l, 512)
    return pl.pallas_call(
        _prenorm_body,
        grid=(b, l // ts),
        in_specs=[pl.BlockSpec((1, ts, d), lambda bi, i: (bi, i, 0)),
                  pl.BlockSpec((1, d), lambda bi, i: (0, 0)),
                  pl.BlockSpec((1, 1, d), lambda bi, i: (bi, 0, 0)),
                  pl.BlockSpec((1, 1, d), lambda bi, i: (bi, 0, 0))],
        out_specs=pl.BlockSpec((1, ts, d), lambda bi, i: (bi, i, 0)),
        out_shape=jax.ShapeDtypeStruct((b, l, d), out_dtype),
        compiler_params=_params(("parallel", "parallel")),
        name="prenorm",
    )(x, gain.reshape(1, d), scale, shift)


def _mm_body(a_ref, w_ref, o_ref):
    o_ref[...] = _dot(a_ref[...], w_ref[...]).astype(o_ref.dtype)


def _cast_body(x_ref, o_ref):
    o_ref[...] = x_ref[...].astype(o_ref.dtype)


def _cast_layer_body(x_ref, o_ref):
    o_ref[...] = x_ref[0].astype(o_ref.dtype)


def _cast_weight(w, layer, tn):
    _, k, n = w.shape
    return pl.pallas_call(
        _cast_layer_body,
        grid=(n // tn,),
        in_specs=[pl.BlockSpec((1, k, tn), lambda j: (layer, 0, j))],
        out_specs=pl.BlockSpec((k, tn), lambda j: (0, j)),
        out_shape=jax.ShapeDtypeStruct((k, n), BF16),
        compiler_params=_params(("parallel",)),
        name="cast_weight",
    )(w)


def _matmul(a, w, n, tm, tn, out_dtype=F32):
    m, k = a.shape
    tm = min(tm, m)
    return pl.pallas_call(
        _mm_body,
        grid=(n // tn, m // tm),
        in_specs=[pl.BlockSpec((tm, k), lambda j, i: (i, 0)),
                  pl.BlockSpec((k, tn), lambda j, i: (0, j))],
        out_specs=pl.BlockSpec((tm, tn), lambda j, i: (i, j)),
        out_shape=jax.ShapeDtypeStruct((m, n), out_dtype),
        compiler_params=_params(("parallel", "parallel")),
        name="in_proj",
    )(a, w)


def _rope_tables(n_tok, rope):
    if not rope:
        return jnp.ones((n_tok, LANES), F32), jnp.zeros((n_tok, LANES), F32)
    rows = n_tok // GRID_W
    row = jnp.repeat(jnp.arange(rows), GRID_W).astype(F32)
    col = jnp.tile(jnp.arange(GRID_W), rows).astype(F32)
    quarter = HEAD_DIM // 4
    inv = ROPE_THETA ** (-jnp.arange(quarter, dtype=F32) / quarter)
    ar = row[:, None] * inv
    ac = col[:, None] * inv
    cos64 = jnp.concatenate([jnp.cos(ar), jnp.cos(ar), jnp.cos(ac), jnp.cos(ac)], axis=-1)
    sin64 = jnp.concatenate([-jnp.sin(ar), jnp.sin(ar), -jnp.sin(ac), jnp.sin(ac)], axis=-1)
    return jnp.tile(cos64, (1, 2)), jnp.tile(sin64, (1, 2))


def _head_ones():
    i = np.arange(LANES)
    return jnp.asarray((i[:, None] // HEAD_DIM) == (i[None, :] // HEAD_DIM), BF16)


def _qk_prep_body(x_ref, g_ref, cos_ref, sin_ref, ones_ref, o_ref, *, scale):
    x = x_ref[0]
    hi, lo = _split(x * x)
    ssum = _dot(hi, ones_ref[...]) + _dot(lo, ones_ref[...])
    y = x * lax.rsqrt(ssum * (1.0 / HEAD_DIM) + EPS) * g_ref[...]
    lane = lax.broadcasted_iota(jnp.int32, y.shape, 1)
    quarter = HEAD_DIM // 4
    partner = jnp.where((lane & quarter) == 0, pltpu.roll(y, LANES - quarter, 1), pltpu.roll(y, quarter, 1))
    o = (y * cos_ref[...] + partner * sin_ref[...]) * scale
    if o_ref.shape[2] == LANES:
        o_ref[0] = o.astype(o_ref.dtype)
    else:
        low = lane < HEAD_DIM
        even = jnp.where(low, o, 0.0)
        odd = jnp.where(low, 0.0, o)
        kv_head = pl.program_id(1) // (WIN_GROUP // 2)
        first = jnp.where(kv_head == 0, even, pltpu.roll(even, HEAD_DIM, 1))
        second = jnp.where(kv_head == 0, pltpu.roll(odd, HEAD_DIM, 1), odd)
        o_ref[0] = jnp.concatenate([first, second], axis=-1).astype(o_ref.dtype)


def _qk_prep(ux, col_off, width, gain, tables, scale, pad_heads=False):
    b, l, _ = ux.shape
    cos_t, sin_t = tables
    c0 = col_off // LANES
    ow = 2 * LANES if pad_heads else LANES
    g = jnp.tile(gain.reshape(1, HEAD_DIM), (1, LANES // HEAD_DIM))
    return pl.pallas_call(
        functools.partial(_qk_prep_body, scale=scale),
        grid=(b, width // LANES),
        in_specs=[pl.BlockSpec((1, l, LANES), lambda bi, j: (bi, 0, c0 + j)),
                  pl.BlockSpec((1, LANES), lambda bi, j: (0, 0)),
                  pl.BlockSpec((l, LANES), lambda bi, j: (0, 0)),
                  pl.BlockSpec((l, LANES), lambda bi, j: (0, 0)),
                  pl.BlockSpec((LANES, LANES), lambda bi, j: (0, 0))],
        out_specs=pl.BlockSpec((1, l, ow), lambda bi, j: (bi, 0, j)),
        out_shape=jax.ShapeDtypeStruct((b, l, width // LANES * ow), BF16),
        compiler_params=_params(("parallel", "parallel")),
        name="qk_prep",
    )(ux, g, cos_t, sin_t, _head_ones())


def _cast_swap_body(x_ref, o_ref, s_ref):
    x = x_ref[0]
    o_ref[0] = x.astype(o_ref.dtype)
    s_ref[0] = pltpu.roll(x, HEAD_DIM, 1).astype(s_ref.dtype)


def _cast_cols(ux, col_off, width, swapped=False):
    b, l, _ = ux.shape
    c0 = col_off // LANES
    blk = pl.BlockSpec((1, l, LANES), lambda bi, j: (bi, 0, j))
    shp = jax.ShapeDtypeStruct((b, l, width), BF16)
    return pl.pallas_call(
        _cast_swap_body if swapped else _cast_body,
        grid=(b, width // LANES),
        in_specs=[pl.BlockSpec((1, l, LANES), lambda bi, j: (bi, 0, c0 + j))],
        out_specs=[blk, blk] if swapped else blk,
        out_shape=[shp, shp] if swapped else shp,
        compiler_params=_params(("parallel", "parallel")),
        name="cast_cols",
    )(ux)


def _nt_dot(a, b):
    return lax.dot_general(a, b, (((1,), (1,)), ((), ())), preferred_element_type=F32)


_WIN_STACK = (0, 2, 1, 3)


def _win_heads(q, sink_ref, h, n):
    heads = [h * WIN_GROUP + g for g in _WIN_STACK]
    qh = jnp.concatenate([q[:, a * LANES:(a + 1) * LANES] for a in heads], axis=0)
    sk = jnp.concatenate([jnp.full((n, 1), sink_ref[a], F32) for a in heads], axis=0)
    return qh, sk


def _win_tiles(o_low, o_high, n):
    low = lax.broadcasted_iota(jnp.int32, (n, LANES), 1) < HEAD_DIM
    return [jnp.where(low, o_low[t * n:(t + 1) * n], o_high[t * n:(t + 1) * n]) for t in range(2)]


WIN_STEP_BLOCKS = 4


def _win_attn_body(sink_ref, q_ref, km_ref, k0_ref, kp_ref, vm_ref, v0_ref, vp_ref, wm_ref, w0_ref, wp_ref,
                   kc_ref, vc_ref, wc_ref, o_ref, *, seq, nsub):
    i = pl.program_id(1)
    kslab = jnp.concatenate([km_ref[0], k0_ref[0], kp_ref[0]], axis=0)
    vslab = jnp.concatenate([vm_ref[0], v0_ref[0], vp_ref[0]], axis=0)
    wslab = jnp.concatenate([wm_ref[0], w0_ref[0], wp_ref[0]], axis=0)
    kc = kc_ref[0]
    qi = lax.broadcasted_iota(jnp.int32, (BLOCK, 3 * BLOCK), 0)
    kj = lax.broadcasted_iota(jnp.int32, (BLOCK, 3 * BLOCK), 1)
    band = jnp.abs(kj - BLOCK - qi) <= WINDOW
    half = WIN_GROUP // 2 * BLOCK
    for j in range(nsub):
        q = q_ref[0, j * BLOCK:(j + 1) * BLOCK]
        ws = slice(j * BLOCK, (j + 3) * BLOCK)
        kwin = kslab[ws]
        v_nat = (vslab[ws], vc_ref[0])
        v_swp = (wslab[ws], wc_ref[0])
        kpos = (i * nsub + j - 1) * BLOCK + kj
        valid = band & (kpos >= 0) & (kpos < seq)
        valid = jnp.concatenate([valid] * WIN_GROUP, axis=0)
        tiles = []
        for h in range(WIN_KV_HEADS):
            qh, sk = _win_heads(q, sink_ref, h, BLOCK)
            s_loc = jnp.where(valid, _nt_dot(qh, kwin), NEG_INF)
            s_ctx = _nt_dot(qh, kc)
            m = jnp.maximum(jnp.maximum(jnp.max(s_loc, axis=-1, keepdims=True),
                                        jnp.max(s_ctx, axis=-1, keepdims=True)), sk)
            p_loc = jnp.exp(s_loc - m)
            p_ctx = jnp.exp(s_ctx - m)
            denom = (jnp.sum(p_loc, axis=-1, keepdims=True) + jnp.sum(p_ctx, axis=-1, keepdims=True)
                     + jnp.exp(sk - m))
            p_loc = p_loc.astype(BF16)
            p_ctx = p_ctx.astype(BF16)
            v_low, v_high = (v_nat, v_swp) if h == 0 else (v_swp, v_nat)
            o_low = (_dot(p_loc[:half], v_low[0]) + _dot(p_ctx[:half], v_low[1])) / denom[:half]
            o_high = (_dot(p_loc[half:], v_high[0]) + _dot(p_ctx[half:], v_high[1])) / denom[half:]
            tiles.extend(_win_tiles(o_low, o_high, BLOCK))
        o_ref[0, j * BLOCK:(j + 1) * BLOCK] = jnp.concatenate(tiles, axis=-1).astype(o_ref.dtype)


def _win_attn(q, k, v, vs, kc, vc, vcs, sink):
    b, s, _ = q.shape
    c = kc.shape[1]
    nb = s // BLOCK
    nsub = math.gcd(nb, WIN_STEP_BLOCKS)
    rows = nsub * BLOCK
    kvw = WIN_KV_HEADS * HEAD_DIM
    cur = lambda bi, i: (bi, i, 0)
    prev = pl.BlockSpec((1, BLOCK, kvw), lambda bi, i: (bi, jnp.maximum(i * nsub - 1, 0), 0))
    main = pl.BlockSpec((1, rows, kvw), cur)
    nxt = pl.BlockSpec((1, BLOCK, kvw), lambda bi, i: (bi, jnp.minimum((i + 1) * nsub, nb - 1), 0))
    cspec = pl.BlockSpec((1, c, kvw), lambda bi, i: (bi, 0, 0))
    return pl.pallas_call(
        functools.partial(_win_attn_body, seq=s, nsub=nsub),
        grid=(b, nb // nsub),
        in_specs=[pl.BlockSpec(memory_space=pltpu.SMEM),
                  pl.BlockSpec((1, rows, WIN_HEADS * LANES), cur),
                  prev, main, nxt, prev, main, nxt, prev, main, nxt,
                  cspec, cspec, cspec],
        out_specs=pl.BlockSpec((1, rows, WIN_HEADS * HEAD_DIM), cur),
        out_shape=jax.ShapeDtypeStruct((b, s, WIN_HEADS * HEAD_DIM), BF16),
        compiler_params=_params(("parallel", "parallel")),
        name="win_attn",
    )(sink.astype(F32), q, k, k, k, v, v, v, vs, vs, vs, kc, vc, vcs)


def _ctx_win_attn_body(sink_ref, q_ref, k_ref, v_ref, w_ref, o_ref):
    q = q_ref[0]
    k = k_ref[0]
    c = q.shape[0]
    half = WIN_GROUP // 2 * c
    tiles = []
    for h in range(WIN_KV_HEADS):
        qh, sk = _win_heads(q, sink_ref, h, c)
        s = _nt_dot(qh, k)
        m = jnp.maximum(jnp.max(s, axis=-1, keepdims=True), sk)
        p = jnp.exp(s - m)
        denom = jnp.sum(p, axis=-1, keepdims=True) + jnp.exp(sk - m)
        p = p.astype(BF16)
        v_low, v_high = (v_ref[0], w_ref[0]) if h == 0 else (w_ref[0], v_ref[0])
        tiles.extend(_win_tiles(_dot(p[:half], v_low) / denom[:half], _dot(p[half:], v_high) / denom[half:], c))
    o_ref[0] = jnp.concatenate(tiles, axis=-1).astype(o_ref.dtype)


def _ctx_win_attn(q, k, v, vs, sink):
    b, c, _ = q.shape
    kvw = WIN_KV_HEADS * HEAD_DIM
    cspec = pl.BlockSpec((1, c, kvw), lambda bi: (bi, 0, 0))
    return pl.pallas_call(
        _ctx_win_attn_body,
        grid=(b,),
        in_specs=[pl.BlockSpec(memory_space=pltpu.SMEM),
                  pl.BlockSpec((1, c, WIN_HEADS * LANES), lambda bi: (bi, 0, 0)),
                  cspec, cspec, cspec],
        out_specs=pl.BlockSpec((1, c, WIN_HEADS * HEAD_DIM), lambda bi: (bi, 0, 0)),
        out_shape=jax.ShapeDtypeStruct((b, c, WIN_HEADS * HEAD_DIM), BF16),
        compiler_params=_params(("parallel",)),
        name="ctx_win_attn",
    )(sink.astype(F32), q, k, v, vs)


DIFF_CHAINS = 2


def _diff_attn_body(lam_ref, q_ref, k_ref, v_ref, g_ref, o_ref, *, lam_init):
    q = q_ref[0]
    k = k_ref[0]
    v = v_ref[0]
    lp = lam_ref[...]
    lam = (jnp.exp(jnp.sum(lp[0:1] * lp[1:2], axis=-1, keepdims=True))
           - jnp.exp(jnp.sum(lp[2:3] * lp[3:4], axis=-1, keepdims=True)) + lam_init)
    lane = lax.broadcasted_iota(jnp.int32, q.shape, 1)
    zero = jnp.zeros_like(q)
    q1 = jnp.where(lane < HEAD_DIM, q, zero)
    q2 = jnp.where(lane < HEAD_DIM, zero, q)
    rows = q.shape[0] // DIFF_CHAINS
    for t in range(DIFF_CHAINS):
        rs = slice(t * rows, (t + 1) * rows)
        s1 = _nt_dot(q1[rs], k)
        s2 = _nt_dot(q2[rs], k)
        p1 = jnp.exp(s1 - jnp.max(s1, axis=-1, keepdims=True))
        p2 = jnp.exp(s2 - jnp.max(s2, axis=-1, keepdims=True))
        r1 = 1.0 / jnp.sum(p1, axis=-1, keepdims=True)
        r2 = lam / jnp.sum(p2, axis=-1, keepdims=True)
        a = (p1 * r1 - p2 * r2).astype(BF16)
        o = _dot(a, v)
        ms = jnp.mean(o * o, axis=-1, keepdims=True)
        o_ref[0, rs] = (o * lax.rsqrt(ms + EPS) * g_ref[...] * (1.0 - lam_init)).astype(o_ref.dtype)


def _diff_attn(q, k, v, lam_params, subln, lam_init, tq):
    b, nq, w = q.shape
    nk = k.shape[1]
    tq = min(tq, nq)
    hw = 2 * HEAD_DIM
    return pl.pallas_call(
        functools.partial(_diff_attn_body, lam_init=lam_init),
        grid=(b, DIFF_HEADS, nq // tq),
        in_specs=[pl.BlockSpec((4, HEAD_DIM), lambda bi, h, t: (0, 0)),
                  pl.BlockSpec((1, tq, hw), lambda bi, h, t: (bi, t, h)),
                  pl.BlockSpec((1, nk, hw), lambda bi, h, t: (bi, 0, h)),
                  pl.BlockSpec((1, nk, hw), lambda bi, h, t: (bi, 0, h)),
                  pl.BlockSpec((1, hw), lambda bi, h, t: (0, 0))],
        out_specs=pl.BlockSpec((1, tq, hw), lambda bi, h, t: (bi, t, h)),
        out_shape=jax.ShapeDtypeStruct((b, nq, w), BF16),
        compiler_params=_params(("parallel", "parallel", "parallel"), 56),
        name="diff_attn",
    )(lam_params, q, k, v, subln.reshape(1, hw))


def _hy_pre_body(x0_ref, x1_ref, v_ref, w0_ref, w1_ref, wv_ref, b0_ref, b1_ref, bv_ref, z_ref, x0c_ref):
    n = x0_ref.shape[1]
    row = lax.broadcasted_iota(jnp.int32, (n, 1), 0)

    def sconv(u_ref, w_ref, b_ref):
        u = u_ref[0]
        w = w_ref[...]
        up = jnp.where(row == 0, 0.0, pltpu.roll(u, 1, 0))
        un = jnp.where(row == n - 1, 0.0, pltpu.roll(u, n - 1, 0))
        return ((b_ref[...] + up * w[0:1]) + u * w[1:2]) + un * w[2:3]

    x0c_ref[0] = sconv(x0_ref, w0_ref, b0_ref)
    z_ref[0] = sconv(v_ref, wv_ref, bv_ref) * sconv(x1_ref, w1_ref, b1_ref)


def _hy_pre(ux, conv_w, conv_b):
    b, l, _ = ux.shape
    nc = HY_WIDTH // LANES
    c0 = OFF_HY // LANES
    ub = lambda seg: pl.BlockSpec((1, l, LANES), lambda bi, j: (bi, 0, c0 + seg * nc + j))
    wb = lambda seg: pl.BlockSpec((3, LANES), lambda bi, j: (0, seg * nc + j))
    bb = lambda seg: pl.BlockSpec((1, LANES), lambda bi, j: (0, seg * nc + j))
    out = pl.BlockSpec((1, l, LANES), lambda bi, j: (bi, 0, j))
    shp = jax.ShapeDtypeStruct((b, l, HY_WIDTH), F32)
    cb = conv_b.reshape(1, 3 * HY_WIDTH)
    return pl.pallas_call(
        _hy_pre_body,
        grid=(b, nc),
        in_specs=[ub(0), ub(1), ub(2), wb(0), wb(1), wb(2), bb(0), bb(1), bb(2)],
        out_specs=[out, out],
        out_shape=[shp, shp],
        compiler_params=_params(("parallel", "parallel")),
        name="hy_pre",
    )(ux, ux, ux, conv_w, conv_w, conv_w, cb, cb, cb)


def _hy_features(l):
    n = jnp.arange(l, dtype=F32)
    pos = jnp.concatenate([n, l - n])
    t = pos / max(l - 1, 1)
    w = 2.0 * math.pi * pos / l
    f = jnp.linspace(1e-4, HY_BANDS - 1, HY_BANDS, dtype=F32)
    flag = jnp.ones((2 * l, 1), F32).at[l].set(0.0)
    feat = jnp.concatenate([t[:, None], jnp.cos(w[:, None] * f), -jnp.sin(w[:, None] * f), flag], axis=-1)
    return jnp.pad(feat, ((0, 0), (0, LANES - HY_EMB - 1)))


def _hy_filter_body(f_ref, w1_ref, b1_ref, w2_ref, b2_ref, w3_ref, fr_ref, dec_ref, k_ref, sum_ref):
    first = (pl.program_id(0) == 0) & (pl.program_id(1) == 0)

    @pl.when(first)
    def _():
        sum_ref[...] = jnp.zeros_like(sum_ref)

    f = f_ref[...]
    fr = fr_ref[...]
    h = jnp.sin(fr[0:1] * (_dot3(f, w1_ref[...]) + b1_ref[...]))
    h = jnp.sin(fr[1:2] * (_dot3(h, w2_ref[...]) + b2_ref[...]))
    h = _dot3(h, w3_ref[...]) * jnp.exp(-f[:, 0:1] * jnp.abs(dec_ref[...]))
    h = h * f[:, HY_EMB:HY_EMB + 1]
    k_ref[...] = h
    sum_ref[...] += jnp.sum(jnp.abs(h), axis=0, keepdims=True)


def _hy_filter(l, p):
    tl = min(l, 1024)
    nt = l // tl
    order = p['hy_w1'].shape[1]
    w1 = jnp.zeros((LANES, order), F32).at[:HY_EMB].set(p['hy_w1'])
    const = lambda d, i: (0, 0)
    return pl.pallas_call(
        _hy_filter_body,
        grid=(2, nt),
        in_specs=[pl.BlockSpec((tl, LANES), lambda d, i: (d * nt + i, 0)),
                  pl.BlockSpec((LANES, order), const),
                  pl.BlockSpec((1, order), const),
                  pl.BlockSpec((order, order), const),
                  pl.BlockSpec((1, order), const),
                  pl.BlockSpec((order, HY_WIDTH), lambda d, i: (0, d)),
                  pl.BlockSpec((2, order), const),
                  pl.BlockSpec((1, HY_WIDTH), lambda d, i: (0, d))],
        out_specs=[pl.BlockSpec((tl, HY_WIDTH), lambda d, i: (d * nt + i, 0)),
                   pl.BlockSpec((1, HY_WIDTH), const)],
        out_shape=[jax.ShapeDtypeStruct((2 * l, HY_WIDTH), F32), jax.ShapeDtypeStruct((1, HY_WIDTH), F32)],
        compiler_params=_params(("arbitrary", "arbitrary")),
        name="hy_filter",
    )(_hy_features(l), w1, p['hy_b1'].reshape(1, order), p['hy_w2'], p['hy_b2'].reshape(1, order),
      p['hy_w3'], p['hy_freq'], p['hy_decay'].reshape(1, 2 * HY_WIDTH))


def _fft_consts(n_la):
    ka = np.arange(FFT_NA)[:, None]
    la = np.arange(n_la)[None, :]
    ang = 2.0 * math.pi * ka * la / FFT_NA
    f1 = np.concatenate([np.cos(ang), -np.sin(ang)], axis=0)
    lb = np.arange(FFT_NB)[None, :]
    th = 2.0 * math.pi * ka * lb / FFT_N
    tc, ts = np.cos(th), np.sin(th)
    a2 = 2.0 * math.pi * np.arange(FFT_NB)[:, None] * np.arange(FFT_NB)[None, :] / FFT_NB
    gre, gim = np.cos(a2), -np.sin(a2)
    g2 = np.block([[gre, gim], [-gim, gre]])
    g2i = np.block([[gre, -gim], [gim, gre]])
    lo = np.arange(FFT_NA // 2)[:, None]
    ph = 2.0 * math.pi * lo * np.arange(FFT_NA)[None, :] / FFT_NA
    f1i = np.concatenate([np.cos(ph), -np.sin(ph)], axis=1) / FFT_N
    return f1, tc, ts, g2, g2i, f1i


def _fft_fwd(x_ref, f1_ref, tc_ref, ts_ref, g2_ref, p2_ref, cc, dot):
    rhs = jnp.concatenate([x_ref[0, c] for c in range(cc)], axis=1)
    a = dot(f1_ref[...].astype(p2_ref.dtype), rhs.astype(p2_ref.dtype))
    tc = tc_ref[...]
    ts = ts_ref[...]
    for c in range(cc):
        are = a[0:FFT_NA, c * FFT_NB:(c + 1) * FFT_NB]
        aim = a[FFT_NA:2 * FFT_NA, c * FFT_NB:(c + 1) * FFT_NB]
        p2_ref[c * FFT_NA:(c + 1) * FFT_NA, 0:FFT_NB] = (are * tc + aim * ts).astype(p2_ref.dtype)
        p2_ref[c * FFT_NA:(c + 1) * FFT_NA, FFT_NB:2 * FFT_NB] = (aim * tc - are * ts).astype(p2_ref.dtype)
    return dot(p2_ref[...], g2_ref[...].astype(p2_ref.dtype))


def _fft_spec_body(x_ref, f1_ref, tc_ref, ts_ref, g2_ref, o_ref, p2_ref, *, cc):
    o_ref[...] = _fft_fwd(x_ref, f1_ref, tc_ref, ts_ref, g2_ref, p2_ref, cc, _dot3)


def _fft_conv_body(x_ref, kf_ref, f1_ref, tc_ref, ts_ref, g2_ref, g2i_ref, f1i_ref, o_ref, p2_ref, r2_ref, *, cc):
    z = _fft_fwd(x_ref, f1_ref, tc_ref, ts_ref, g2_ref, p2_ref, cc, _dot)
    kf = kf_ref[...]
    zre, zim = z[:, :FFT_NB], z[:, FFT_NB:]
    kre, kim = kf[:, :FFT_NB], kf[:, FFT_NB:]
    y = jnp.concatenate([zre * kre - zim * kim, zre * kim + zim * kre], axis=1).astype(BF16)
    u = _dot(y, g2i_ref[...].astype(BF16))
    tc = tc_ref[...]
    ts = ts_ref[...]
    for c in range(cc):
        ure = u[c * FFT_NA:(c + 1) * FFT_NA, 0:FFT_NB]
        uim = u[c * FFT_NA:(c + 1) * FFT_NA, FFT_NB:2 * FFT_NB]
        r2_ref[0:FFT_NA, c * FFT_NB:(c + 1) * FFT_NB] = (ure * tc - uim * ts).astype(BF16)
        r2_ref[FFT_NA:2 * FFT_NA, c * FFT_NB:(c + 1) * FFT_NB] = (ure * ts + uim * tc).astype(BF16)
    out = _dot(f1i_ref[...].astype(BF16), r2_ref[...])
    for c in range(cc):
        o_ref[0, c] = out[:, c * FFT_NB:(c + 1) * FFT_NB]


FFT_CC = 32


def _fft_spectrum(kt):
    nch, n_la, _ = kt.shape
    cc = FFT_CC
    f1, tc, ts, g2, _, _ = _fft_consts(n_la)
    const = lambda j: (0, 0)
    return pl.pallas_call(
        functools.partial(_fft_spec_body, cc=cc),
        grid=(nch // cc,),
        in_specs=[pl.BlockSpec((1, cc, n_la, FFT_NB), lambda j: (0, j, 0, 0)),
                  pl.BlockSpec((2 * FFT_NA, n_la), const),
                  pl.BlockSpec((FFT_NA, FFT_NB), const),
                  pl.BlockSpec((FFT_NA, FFT_NB), const),
                  pl.BlockSpec((2 * FFT_NB, 2 * FFT_NB), const)],
        out_specs=pl.BlockSpec((cc * FFT_NA, 2 * FFT_NB), lambda j: (j, 0)),
        out_shape=jax.ShapeDtypeStruct((nch * FFT_NA, 2 * FFT_NB), F32),
        scratch_shapes=[pltpu.VMEM((cc * FFT_NA, 2 * FFT_NB), F32)],
        compiler_params=_params(("parallel",)),
        name="fft_spectrum",
    )(kt[None], jnp.asarray(f1, F32), jnp.asarray(tc, F32), jnp.asarray(ts, F32), jnp.asarray(g2, F32))


def _fft_conv(zt, kf):
    b, nch, n_la, _ = zt.shape
    cc = FFT_CC
    f1, tc, ts, g2, g2i, f1i = _fft_consts(n_la)
    const = lambda j, bi: (0, 0)
    return pl.pallas_call(
        functools.partial(_fft_conv_body, cc=cc),
        grid=(nch // cc, b),
        in_specs=[pl.BlockSpec((1, cc, n_la, FFT_NB), lambda j, bi: (bi, j, 0, 0)),
                  pl.BlockSpec((cc * FFT_NA, 2 * FFT_NB), lambda j, bi: (j, 0)),
                  pl.BlockSpec((2 * FFT_NA, n_la), const),
                  pl.BlockSpec((FFT_NA, FFT_NB), const),
                  pl.BlockSpec((FFT_NA, FFT_NB), const),
                  pl.BlockSpec((2 * FFT_NB, 2 * FFT_NB), const),
                  pl.BlockSpec((2 * FFT_NB, 2 * FFT_NB), const),
                  pl.BlockSpec((FFT_NA // 2, 2 * FFT_NA), const)],
        out_specs=pl.BlockSpec((1, cc, n_la, FFT_NB), lambda j, bi: (bi, j, 0, 0)),
        out_shape=jax.ShapeDtypeStruct((b, nch, n_la, FFT_NB), F32),
        scratch_shapes=[pltpu.VMEM((cc * FFT_NA, 2 * FFT_NB), BF16),
                        pltpu.VMEM((2 * FFT_NA, cc * FFT_NB), BF16)],
        compiler_params=_params(("parallel", "parallel")),
        name="fft_conv",
    )(zt, kf, jnp.asarray(f1, F32), jnp.asarray(tc, F32), jnp.asarray(ts, F32),
      jnp.asarray(g2, F32), jnp.asarray(g2i, F32), jnp.asarray(f1i, F32))


def _dft_conv_body(z_ref, k_ref, fw_ref, fi_ref, o_ref):
    l = z_ref.shape[1]
    fw = fw_ref[...]
    n = fw.shape[1]
    kf = _dot3(fw, k_ref[...])
    zf = _dot3(fw[:, :l], z_ref[0])
    kre, kim = kf[:n], kf[n:]
    zre, zim = zf[:n], zf[n:]
    y = jnp.concatenate([zre * kre - zim * kim, zre * kim + zim * kre], axis=0)
    o_ref[0] = _dot3(fi_ref[...], y)


def _dft_conv(z, k):
    b, l, w = z.shape
    n = 2 * l
    ang = 2.0 * math.pi * np.arange(n)[:, None] * np.arange(n)[None, :] / n
    fw = np.concatenate([np.cos(ang), -np.sin(ang)], axis=0)
    ai = ang[:l]
    fi = np.concatenate([np.cos(ai), -np.sin(ai)], axis=1) / n
    return pl.pallas_call(
        _dft_conv_body,
        grid=(b,),
        in_specs=[pl.BlockSpec((1, l, w), lambda bi: (bi, 0, 0)),
                  pl.BlockSpec((n, w), lambda bi: (0, 0)),
                  pl.BlockSpec((2 * n, n), lambda bi: (0, 0)),
                  pl.BlockSpec((l, 2 * n), lambda bi: (0, 0))],
        out_specs=pl.BlockSpec((1, l, w), lambda bi: (bi, 0, 0)),
        out_shape=jax.ShapeDtypeStruct((b, l, w), F32),
        compiler_params=_params(("parallel",)),
        name="dft_conv",
    )(z, k, jnp.asarray(fw, F32), jnp.asarray(fi, F32))


def _hy_post_body(y_ref, z_ref, x0_ref, ks_ref, sk_ref, o_ref):
    y = y_ref[0] / (ks_ref[...] + EPS) + sk_ref[...] * z_ref[0]
    o_ref[0] = (y * x0_ref[0]).astype(o_ref.dtype)


def _hy_post(yraw, z, x0c, ksum, skip):
    b, l, w = z.shape
    ts = min(l, 1024)
    blk = pl.BlockSpec((1, ts, w), lambda bi, i: (bi, i, 0))
    vec = pl.BlockSpec((1, w), lambda bi, i: (0, 0))
    return pl.pallas_call(
        _hy_post_body,
        grid=(b, l // ts),
        in_specs=[blk, blk, blk, vec, vec],
        out_specs=blk,
        out_shape=jax.ShapeDtypeStruct((b, l, w), BF16),
        compiler_params=_params(("parallel", "parallel")),
        name="hy_post",
    )(yraw, z, x0c, ksum, skip.reshape(1, w))


def _hyena(ux, p):
    b, l, _ = ux.shape
    z, x0c = _hy_pre(ux, p['hy_conv_w'], p['hy_conv_b'])
    k_raw, ksum = _hy_filter(l, p)
    if 2 * l == FFT_N:
        kt = k_raw.T.reshape(HY_WIDTH, FFT_NA, FFT_NB)
        kf = _fft_spectrum(kt)
        zt = jnp.swapaxes(z, 1, 2).reshape(b, HY_WIDTH, l // FFT_NB, FFT_NB)
        yt = _fft_conv(zt, kf)
        yraw = jnp.swapaxes(yt.reshape(b, HY_WIDTH, l), 1, 2)
    else:
        yraw = _dft_conv(z, k_raw)
    return _hy_post(yraw, z, x0c, ksum, p['hy_skip'])


def _pool_body(u_ref, w_ref, sc_ref, o_ref, *, win):
    u = u_ref[0]
    n = u.shape[0]
    lo = win // 2
    hi = win - 1 - lo
    row = lax.broadcasted_iota(jnp.int32, (n, 1), 0)
    acc = u
    for j in range(1, lo + 1):
        acc = acc + jnp.where(row >= j, pltpu.roll(u, j, 0), 0.0)
    for j in range(1, hi + 1):
        acc = acc + jnp.where(row < n - j, pltpu.roll(u, n - j, 0), 0.0)
    cnt = (jnp.minimum(row + hi + 1, n) - jnp.maximum(row - lo, 0)).astype(F32)
    d = acc / cnt - u
    o_ref[0] = (_dot(d.astype(BF16), w_ref[0].astype(BF16)) * sc_ref[...]).astype(o_ref.dtype)


def _pool_mix(ux, w_grp, scale):
    b, l, _ = ux.shape
    c0 = OFF_PL // LANES
    outs = []
    for gi, win in enumerate(POOL_WINDOWS):
        outs.append(pl.pallas_call(
            functools.partial(_pool_body, win=win),
            grid=(b,),
            in_specs=[pl.BlockSpec((1, l, LANES), lambda bi, gi=gi: (bi, 0, c0 + gi)),
                      pl.BlockSpec((1, LANES, LANES), lambda bi, gi=gi: (gi, 0, 0)),
                      pl.BlockSpec((1, LANES), lambda bi, gi=gi: (0, gi))],
            out_specs=pl.BlockSpec((1, l, LANES), lambda bi: (bi, 0, 0)),
            out_shape=jax.ShapeDtypeStruct((b, l, LANES), BF16),
            compiler_params=_params(("parallel",)),
            name=f"pool_mix{win}",
        )(ux, w_grp, scale.reshape(1, -1)))
    return jnp.concatenate(outs, axis=-1)


def _merge1_body(hx_ref, y0, y1, y2, y3, g0, g1, g2, g3, b0, b1, b2, b3, o_ref):
    hx = hx_ref[...]
    acc = None
    for y_ref, g_ref, b_ref in ((y0, g0, b0), (y1, g1, b1), (y2, g2, b2), (y3, g3, b3)):
        t = jax.nn.sigmoid(_dot(hx, g_ref[...])) * _dot(y_ref[...], b_ref[0, 0])
        acc = t if acc is None else acc + t
    o_ref[...] = acc.astype(o_ref.dtype)


def _merge1(hx, ys, w_in, gate_off, w_branch, layer, tm=1024, tn=256):
    m, d = hx.shape
    tm = min(tm, m)
    nj = d // tn
    g0 = gate_off // tn
    row = lambda i, j: (i, 0)
    gspec = lambda br: pl.BlockSpec((d, tn), lambda i, j, br=br: (0, g0 + br * nj + j))
    bspec = lambda br: pl.BlockSpec((1, 1, BRANCH_W, tn), lambda i, j, br=br: (layer, br, 0, j))
    return pl.pallas_call(
        _merge1_body,
        grid=(m // tm, nj),
        in_specs=([pl.BlockSpec((tm, d), row)] + [pl.BlockSpec((tm, BRANCH_W), row)] * N_BRANCH
                  + [gspec(br) for br in range(N_BRANCH)] + [bspec(br) for br in range(N_BRANCH)]),
        out_specs=pl.BlockSpec((tm, tn), lambda i, j: (i, j)),
        out_shape=jax.ShapeDtypeStruct((m, d), BF16),
        compiler_params=_params(("parallel", "parallel")),
        name="merge_gate",
    )(hx, *ys, w_in, w_in, w_in, w_in, w_branch, w_branch, w_branch, w_branch)


def _merge2_body(a_ref, w_ref, x_ref, g_ref, o_ref):
    o_ref[0] = x_ref[0] + g_ref[0] * _dot(a_ref[0], w_ref[0])


def _merge2(acc, w_out, layer, x, gate, tm=512):
    b, l, d = x.shape
    tm = min(tm, l)
    blk = pl.BlockSpec((1, tm, d), lambda bi, i: (bi, i, 0))
    return pl.pallas_call(
        _merge2_body,
        grid=(b, l // tm),
        in_specs=[blk, pl.BlockSpec((1, d, d), lambda bi, i: (layer, 0, 0)), blk,
                  pl.BlockSpec((1, 1, d), lambda bi, i: (bi, 0, 0))],
        out_specs=blk,
        out_shape=jax.ShapeDtypeStruct((b, l, d), F32),
        compiler_params=_params(("parallel", "parallel")),
        name="merge_out",
    )(acc, w_out, x, gate)


def _route_body(x_ref, g_ref, sc_ref, sh_ref, wr_ref, br_ref, h_ref, e_ref, w_ref):
    x = x_ref[0]
    ms = jnp.mean(x * x, axis=-1, keepdims=True)
    h = (x * lax.rsqrt(ms + EPS) * g_ref[...]) * (1.0 + sc_ref[0]) + sh_ref[0]
    h_ref[0] = h
    logits = _dot3(h, wr_ref[...]) + br_ref[...]
    lane = lax.broadcasted_iota(jnp.int32, logits.shape, 1)
    big = jnp.int32(LANES)
    gl = jnp.where(lane < N_GROUPS, logits, -jnp.inf)
    ge = jnp.exp(gl - jnp.max(gl, axis=-1, keepdims=True))
    pg = ge / jnp.sum(ge, axis=-1, keepdims=True)
    pg_top = jnp.max(pg, axis=-1, keepdims=True)
    g_idx = jnp.min(jnp.where((pg == pg_top) & (lane < N_GROUPS), lane, big), axis=-1, keepdims=True)
    first = N_GROUPS + EXPERTS_PER_GROUP * g_idx
    le = jnp.where((lane >= first) & (lane < first + EXPERTS_PER_GROUP), logits, -jnp.inf)
    v1 = jnp.max(le, axis=-1, keepdims=True)
    i1 = jnp.min(jnp.where(le == v1, lane, big), axis=-1, keepdims=True)
    le2 = jnp.where(lane == i1, -jnp.inf, le)
    v2 = jnp.max(le2, axis=-1, keepdims=True)
    i2 = jnp.min(jnp.where(le2 == v2, lane, big), axis=-1, keepdims=True)
    e2 = jnp.exp(v2 - v1)
    den = 1.0 + e2
    e_ref[0] = jnp.where(lane == 0, i1 - N_GROUPS, jnp.where(lane == 1, i2 - N_GROUPS, 0))
    w_ref[0] = jnp.where(lane == 0, (1.0 / den) * pg_top, jnp.where(lane == 1, (e2 / den) * pg_top, 0.0))


def _route(x, gain, scale, shift, w_router, b_router, tm=512):
    b, l, d = x.shape
    tm = min(tm, l)
    blk = pl.BlockSpec((1, tm, d), lambda bi, i: (bi, i, 0))
    mod = pl.BlockSpec((1, 1, d), lambda bi, i: (bi, 0, 0))
    sm = pl.BlockSpec((1, tm, LANES), lambda bi, i: (bi, i, 0))
    return pl.pallas_call(
        _route_body,
        grid=(b, l // tm),
        in_specs=[blk, pl.BlockSpec((1, d), lambda bi, i: (0, 0)), mod, mod,
                  pl.BlockSpec((d, LANES), lambda bi, i: (0, 0)),
                  pl.BlockSpec((1, LANES), lambda bi, i: (0, 0))],
        out_specs=[blk, sm, sm],
        out_shape=[jax.ShapeDtypeStruct((b, l, d), F32),
                   jax.ShapeDtypeStruct((b, l, LANES), jnp.int32),
                   jax.ShapeDtypeStruct((b, l, LANES), F32)],
        compiler_params=_params(("parallel", "parallel")),
        name="route",
    )(x, gain.reshape(1, d), scale, shift, w_router, b_router)


def _row_gather_body(idx_ref, src, o_ref, buf, sem):
    i = pl.program_id(0)
    n = pl.num_programs(0)
    tm = o_ref.shape[0]
    slot = i % 2

    @pl.when(i < n - 1)
    def _():
        for r in range(tm):
            pltpu.make_async_copy(src.at[pl.ds(idx_ref[0, 0, r], 1)], buf.at[slot, pl.ds(r, 1)],
                                  sem.at[slot]).start(priority=r % 2)

    @pl.when(i >= 1)
    def _():
        pltpu.make_async_copy(src.at[pl.ds(0, tm)], buf.at[1 - slot], sem.at[1 - slot]).wait()
        o_ref[...] = buf[1 - slot]


def _row_gather(src, idx):
    nblk, tm = idx.shape
    d = src.shape[1]
    return pl.pallas_call(
        _row_gather_body,
        grid=(nblk + 1,),
        in_specs=[pl.BlockSpec((1, 1, tm), lambda i: (jnp.minimum(i, nblk - 1), 0, 0), memory_space=pltpu.SMEM),
                  pl.BlockSpec(memory_space=pl.ANY)],
        out_specs=pl.BlockSpec((tm, d), lambda i: (jnp.maximum(i - 1, 0), 0)),
        out_shape=jax.ShapeDtypeStruct((nblk * tm, d), src.dtype),
        scratch_shapes=[pltpu.VMEM((2, tm, d), src.dtype), pltpu.SemaphoreType.DMA((2,))],
        compiler_params=_params(("arbitrary",)),
        name="row_gather",
    )(idx.reshape(nblk, 1, tm), src)


def _row_scatter_body(idx_ref, x_ref, dst, buf, sem):
    i = pl.program_id(0)
    n = pl.num_programs(0)
    tm = x_ref.shape[0]
    slot = i % 2

    def wait(s):
        pltpu.make_async_copy(buf.at[s], dst.at[pl.ds(0, tm)], sem.at[s]).wait()

    @pl.when(i >= 2)
    def _():
        wait(slot)

    buf[slot] = x_ref[...]
    for r in range(tm):
        pltpu.make_async_copy(buf.at[slot, pl.ds(r, 1)], dst.at[pl.ds(idx_ref[0, 0, r], 1)],
                              sem.at[slot]).start(priority=r % 2)

    @pl.when(i == n - 1)
    def _():
        wait(slot)

        @pl.when(n >= 2)
        def _():
            wait(1 - slot)


def _row_scatter(src, idx):
    nblk, tm = idx.shape
    d = src.shape[1]
    return pl.pallas_call(
        _row_scatter_body,
        grid=(nblk,),
        in_specs=[pl.BlockSpec((1, 1, tm), lambda i: (i, 0, 0), memory_space=pltpu.SMEM),
                  pl.BlockSpec((tm, d), lambda i: (i, 0))],
        out_specs=pl.BlockSpec(memory_space=pl.ANY),
        out_shape=jax.ShapeDtypeStruct((nblk * tm, d), src.dtype),
        scratch_shapes=[pltpu.VMEM((2, tm, d), src.dtype), pltpu.SemaphoreType.DMA((2,))],
        compiler_params=_params(("arbitrary",)),
        name="row_scatter",
    )(idx.reshape(nblk, 1, tm), src)


def _ffn_body(blk_e_ref, nused_ref, x_ref, sw_ref, w1_ref, w3_ref, w2_ref, o_ref):
    i = pl.program_id(0)

    @pl.when(i < nused_ref[0])
    def _():
        x = x_ref[...].astype(BF16)
        a = _dot(x, w1_ref[0, 0])
        h = ((a * jax.nn.sigmoid(a)) * _dot(x, w3_ref[0, 0])).astype(BF16)
        o_ref[...] = _dot(h, w2_ref[0, 0]) * sw_ref[...]

    @pl.when(i >= nused_ref[0])
    def _():
        o_ref[...] = jnp.zeros_like(o_ref)


def _moe_experts(h2, blk_e, nused, gidx, sidx, slot_w, w1, w3, w2, layer):
    nblk, tm = gidx.shape
    d = h2.shape[1]
    ff = w1.shape[3]
    p = nblk * tm
    xb = _row_gather(h2, gidx)
    grid_spec = pltpu.PrefetchScalarGridSpec(
        num_scalar_prefetch=2,
        grid=(nblk,),
        in_specs=[pl.BlockSpec((tm, d), lambda i, be, nu: (i, 0)),
                  pl.BlockSpec((tm, 1), lambda i, be, nu: (i, 0)),
                  pl.BlockSpec((1, 1, d, ff), lambda i, be, nu: (layer, be[i], 0, 0)),
                  pl.BlockSpec((1, 1, d, ff), lambda i, be, nu: (layer, be[i], 0, 0)),
                  pl.BlockSpec((1, 1, ff, d), lambda i, be, nu: (layer, be[i], 0, 0))],
        out_specs=pl.BlockSpec((tm, d), lambda i, be, nu: (i, 0)),
    )
    yb = pl.pallas_call(
        _ffn_body,
        grid_spec=grid_spec,
        out_shape=jax.ShapeDtypeStruct((p, d), F32),
        compiler_params=_params(("arbitrary",), 56),
        name="moe_ffn",
    )(blk_e, nused, xb, slot_w, w1, w3, w2)
    return _row_scatter(yb, sidx)


def _moe_plan(e_idx, e_w, tm):
    n = e_idx.shape[0]
    m = n * TOP_K
    e_flat = e_idx.reshape(m)
    w_flat = e_w.reshape(m)
    order = jnp.argsort(e_flat).astype(jnp.int32)
    experts = jnp.arange(N_EXPERTS, dtype=jnp.int32)
    counts = jnp.sum((e_flat[:, None] == experts[None, :]).astype(jnp.int32), axis=0)
    padded = (counts + tm - 1) // tm * tm
    pad_end = jnp.cumsum(padded)
    pad_start = pad_end - padded
    cnt_start = jnp.cumsum(counts) - counts
    nblk = -(-(m + N_EXPERTS * (tm - 1)) // tm)
    blk_first = jnp.arange(nblk, dtype=jnp.int32) * tm
    blk_e = jnp.minimum(jnp.sum((pad_end[None, :] <= blk_first[:, None]).astype(jnp.int32), axis=1), N_EXPERTS - 1)
    rank = blk_first[:, None] + jnp.arange(tm, dtype=jnp.int32)[None, :] - pad_start[blk_e][:, None]
    valid = rank < counts[blk_e][:, None]
    src = order[jnp.clip(cnt_start[blk_e][:, None] + rank, 0, m - 1)]
    tok = jnp.where(valid, src // TOP_K, 0)
    slot_w = jnp.where(valid, w_flat[src], 0.0)
    pad_rank = (jnp.cumsum(jnp.logical_not(valid).reshape(-1).astype(jnp.int32)) - 1).reshape(nblk, tm)
    dst = jnp.where(valid, (src % TOP_K) * n + src // TOP_K, m + pad_rank)
    nused = (pad_end[-1] // tm).astype(jnp.int32).reshape(1)
    return blk_e, nused, tok.astype(jnp.int32), dst.astype(jnp.int32), slot_w.reshape(nblk * tm, 1).astype(F32)


def _combine_body(x_ref, y0_ref, y1_ref, g_ref, o_ref):
    o_ref[0] = x_ref[0] + g_ref[0] * (y0_ref[...] + y1_ref[...])


def _moe_combine(x, yk, gate, row_off, n_tok, tm=256):
    b, l, d = x.shape
    tm = min(tm, l)
    assert l % tm == 0 and row_off % tm == 0 and n_tok % tm == 0
    nl = l // tm
    blk = pl.BlockSpec((1, tm, d), lambda bi, i: (bi, i, 0))
    o0 = row_off // tm
    o1 = (n_tok + row_off) // tm
    return pl.pallas_call(
        _combine_body,
        grid=(b, nl),
        in_specs=[blk,
                  pl.BlockSpec((tm, d), lambda bi, i: (o0 + bi * nl + i, 0)),
                  pl.BlockSpec((tm, d), lambda bi, i: (o1 + bi * nl + i, 0)),
                  pl.BlockSpec((1, 1, d), lambda bi, i: (bi, 0, 0))],
        out_specs=blk,
        out_shape=jax.ShapeDtypeStruct((b, l, d), F32),
        compiler_params=_params(("parallel", "parallel")),
        name="moe_combine",
    )(x, yk, yk, gate)


def _mixers(ux, ucx, p, lam_init, latent):
    b, l, _ = ux.shape
    rope_q = _rope_tables(l, latent)
    no_rope = _rope_tables(ucx.shape[1], False)
    scale = HEAD_DIM ** -0.5
    qw = _qk_prep(ux, OFF_WQ, 512, p['win_qnorm'], rope_q, scale, pad_heads=True)
    qd = _qk_prep(ux, OFF_DQ, 512, p['diff_qnorm'], rope_q, scale)
    kcw = _qk_prep(ucx, OFF_WK, 128, p['win_knorm'], no_rope, 1.0)
    vcw, vcws = _cast_cols(ucx, OFF_WV, 128, swapped=True)
    kcd = _qk_prep(ucx, OFF_DK, 512, p['diff_knorm'], no_rope, 1.0)
    vcd = _cast_cols(ucx, OFF_DV, 512)
    if latent:
        kw = _qk_prep(ux, OFF_WK, 128, p['win_knorm'], rope_q, 1.0)
        vw, vws = _cast_cols(ux, OFF_WV, 128, swapped=True)
        kd = _qk_prep(ux, OFF_DK, 512, p['diff_knorm'], rope_q, 1.0)
        vd = _cast_cols(ux, OFF_DV, 512)
        y_win = _win_attn(qw, kw, vw, vws, kcw, vcw, vcws, p['win_sink'])
        k_all = jnp.concatenate([kd, kcd], axis=1)
        v_all = jnp.concatenate([vd, vcd], axis=1)
    else:
        y_win = _ctx_win_attn(qw, kcw, vcw, vcws, p['win_sink'])
        k_all, v_all = kcd, vcd
    y_diff = _diff_attn(qd, k_all, v_all, p['diff_lambda'], p['diff_subln'], lam_init, 512)
    y_hy = _hyena(ux, p)
    y_pool = _pool_mix(ux, p['pool_proj'], p['pool_scale'])
    return y_win, y_diff, y_hy, y_pool


def _layer(x, ctx, mod, p, layer_idx, update_ctx):
    b, s, d = x.shape
    c = ctx.shape[1]
    lam_init = 0.8 - 0.6 * math.exp(-0.3 * layer_idx)
    chunks = [mod[:, i * d:(i + 1) * d] for i in range(6)]
    sh1, sc1, g1, sh2, sc2, g2 = [t[:b].reshape(b, 1, d) for t in chunks]
    csh1, csc1, cg1, csh2, csc2, cg2 = [jnp.broadcast_to(t[b:b + 1].reshape(1, 1, d), (b, 1, d)) for t in chunks]

    w_in = _cast_weight(p['w_in'], layer_idx, 7 * LANES)
    w_branch = p['w_branch']
    w_out = p['w_out']
    tn_in = OFF_GT // 2

    hx = _prenorm(x, p['norm1'], sc1, sh1)
    hc = _prenorm(ctx, p['norm1'], csc1, csh1)
    ux = _matmul(hx.reshape(b * s, d), w_in, OFF_GT, 512, tn_in).reshape(b, s, OFF_GT)
    ucx = _matmul(hc.reshape(b * c, d), w_in, OFF_GT, 512, tn_in).reshape(b, c, OFF_GT)

    ys = _mixers(ux, ucx, p, lam_init, True)
    acc = _merge1(hx.reshape(b * s, d), [y.reshape(b * s, BRANCH_W) for y in ys], w_in, OFF_GT, w_branch, layer_idx)
    x = _merge2(acc.reshape(b, s, d), w_out, layer_idx, x, g1)
    if update_ctx:
        ycs = _mixers(ucx, ucx, p, lam_init, False)
        acc_c = _merge1(hc.reshape(b * c, d), [y.reshape(b * c, BRANCH_W) for y in ycs], w_in, OFF_GT, w_branch, layer_idx)
        ctx = _merge2(acc_c.reshape(b, c, d), w_out, layer_idx, ctx, cg1)

    w_router = jnp.zeros((d, LANES), F32).at[:, :N_GROUPS].set(p['r_group_w'])
    w_router = w_router.at[:, N_GROUPS:N_GROUPS + N_EXPERTS].set(p['r_expert_w'])
    b_router = jnp.zeros((1, LANES), F32).at[0, :N_GROUPS].set(p['r_group_b'])
    b_router = b_router.at[0, N_GROUPS:N_GROUPS + N_EXPERTS].set(p['r_expert_b'])
    h2, e_idx, e_w = _route(x, p['norm2'], sc2, sh2, w_router, b_router)
    h2 = h2.reshape(b * s, d)
    e_idx = e_idx.reshape(b * s, LANES)[:, :TOP_K]
    e_w = e_w.reshape(b * s, LANES)[:, :TOP_K]
    if update_ctx:
        h2c, ec_idx, ec_w = _route(ctx, p['norm2'], csc2, csh2, w_router, b_router)
        h2 = jnp.concatenate([h2, h2c.reshape(b * c, d)], axis=0)
        e_idx = jnp.concatenate([e_idx, ec_idx.reshape(b * c, LANES)[:, :TOP_K]], axis=0)
        e_w = jnp.concatenate([e_w, ec_w.reshape(b * c, LANES)[:, :TOP_K]], axis=0)
    n_tok = h2.shape[0]
    blk_e, nused, gidx, sidx, slot_w = _moe_plan(e_idx, e_w, MOE_TM)
    yk = _moe_experts(h2, blk_e, nused, gidx, sidx, slot_w, p['e_w1'], p['e_w3'], p['e_w2'], layer_idx)
    x = _moe_combine(x, yk, g2, 0, n_tok)
    if update_ctx:
        ctx = _moe_combine(ctx, yk, cg2, b * s, n_tok)
    return x, ctx


def kernel(x, c, ctx, c_ctx, w_mod, b_mod, norm1, norm2, w_in, win_sink, win_qnorm, win_knorm, diff_qnorm, diff_knorm, diff_lambda, diff_subln, hy_conv_w, hy_conv_b, hy_w1, hy_b1, hy_w2, hy_b2, hy_w3, hy_freq, hy_decay, hy_skip, pool_proj, pool_scale, w_branch, w_out, r_group_w, r_group_b, r_expert_w, r_expert_b, e_w1, e_w3, e_w2):
    depth = w_mod.shape[0]
    b, _, d = x.shape
    cs = jnp.zeros((8, d), F32).at[:b].set(c).at[b].set(c_ctx)
    stacked = dict(w_in=w_in, w_branch=w_branch.astype(BF16), w_out=w_out.astype(BF16),
                   e_w1=e_w1.astype(BF16), e_w3=e_w3.astype(BF16), e_w2=e_w2.astype(BF16))
    for l in range(depth):
        p = dict(norm1=norm1[l], norm2=norm2[l],
                 win_sink=win_sink[l], win_qnorm=win_qnorm[l], win_knorm=win_knorm[l],
                 diff_qnorm=diff_qnorm[l], diff_knorm=diff_knorm[l], diff_lambda=diff_lambda[l],
                 diff_subln=diff_subln[l], hy_conv_w=hy_conv_w[l], hy_conv_b=hy_conv_b[l],
                 hy_w1=hy_w1[l], hy_b1=hy_b1[l], hy_w2=hy_w2[l], hy_b2=hy_b2[l], hy_w3=hy_w3[l],
                 hy_freq=hy_freq[l], hy_decay=hy_decay[l], hy_skip=hy_skip[l],
                 pool_proj=pool_proj[l], pool_scale=pool_scale[l],
                 r_group_w=r_group_w[l], r_group_b=r_group_b[l], r_expert_w=r_expert_w[l],
                 r_expert_b=r_expert_b[l], **stacked)
        mod = _modulation(cs, w_mod, b_mod, l)
        x, ctx = _layer(x, ctx, mod, p, l, l < depth - 1)
    return x
```

```python
import functools
import math

import jax
import jax.numpy as jnp
import numpy as np
from jax import lax
from jax.experimental import pallas as pl
from jax.experimental.pallas import tpu as pltpu

F32 = jnp.float32
BF16 = jnp.bfloat16

GRID_W = 64
HEAD_DIM = 64
ROPE_THETA = 10000.0
EPS = 1e-6
NEG_INF = -1e30
BLOCK = 128
WINDOW = 128
LANES = 128

WIN_HEADS = 8
WIN_KV_HEADS = 2
WIN_GROUP = WIN_HEADS // WIN_KV_HEADS
DIFF_HEADS = 4
HY_WIDTH = 512
HY_EMB = 33
HY_BANDS = 16
POOL_WINDOWS = (2, 4, 8, 16)
N_BRANCH = 4
BRANCH_W = 512

OFF_WQ = 0
OFF_WK = 512
OFF_WV = 640
OFF_DQ = 768
OFF_DK = 1280
OFF_DV = 1792
OFF_HY = 2304
OFF_PL = 3840
OFF_GT = 4352

N_GROUPS = 4
EXPERTS_PER_GROUP = 8
N_EXPERTS = 32
TOP_K = 2
MOE_TM = 256

FFT_NA = 64
FFT_NB = 128
FFT_N = FFT_NA * FFT_NB


def _params(sem, mib=48):
    return pltpu.CompilerParams(dimension_semantics=sem, vmem_limit_bytes=mib * 2**20)


def _dot(a, b):
    return jnp.dot(a, b, preferred_element_type=F32)


def _split(a):
    hi = a.astype(BF16)
    lo = (a - hi.astype(F32)).astype(BF16)
    return hi, lo


def _dot3(a, b):
    ah, al = _split(a)
    bh, bl = _split(b)
    return _dot(ah, bh) + (_dot(ah, bl) + _dot(al, bh))


def _mod_body(c_ref, w_ref, b_ref, o_ref):
    c = c_ref[...]
    a = (c * jax.nn.sigmoid(c)).astype(BF16)
    o_ref[...] = _dot(a, w_ref[0].astype(BF16)) + b_ref[0]


def _modulation(cs, w_mod, b_mod, layer):
    depth, d, n = w_mod.shape
    tn = 1024
    return pl.pallas_call(
        _mod_body,
        grid=(n // tn,),
        in_specs=[pl.BlockSpec((8, d), lambda j: (0, 0)),
                  pl.BlockSpec((1, d, tn), lambda j: (layer, 0, j)),
                  pl.BlockSpec((1, 1, tn), lambda j: (layer, 0, j))],
        out_specs=pl.BlockSpec((8, tn), lambda j: (0, j)),
        out_shape=jax.ShapeDtypeStruct((8, n), F32),
        compiler_params=_params(("parallel",)),
        name="modulation",
    )(cs, w_mod, b_mod.reshape(depth, 1, n))


def _prenorm_body(x_ref, g_ref, sc_ref, sh_ref, o_ref):
    x = x_ref[0]
    ms = jnp.mean(x * x, axis=-1, keepdims=True)
    y = x * lax.rsqrt(ms + EPS) * g_ref[...]
    o_ref[0] = (y * (1.0 + sc_ref[0]) + sh_ref[0]).astype(o_ref.dtype)


def _prenorm(x, gain, scale, shift, out_dtype=BF16):
    b, l, d = x.shape
    ts = min(l, 512)
    return pl.pallas_call(
        _prenorm_body,
        grid=(b, l // ts),
        in_specs=[pl.BlockSpec((1, ts, d), lambda bi, i: (bi, i, 0)),
                  pl.BlockSpec((1, d), lambda bi, i: (0, 0)),
                  pl.BlockSpec((1, 1, d), lambda bi, i: (bi, 0, 0)),
                  pl.BlockSpec((1, 1, d), lambda bi, i: (bi, 0, 0))],
        out_specs=pl.BlockSpec((1, ts, d), lambda bi, i: (bi, i, 0)),
        out_shape=jax.ShapeDtypeStruct((b, l, d), out_dtype),
        compiler_params=_params(("parallel", "parallel")),
        name="prenorm",
    )(x, gain.reshape(1, d), scale, shift)


def _mm_body(a_ref, w_ref, o_ref):
    o_ref[...] = _dot(a_ref[...], w_ref[...]).astype(o_ref.dtype)


def _cast_body(x_ref, o_ref):
    o_ref[...] = x_ref[...].astype(o_ref.dtype)


def _cast_layer_body(x_ref, o_ref):
    o_ref[...] = x_ref[0].astype(o_ref.dtype)


def _cast_weight(w, layer, tn):
    _, k, n = w.shape
    return pl.pallas_call(
        _cast_layer_body,
        grid=(n // tn,),
        in_specs=[pl.BlockSpec((1, k, tn), lambda j: (layer, 0, j))],
        out_specs=pl.BlockSpec((k, tn), lambda j: (0, j)),
        out_shape=jax.ShapeDtypeStruct((k, n), BF16),
        compiler_params=_params(("parallel",)),
        name="cast_weight",
    )(w)


def _matmul(a, w, n, tm, tn, out_dtype=F32):
    m, k = a.shape
    tm = min(tm, m)
    return pl.pallas_call(
        _mm_body,
        grid=(n // tn, m // tm),
        in_specs=[pl.BlockSpec((tm, k), lambda j, i: (i, 0)),
                  pl.BlockSpec((k, tn), lambda j, i: (0, j))],
        out_specs=pl.BlockSpec((tm, tn), lambda j, i: (i, j)),
        out_shape=jax.ShapeDtypeStruct((m, n), out_dtype),
        compiler_params=_params(("parallel", "parallel")),
        name="in_proj",
    )(a, w)


def _rope_tables(n_tok, rope):
    if not rope:
        return jnp.ones((n_tok, LANES), F32), jnp.zeros((n_tok, LANES), F32)
    rows = n_tok // GRID_W
    row = jnp.repeat(jnp.arange(rows), GRID_W).astype(F32)
    col = jnp.tile(jnp.arange(GRID_W), rows).astype(F32)
    quarter = HEAD_DIM // 4
    inv = ROPE_THETA ** (-jnp.arange(quarter, dtype=F32) / quarter)
    ar = row[:, None] * inv
    ac = col[:, None] * inv
    cos64 = jnp.concatenate([jnp.cos(ar), jnp.cos(ar), jnp.cos(ac), jnp.cos(ac)], axis=-1)
    sin64 = jnp.concatenate([-jnp.sin(ar), jnp.sin(ar), -jnp.sin(ac), jnp.sin(ac)], axis=-1)
    return jnp.tile(cos64, (1, 2)), jnp.tile(sin64, (1, 2))


def _head_ones():
    i = np.arange(LANES)
    return jnp.asarray((i[:, None] // HEAD_DIM) == (i[None, :] // HEAD_DIM), BF16)


def _qk_prep_body(x_ref, g_ref, cos_ref, sin_ref, ones_ref, o_ref, *, scale):
    x = x_ref[0]
    hi, lo = _split(x * x)
    ssum = _dot(hi, ones_ref[...]) + _dot(lo, ones_ref[...])
    y = x * lax.rsqrt(ssum * (1.0 / HEAD_DIM) + EPS) * g_ref[...]
    lane = lax.broadcasted_iota(jnp.int32, y.shape, 1)
    quarter = HEAD_DIM // 4
    partner = jnp.where((lane & quarter) == 0, pltpu.roll(y, LANES - quarter, 1), pltpu.roll(y, quarter, 1))
    o = (y * cos_ref[...] + partner * sin_ref[...]) * scale
    if o_ref.shape[2] == LANES:
        o_ref[0] = o.astype(o_ref.dtype)
    else:
        low = lane < HEAD_DIM
        even = jnp.where(low, o, 0.0)
        odd = jnp.where(low, 0.0, o)
        kv_head = pl.program_id(1) // (WIN_GROUP // 2)
        first = jnp.where(kv_head == 0, even, pltpu.roll(even, HEAD_DIM, 1))
        second = jnp.where(kv_head == 0, pltpu.roll(odd, HEAD_DIM, 1), odd)
        o_ref[0] = jnp.concatenate([first, second], axis=-1).astype(o_ref.dtype)


def _qk_prep(ux, col_off, width, gain, tables, scale, pad_heads=False):
    b, l, _ = ux.shape
    cos_t, sin_t = tables
    c0 = col_off // LANES
    ow = 2 * LANES if pad_heads else LANES
    g = jnp.tile(gain.reshape(1, HEAD_DIM), (1, LANES // HEAD_DIM))
    return pl.pallas_call(
        functools.partial(_qk_prep_body, scale=scale),
        grid=(b, width // LANES),
        in_specs=[pl.BlockSpec((1, l, LANES), lambda bi, j: (bi, 0, c0 + j)),
                  pl.BlockSpec((1, LANES), lambda bi, j: (0, 0)),
                  pl.BlockSpec((l, LANES), lambda bi, j: (0, 0)),
                  pl.BlockSpec((l, LANES), lambda bi, j: (0, 0)),
                  pl.BlockSpec((LANES, LANES), lambda bi, j: (0, 0))],
        out_specs=pl.BlockSpec((1, l, ow), lambda bi, j: (bi, 0, j)),
        out_shape=jax.ShapeDtypeStruct((b, l, width // LANES * ow), BF16),
        compiler_params=_params(("parallel", "parallel")),
        name="qk_prep",
    )(ux, g, cos_t, sin_t, _head_ones())


def _cast_swap_body(x_ref, o_ref, s_ref):
    x = x_ref[0]
    o_ref[0] = x.astype(o_ref.dtype)
    s_ref[0] = pltpu.roll(x, HEAD_DIM, 1).astype(s_ref.dtype)


def _cast_cols(ux, col_off, width, swapped=False):
    b, l, _ = ux.shape
    c0 = col_off // LANES
    blk = pl.BlockSpec((1, l, LANES), lambda bi, j: (bi, 0, j))
    shp = jax.ShapeDtypeStruct((b, l, width), BF16)
    return pl.pallas_call(
        _cast_swap_body if swapped else _cast_body,
        grid=(b, width // LANES),
        in_specs=[pl.BlockSpec((1, l, LANES), lambda bi, j: (bi, 0, c0 + j))],
        out_specs=[blk, blk] if swapped else blk,
        out_shape=[shp, shp] if swapped else shp,
        compiler_params=_params(("parallel", "parallel")),
        name="cast_cols",
    )(ux)


def _nt_dot(a, b):
    return lax.dot_general(a, b, (((1,), (1,)), ((), ())), preferred_element_type=F32)


_WIN_STACK = (0, 2, 1, 3)


def _win_heads(q, sink_ref, h, n):
    heads = [h * WIN_GROUP + g for g in _WIN_STACK]
    qh = jnp.concatenate([q[:, a * LANES:(a + 1) * LANES] for a in heads], axis=0)
    sk = jnp.concatenate([jnp.full((n, 1), sink_ref[a], F32) for a in heads], axis=0)
    return qh, sk


def _win_tiles(o_low, o_high, n):
    low = lax.broadcasted_iota(jnp.int32, (n, LANES), 1) < HEAD_DIM
    return [jnp.where(low, o_low[t * n:(t + 1) * n], o_high[t * n:(t + 1) * n]) for t in range(2)]


WIN_STEP_BLOCKS = 4


def _win_attn_body(sink_ref, q_ref, km_ref, k0_ref, kp_ref, vm_ref, v0_ref, vp_ref, wm_ref, w0_ref, wp_ref,
                   kc_ref, vc_ref, wc_ref, o_ref, *, seq, nsub):
    i = pl.program_id(1)
    kslab = jnp.concatenate([km_ref[0], k0_ref[0], kp_ref[0]], axis=0)
    vslab = jnp.concatenate([vm_ref[0], v0_ref[0], vp_ref[0]], axis=0)
    wslab = jnp.concatenate([wm_ref[0], w0_ref[0], wp_ref[0]], axis=0)
    kc = kc_ref[0]
    qi = lax.broadcasted_iota(jnp.int32, (BLOCK, 3 * BLOCK), 0)
    kj = lax.broadcasted_iota(jnp.int32, (BLOCK, 3 * BLOCK), 1)
    band = jnp.abs(kj - BLOCK - qi) <= WINDOW
    half = WIN_GROUP // 2 * BLOCK
    for j in range(nsub):
        q = q_ref[0, j * BLOCK:(j + 1) * BLOCK]
        ws = slice(j * BLOCK, (j + 3) * BLOCK)
        kwin = kslab[ws]
        v_nat = (vslab[ws], vc_ref[0])
        v_swp = (wslab[ws], wc_ref[0])
        kpos = (i * nsub + j - 1) * BLOCK + kj
        valid = band & (kpos >= 0) & (kpos < seq)
        valid = jnp.concatenate([valid] * WIN_GROUP, axis=0)
        tiles = []
        for h in range(WIN_KV_HEADS):
            qh, sk = _win_heads(q, sink_ref, h, BLOCK)
            s_loc = jnp.where(valid, _nt_dot(qh, kwin), NEG_INF)
            s_ctx = _nt_dot(qh, kc)
            m = jnp.maximum(jnp.maximum(jnp.max(s_loc, axis=-1, keepdims=True),
                                        jnp.max(s_ctx, axis=-1, keepdims=True)), sk)
            p_loc = jnp.exp(s_loc - m)
            p_ctx = jnp.exp(s_ctx - m)
            denom = (jnp.sum(p_loc, axis=-1, keepdims=True) + jnp.sum(p_ctx, axis=-1, keepdims=True)
                     + jnp.exp(sk - m))
            p_loc = p_loc.astype(BF16)
            p_ctx = p_ctx.astype(BF16)
            v_low, v_high = (v_nat, v_swp) if h == 0 else (v_swp, v_nat)
            o_low = (_dot(p_loc[:half], v_low[0]) + _dot(p_ctx[:half], v_low[1])) / denom[:half]
            o_high = (_dot(p_loc[half:], v_high[0]) + _dot(p_ctx[half:], v_high[1])) / denom[half:]
            tiles.extend(_win_tiles(o_low, o_high, BLOCK))
        o_ref[0, j * BLOCK:(j + 1) * BLOCK] = jnp.concatenate(tiles, axis=-1).astype(o_ref.dtype)


def _win_attn(q, k, v, vs, kc, vc, vcs, sink):
    b, s, _ = q.shape
    c = kc.shape[1]
    nb = s // BLOCK
    nsub = math.gcd(nb, WIN_STEP_BLOCKS)
    rows = nsub * BLOCK
    kvw = WIN_KV_HEADS * HEAD_DIM
    cur = lambda bi, i: (bi, i, 0)
    prev = pl.BlockSpec((1, BLOCK, kvw), lambda bi, i: (bi, jnp.maximum(i * nsub - 1, 0), 0))
    main = pl.BlockSpec((1, rows, kvw), cur)
    nxt = pl.BlockSpec((1, BLOCK, kvw), lambda bi, i: (bi, jnp.minimum((i + 1) * nsub, nb - 1), 0))
    cspec = pl.BlockSpec((1, c, kvw), lambda bi, i: (bi, 0, 0))
    return pl.pallas_call(
        functools.partial(_win_attn_body, seq=s, nsub=nsub),
        grid=(b, nb // nsub),
        in_specs=[pl.BlockSpec(memory_space=pltpu.SMEM),
                  pl.BlockSpec((1, rows, WIN_HEADS * LANES), cur),
                  prev, main, nxt, prev, main, nxt, prev, main, nxt,
                  cspec, cspec, cspec],
        out_specs=pl.BlockSpec((1, rows, WIN_HEADS * HEAD_DIM), cur),
        out_shape=jax.ShapeDtypeStruct((b, s, WIN_HEADS * HEAD_DIM), BF16),
        compiler_params=_params(("parallel", "parallel")),
        name="win_attn",
    )(sink.astype(F32), q, k, k, k, v, v, v, vs, vs, vs, kc, vc, vcs)


def _ctx_win_attn_body(sink_ref, q_ref, k_ref, v_ref, w_ref, o_ref):
    q = q_ref[0]
    k = k_ref[0]
    c = q.shape[0]
    half = WIN_GROUP // 2 * c
    tiles = []
    for h in range(WIN_KV_HEADS):
        qh, sk = _win_heads(q, sink_ref, h, c)
        s = _nt_dot(qh, k)
        m = jnp.maximum(jnp.max(s, axis=-1, keepdims=True), sk)
        p = jnp.exp(s - m)
        denom = jnp.sum(p, axis=-1, keepdims=True) + jnp.exp(sk - m)
        p = p.astype(BF16)
        v_low, v_high = (v_ref[0], w_ref[0]) if h == 0 else (w_ref[0], v_ref[0])
        tiles.extend(_win_tiles(_dot(p[:half], v_low) / denom[:half], _dot(p[half:], v_high) / denom[half:], c))
    o_ref[0] = jnp.concatenate(tiles, axis=-1).astype(o_ref.dtype)


def _ctx_win_attn(q, k, v, vs, sink):
    b, c, _ = q.shape
    kvw = WIN_KV_HEADS * HEAD_DIM
    cspec = pl.BlockSpec((1, c, kvw), lambda bi: (bi, 0, 0))
    return pl.pallas_call(
        _ctx_win_attn_body,
        grid=(b,),
        in_specs=[pl.BlockSpec(memory_space=pltpu.SMEM),
                  pl.BlockSpec((1, c, WIN_HEADS * LANES), lambda bi: (bi, 0, 0)),
                  cspec, cspec, cspec],
        out_specs=pl.BlockSpec((1, c, WIN_HEADS * HEAD_DIM), lambda bi: (bi, 0, 0)),
        out_shape=jax.ShapeDtypeStruct((b, c, WIN_HEADS * HEAD_DIM), BF16),
        compiler_params=_params(("parallel",)),
        name="ctx_win_attn",
    )(sink.astype(F32), q, k, v, vs)


DIFF_CHAINS = 2


def _diff_attn_body(lam_ref, q_ref, k_ref, v_ref, g_ref, o_ref, *, lam_init):
    q = q_ref[0]
    k = k_ref[0]
    v = v_ref[0]
    lp = lam_ref[...]
    lam = (jnp.exp(jnp.sum(lp[0:1] * lp[1:2], axis=-1, keepdims=True))
           - jnp.exp(jnp.sum(lp[2:3] * lp[3:4], axis=-1, keepdims=True)) + lam_init)
    lane = lax.broadcasted_iota(jnp.int32, q.shape, 1)
    zero = jnp.zeros_like(q)
    q1 = jnp.where(lane < HEAD_DIM, q, zero)
    q2 = jnp.where(lane < HEAD_DIM, zero, q)
    rows = q.shape[0] // DIFF_CHAINS
    for t in range(DIFF_CHAINS):
        rs = slice(t * rows, (t + 1) * rows)
        s1 = _nt_dot(q1[rs], k)
        s2 = _nt_dot(q2[rs], k)
        p1 = jnp.exp2(s1 - jnp.max(s1, axis=-1, keepdims=True))
        p2 = jnp.exp2(s2 - jnp.max(s2, axis=-1, keepdims=True))
        l1 = jnp.sum(p1, axis=-1, keepdims=True)
        r21 = lam * l1 / jnp.sum(p2, axis=-1, keepdims=True)
        a = (p1 - p2 * r21).astype(BF16)
        o = _dot(a, v) / l1
        ms = jnp.mean(o * o, axis=-1, keepdims=True)
        o_ref[0, rs] = (o * lax.rsqrt(ms + EPS) * g_ref[...] * (1.0 - lam_init)).astype(o_ref.dtype)


def _diff_attn(q, k, v, lam_params, subln, lam_init, tq):
    b, nq, w = q.shape
    nk = k.shape[1]
    tq = min(tq, nq)
    hw = 2 * HEAD_DIM
    return pl.pallas_call(
        functools.partial(_diff_attn_body, lam_init=lam_init),
        grid=(b, DIFF_HEADS, nq // tq),
        in_specs=[pl.BlockSpec((4, HEAD_DIM), lambda bi, h, t: (0, 0)),
                  pl.BlockSpec((1, tq, hw), lambda bi, h, t: (bi, t, h)),
                  pl.BlockSpec((1, nk, hw), lambda bi, h, t: (bi, 0, h)),
                  pl.BlockSpec((1, nk, hw), lambda bi, h, t: (bi, 0, h)),
                  pl.BlockSpec((1, hw), lambda bi, h, t: (0, 0))],
        out_specs=pl.BlockSpec((1, tq, hw), lambda bi, h, t: (bi, t, h)),
        out_shape=jax.ShapeDtypeStruct((b, nq, w), BF16),
        compiler_params=_params(("parallel", "parallel", "parallel"), 56),
        name="diff_attn",
    )(lam_params, q, k, v, subln.reshape(1, hw))


def _hy_pre_body(x0_ref, x1_ref, v_ref, w0_ref, w1_ref, wv_ref, b0_ref, b1_ref, bv_ref, z_ref, x0c_ref):
    n = x0_ref.shape[1]
    row = lax.broadcasted_iota(jnp.int32, (n, 1), 0)

    def sconv(u_ref, w_ref, b_ref):
        u = u_ref[0]
        w = w_ref[...]
        up = jnp.where(row == 0, 0.0, pltpu.roll(u, 1, 0))
        un = jnp.where(row == n - 1, 0.0, pltpu.roll(u, n - 1, 0))
        return ((b_ref[...] + up * w[0:1]) + u * w[1:2]) + un * w[2:3]

    x0c_ref[0] = sconv(x0_ref, w0_ref, b0_ref)
    z_ref[0] = sconv(v_ref, wv_ref, bv_ref) * sconv(x1_ref, w1_ref, b1_ref)


def _hy_pre(ux, conv_w, conv_b):
    b, l, _ = ux.shape
    nc = HY_WIDTH // LANES
    c0 = OFF_HY // LANES
    ub = lambda seg: pl.BlockSpec((1, l, LANES), lambda bi, j: (bi, 0, c0 + seg * nc + j))
    wb = lambda seg: pl.BlockSpec((3, LANES), lambda bi, j: (0, seg * nc + j))
    bb = lambda seg: pl.BlockSpec((1, LANES), lambda bi, j: (0, seg * nc + j))
    out = pl.BlockSpec((1, l, LANES), lambda bi, j: (bi, 0, j))
    shp = jax.ShapeDtypeStruct((b, l, HY_WIDTH), F32)
    cb = conv_b.reshape(1, 3 * HY_WIDTH)
    return pl.pallas_call(
        _hy_pre_body,
        grid=(b, nc),
        in_specs=[ub(0), ub(1), ub(2), wb(0), wb(1), wb(2), bb(0), bb(1), bb(2)],
        out_specs=[out, out],
        out_shape=[shp, shp],
        compiler_params=_params(("parallel", "parallel")),
        name="hy_pre",
    )(ux, ux, ux, conv_w, conv_w, conv_w, cb, cb, cb)


def _hy_features(l):
    n = jnp.arange(l, dtype=F32)
    pos = jnp.concatenate([n, l - n])
    t = pos / max(l - 1, 1)
    w = 2.0 * math.pi * pos / l
    f = jnp.linspace(1e-4, HY_BANDS - 1, HY_BANDS, dtype=F32)
    flag = jnp.ones((2 * l, 1), F32).at[l].set(0.0)
    feat = jnp.concatenate([t[:, None], jnp.cos(w[:, None] * f), -jnp.sin(w[:, None] * f), flag], axis=-1)
    return jnp.pad(feat, ((0, 0), (0, LANES - HY_EMB - 1)))


def _hy_filter_body(f_ref, w1_ref, b1_ref, w2_ref, b2_ref, w3_ref, fr_ref, dec_ref, k_ref, sum_ref):
    first = (pl.program_id(0) == 0) & (pl.program_id(1) == 0)

    @pl.when(first)
    def _():
        sum_ref[...] = jnp.zeros_like(sum_ref)

    f = f_ref[...]
    fr = fr_ref[...]
    h = jnp.sin(fr[0:1] * (_dot3(f, w1_ref[...]) + b1_ref[...]))
    h = jnp.sin(fr[1:2] * (_dot3(h, w2_ref[...]) + b2_ref[...]))
    h = _dot3(h, w3_ref[...]) * jnp.exp(-f[:, 0:1] * jnp.abs(dec_ref[...]))
    h = h * f[:, HY_EMB:HY_EMB + 1]
    k_ref[...] = h
    sum_ref[...] += jnp.sum(jnp.abs(h), axis=0, keepdims=True)


def _hy_filter(l, p):
    tl = min(l, 1024)
    nt = l // tl
    order = p['hy_w1'].shape[1]
    w1 = jnp.zeros((LANES, order), F32).at[:HY_EMB].set(p['hy_w1'])
    const = lambda d, i: (0, 0)
    return pl.pallas_call(
        _hy_filter_body,
        grid=(2, nt),
        in_specs=[pl.BlockSpec((tl, LANES), lambda d, i: (d * nt + i, 0)),
                  pl.BlockSpec((LANES, order), const),
                  pl.BlockSpec((1, order), const),
                  pl.BlockSpec((order, order), const),
                  pl.BlockSpec((1, order), const),
                  pl.BlockSpec((order, HY_WIDTH), lambda d, i: (0, d)),
                  pl.BlockSpec((2, order), const),
                  pl.BlockSpec((1, HY_WIDTH), lambda d, i: (0, d))],
        out_specs=[pl.BlockSpec((tl, HY_WIDTH), lambda d, i: (d * nt + i, 0)),
                   pl.BlockSpec((1, HY_WIDTH), const)],
        out_shape=[jax.ShapeDtypeStruct((2 * l, HY_WIDTH), F32), jax.ShapeDtypeStruct((1, HY_WIDTH), F32)],
        compiler_params=_params(("arbitrary", "arbitrary")),
        name="hy_filter",
    )(_hy_features(l), w1, p['hy_b1'].reshape(1, order), p['hy_w2'], p['hy_b2'].reshape(1, order),
      p['hy_w3'], p['hy_freq'], p['hy_decay'].reshape(1, 2 * HY_WIDTH))


def _fft_consts(n_la):
    ka = np.arange(FFT_NA)[:, None]
    la = np.arange(n_la)[None, :]
    ang = 2.0 * math.pi * ka * la / FFT_NA
    f1 = np.concatenate([np.cos(ang), -np.sin(ang)], axis=0)
    lb = np.arange(FFT_NB)[None, :]
    th = 2.0 * math.pi * ka * lb / FFT_N
    tc, ts = np.cos(th), np.sin(th)
    a2 = 2.0 * math.pi * np.arange(FFT_NB)[:, None] * np.arange(FFT_NB)[None, :] / FFT_NB
    gre, gim = np.cos(a2), -np.sin(a2)
    g2 = np.block([[gre, gim], [-gim, gre]])
    g2i = np.block([[gre, -gim], [gim, gre]])
    lo = np.arange(FFT_NA // 2)[:, None]
    ph = 2.0 * math.pi * lo * np.arange(FFT_NA)[None, :] / FFT_NA
    f1i = np.concatenate([np.cos(ph), -np.sin(ph)], axis=1) / FFT_N
    return f1, tc, ts, g2, g2i, f1i


def _fft_fwd(x_ref, f1_ref, tc_ref, ts_ref, g2_ref, p2_ref, cc, dot):
    rhs = jnp.concatenate([x_ref[0, c] for c in range(cc)], axis=1)
    a = dot(f1_ref[...].astype(p2_ref.dtype), rhs.astype(p2_ref.dtype))
    tc = tc_ref[...]
    ts = ts_ref[...]
    for c in range(cc):
        are = a[0:FFT_NA, c * FFT_NB:(c + 1) * FFT_NB]
        aim = a[FFT_NA:2 * FFT_NA, c * FFT_NB:(c + 1) * FFT_NB]
        p2_ref[c * FFT_NA:(c + 1) * FFT_NA, 0:FFT_NB] = (are * tc + aim * ts).astype(p2_ref.dtype)
        p2_ref[c * FFT_NA:(c + 1) * FFT_NA, FFT_NB:2 * FFT_NB] = (aim * tc - are * ts).astype(p2_ref.dtype)
    return dot(p2_ref[...], g2_ref[...].astype(p2_ref.dtype))


def _fft_spec_body(x_ref, f1_ref, tc_ref, ts_ref, g2_ref, o_ref, p2_ref, *, cc):
    o_ref[...] = _fft_fwd(x_ref, f1_ref, tc_ref, ts_ref, g2_ref, p2_ref, cc, _dot3)


def _fft_conv_body(x_ref, kf_ref, f1_ref, tc_ref, ts_ref, g2_ref, g2i_ref, f1i_ref, o_ref, p2_ref, r2_ref, *, cc):
    z = _fft_fwd(x_ref, f1_ref, tc_ref, ts_ref, g2_ref, p2_ref, cc, _dot)
    kf = kf_ref[...]
    zre, zim = z[:, :FFT_NB], z[:, FFT_NB:]
    kre, kim = kf[:, :FFT_NB], kf[:, FFT_NB:]
    y = jnp.concatenate([zre * kre - zim * kim, zre * kim + zim * kre], axis=1).astype(BF16)
    u = _dot(y, g2i_ref[...].astype(BF16))
    tc = tc_ref[...]
    ts = ts_ref[...]
    for c in range(cc):
        ure = u[c * FFT_NA:(c + 1) * FFT_NA, 0:FFT_NB]
        uim = u[c * FFT_NA:(c + 1) * FFT_NA, FFT_NB:2 * FFT_NB]
        r2_ref[0:FFT_NA, c * FFT_NB:(c + 1) * FFT_NB] = (ure * tc - uim * ts).astype(BF16)
        r2_ref[FFT_NA:2 * FFT_NA, c * FFT_NB:(c + 1) * FFT_NB] = (ure * ts + uim * tc).astype(BF16)
    out = _dot(f1i_ref[...].astype(BF16), r2_ref[...])
    for c in range(cc):
        o_ref[0, c] = out[:, c * FFT_NB:(c + 1) * FFT_NB]


FFT_CC = 32


def _fft_spectrum(kt):
    nch, n_la, _ = kt.shape
    cc = FFT_CC
    f1, tc, ts, g2, _, _ = _fft_consts(n_la)
    const = lambda j: (0, 0)
    return pl.pallas_call(
        functools.partial(_fft_spec_body, cc=cc),
        grid=(nch // cc,),
        in_specs=[pl.BlockSpec((1, cc, n_la, FFT_NB), lambda j: (0, j, 0, 0)),
                  pl.BlockSpec((2 * FFT_NA, n_la), const),
                  pl.BlockSpec((FFT_NA, FFT_NB), const),
                  pl.BlockSpec((FFT_NA, FFT_NB), const),
                  pl.BlockSpec((2 * FFT_NB, 2 * FFT_NB), const)],
        out_specs=pl.BlockSpec((cc * FFT_NA, 2 * FFT_NB), lambda j: (j, 0)),
        out_shape=jax.ShapeDtypeStruct((nch * FFT_NA, 2 * FFT_NB), F32),
        scratch_shapes=[pltpu.VMEM((cc * FFT_NA, 2 * FFT_NB), F32)],
        compiler_params=_params(("parallel",)),
        name="fft_spectrum",
    )(kt[None], jnp.asarray(f1, F32), jnp.asarray(tc, F32), jnp.asarray(ts, F32), jnp.asarray(g2, F32))


def _fft_conv(zt, kf):
    b, nch, n_la, _ = zt.shape
    cc = FFT_CC
    f1, tc, ts, g2, g2i, f1i = _fft_consts(n_la)
    const = lambda j, bi: (0, 0)
    return pl.pallas_call(
        functools.partial(_fft_conv_body, cc=cc),
        grid=(nch // cc, b),
        in_specs=[pl.BlockSpec((1, cc, n_la, FFT_NB), lambda j, bi: (bi, j, 0, 0)),
                  pl.BlockSpec((cc * FFT_NA, 2 * FFT_NB), lambda j, bi: (j, 0)),
                  pl.BlockSpec((2 * FFT_NA, n_la), const),
                  pl.BlockSpec((FFT_NA, FFT_NB), const),
                  pl.BlockSpec((FFT_NA, FFT_NB), const),
                  pl.BlockSpec((2 * FFT_NB, 2 * FFT_NB), const),
                  pl.BlockSpec((2 * FFT_NB, 2 * FFT_NB), const),
                  pl.BlockSpec((FFT_NA // 2, 2 * FFT_NA), const)],
        out_specs=pl.BlockSpec((1, cc, n_la, FFT_NB), lambda j, bi: (bi, j, 0, 0)),
        out_shape=jax.ShapeDtypeStruct((b, nch, n_la, FFT_NB), F32),
        scratch_shapes=[pltpu.VMEM((cc * FFT_NA, 2 * FFT_NB), BF16),
                        pltpu.VMEM((2 * FFT_NA, cc * FFT_NB), BF16)],
        compiler_params=_params(("parallel", "parallel")),
        name="fft_conv",
    )(zt, kf, jnp.asarray(f1, F32), jnp.asarray(tc, F32), jnp.asarray(ts, F32),
      jnp.asarray(g2, F32), jnp.asarray(g2i, F32), jnp.asarray(f1i, F32))


def _dft_conv_body(z_ref, k_ref, fw_ref, fi_ref, o_ref):
    l = z_ref.shape[1]
    fw = fw_ref[...]
    n = fw.shape[1]
    kf = _dot3(fw, k_ref[...])
    zf = _dot3(fw[:, :l], z_ref[0])
    kre, kim = kf[:n], kf[n:]
    zre, zim = zf[:n], zf[n:]
    y = jnp.concatenate([zre * kre - zim * kim, zre * kim + zim * kre], axis=0)
    o_ref[0] = _dot3(fi_ref[...], y)


def _dft_conv(z, k):
    b, l, w = z.shape
    n = 2 * l
    ang = 2.0 * math.pi * np.arange(n)[:, None] * np.arange(n)[None, :] / n
    fw = np.concatenate([np.cos(ang), -np.sin(ang)], axis=0)
    ai = ang[:l]
    fi = np.concatenate([np.cos(ai), -np.sin(ai)], axis=1) / n
    return pl.pallas_call(
        _dft_conv_body,
        grid=(b,),
        in_specs=[pl.BlockSpec((1, l, w), lambda bi: (bi, 0, 0)),
                  pl.BlockSpec((n, w), lambda bi: (0, 0)),
                  pl.BlockSpec((2 * n, n), lambda bi: (0, 0)),
                  pl.BlockSpec((l, 2 * n), lambda bi: (0, 0))],
        out_specs=pl.BlockSpec((1, l, w), lambda bi: (bi, 0, 0)),
        out_shape=jax.ShapeDtypeStruct((b, l, w), F32),
        compiler_params=_params(("parallel",)),
        name="dft_conv",
    )(z, k, jnp.asarray(fw, F32), jnp.asarray(fi, F32))


def _hy_post_body(y_ref, z_ref, x0_ref, ks_ref, sk_ref, o_ref):
    y = y_ref[0] / (ks_ref[...] + EPS) + sk_ref[...] * z_ref[0]
    o_ref[0] = (y * x0_ref[0]).astype(o_ref.dtype)


def _hy_post(yraw, z, x0c, ksum, skip):
    b, l, w = z.shape
    ts = min(l, 1024)
    blk = pl.BlockSpec((1, ts, w), lambda bi, i: (bi, i, 0))
    vec = pl.BlockSpec((1, w), lambda bi, i: (0, 0))
    return pl.pallas_call(
        _hy_post_body,
        grid=(b, l // ts),
        in_specs=[blk, blk, blk, vec, vec],
        out_specs=blk,
        out_shape=jax.ShapeDtypeStruct((b, l, w), BF16),
        compiler_params=_params(("parallel", "parallel")),
        name="hy_post",
    )(yraw, z, x0c, ksum, skip.reshape(1, w))


def _hyena(ux, p):
    b, l, _ = ux.shape
    z, x0c = _hy_pre(ux, p['hy_conv_w'], p['hy_conv_b'])
    k_raw, ksum = _hy_filter(l, p)
    if 2 * l == FFT_N:
        kt = k_raw.T.reshape(HY_WIDTH, FFT_NA, FFT_NB)
        kf = _fft_spectrum(kt)
        zt = jnp.swapaxes(z, 1, 2).reshape(b, HY_WIDTH, l // FFT_NB, FFT_NB)
        yt = _fft_conv(zt, kf)
        yraw = jnp.swapaxes(yt.reshape(b, HY_WIDTH, l), 1, 2)
    else:
        yraw = _dft_conv(z, k_raw)
    return _hy_post(yraw, z, x0c, ksum, p['hy_skip'])


def _pool_body(u_ref, w_ref, sc_ref, o_ref, *, win):
    u = u_ref[0]
    n = u.shape[0]
    lo = win // 2
    hi = win - 1 - lo
    row = lax.broadcasted_iota(jnp.int32, (n, 1), 0)
    acc = u
    for j in range(1, lo + 1):
        acc = acc + jnp.where(row >= j, pltpu.roll(u, j, 0), 0.0)
    for j in range(1, hi + 1):
        acc = acc + jnp.where(row < n - j, pltpu.roll(u, n - j, 0), 0.0)
    cnt = (jnp.minimum(row + hi + 1, n) - jnp.maximum(row - lo, 0)).astype(F32)
    d = acc / cnt - u
    o_ref[0] = (_dot(d.astype(BF16), w_ref[0].astype(BF16)) * sc_ref[...]).astype(o_ref.dtype)


def _pool_mix(ux, w_grp, scale):
    b, l, _ = ux.shape
    c0 = OFF_PL // LANES
    outs = []
    for gi, win in enumerate(POOL_WINDOWS):
        outs.append(pl.pallas_call(
            functools.partial(_pool_body, win=win),
            grid=(b,),
            in_specs=[pl.BlockSpec((1, l, LANES), lambda bi, gi=gi: (bi, 0, c0 + gi)),
                      pl.BlockSpec((1, LANES, LANES), lambda bi, gi=gi: (gi, 0, 0)),
                      pl.BlockSpec((1, LANES), lambda bi, gi=gi: (0, gi))],
            out_specs=pl.BlockSpec((1, l, LANES), lambda bi: (bi, 0, 0)),
            out_shape=jax.ShapeDtypeStruct((b, l, LANES), BF16),
            compiler_params=_params(("parallel",)),
            name=f"pool_mix{win}",
        )(ux, w_grp, scale.reshape(1, -1)))
    return jnp.concatenate(outs, axis=-1)


def _merge1_body(hx_ref, y0, y1, y2, y3, g0, g1, g2, g3, b0, b1, b2, b3, o_ref):
    hx = hx_ref[...]
    acc = None
    for y_ref, g_ref, b_ref in ((y0, g0, b0), (y1, g1, b1), (y2, g2, b2), (y3, g3, b3)):
        t = jax.nn.sigmoid(_dot(hx, g_ref[...])) * _dot(y_ref[...], b_ref[0, 0])
        acc = t if acc is None else acc + t
    o_ref[...] = acc.astype(o_ref.dtype)


def _merge1(hx, ys, w_in, gate_off, w_branch, layer, tm=1024, tn=256):
    m, d = hx.shape
    tm = min(tm, m)
    nj = d // tn
    g0 = gate_off // tn
    row = lambda i, j: (i, 0)
    gspec = lambda br: pl.BlockSpec((d, tn), lambda i, j, br=br: (0, g0 + br * nj + j))
    bspec = lambda br: pl.BlockSpec((1, 1, BRANCH_W, tn), lambda i, j, br=br: (layer, br, 0, j))
    return pl.pallas_call(
        _merge1_body,
        grid=(m // tm, nj),
        in_specs=([pl.BlockSpec((tm, d), row)] + [pl.BlockSpec((tm, BRANCH_W), row)] * N_BRANCH
                  + [gspec(br) for br in range(N_BRANCH)] + [bspec(br) for br in range(N_BRANCH)]),
        out_specs=pl.BlockSpec((tm, tn), lambda i, j: (i, j)),
        out_shape=jax.ShapeDtypeStruct((m, d), BF16),
        compiler_params=_params(("parallel", "parallel")),
        name="merge_gate",
    )(hx, *ys, w_in, w_in, w_in, w_in, w_branch, w_branch, w_branch, w_branch)


def _merge2_body(a_ref, w_ref, x_ref, g_ref, o_ref):
    o_ref[0] = x_ref[0] + g_ref[0] * _dot(a_ref[0], w_ref[0])


def _merge2(acc, w_out, layer, x, gate, tm=512):
    b, l, d = x.shape
    tm = min(tm, l)
    blk = pl.BlockSpec((1, tm, d), lambda bi, i: (bi, i, 0))
    return pl.pallas_call(
        _merge2_body,
        grid=(b, l // tm),
        in_specs=[blk, pl.BlockSpec((1, d, d), lambda bi, i: (layer, 0, 0)), blk,
                  pl.BlockSpec((1, 1, d), lambda bi, i: (bi, 0, 0))],
        out_specs=blk,
        out_shape=jax.ShapeDtypeStruct((b, l, d), F32),
        compiler_params=_params(("parallel", "parallel")),
        name="merge_out",
    )(acc, w_out, x, gate)


def _store_token_major(ref, x):
    n, d = x.shape
    pitch = d // LANES
    for j in range(pitch):
        ref[pl.ds(j, n, stride=pitch), :] = x[:, j * LANES:(j + 1) * LANES]


def _load_token_major(ref, n):
    pitch = ref.shape[0] // n
    return jnp.concatenate([ref[pl.ds(j, n, stride=pitch), :] for j in range(pitch)], axis=1)


def _route_body(x_ref, g_ref, sc_ref, sh_ref, wr_ref, br_ref, h_ref, e_ref, w_ref):
    x = x_ref[0]
    ms = jnp.mean(x * x, axis=-1, keepdims=True)
    h = (x * lax.rsqrt(ms + EPS) * g_ref[...]) * (1.0 + sc_ref[0]) + sh_ref[0]
    _store_token_major(h_ref, h)
    logits = _dot3(h, wr_ref[...]) + br_ref[...]
    lane = lax.broadcasted_iota(jnp.int32, logits.shape, 1)
    big = jnp.int32(LANES)
    gl = jnp.where(lane < N_GROUPS, logits, -jnp.inf)
    ge = jnp.exp(gl - jnp.max(gl, axis=-1, keepdims=True))
    pg = ge / jnp.sum(ge, axis=-1, keepdims=True)
    pg_top = jnp.max(pg, axis=-1, keepdims=True)
    g_idx = jnp.min(jnp.where((pg == pg_top) & (lane < N_GROUPS), lane, big), axis=-1, keepdims=True)
    first = N_GROUPS + EXPERTS_PER_GROUP * g_idx
    le = jnp.where((lane >= first) & (lane < first + EXPERTS_PER_GROUP), logits, -jnp.inf)
    v1 = jnp.max(le, axis=-1, keepdims=True)
    i1 = jnp.min(jnp.where(le == v1, lane, big), axis=-1, keepdims=True)
    le2 = jnp.where(lane == i1, -jnp.inf, le)
    v2 = jnp.max(le2, axis=-1, keepdims=True)
    i2 = jnp.min(jnp.where(le2 == v2, lane, big), axis=-1, keepdims=True)
    e2 = jnp.exp(v2 - v1)
    den = 1.0 + e2
    e_ref[0] = jnp.where(lane == 0, i1 - N_GROUPS, jnp.where(lane == 1, i2 - N_GROUPS, 0))
    w_ref[0] = jnp.where(lane == 0, (1.0 / den) * pg_top, jnp.where(lane == 1, (e2 / den) * pg_top, 0.0))


def _route(x, gain, scale, shift, w_router, b_router, tm=512):
    b, l, d = x.shape
    tm = min(tm, l)
    blk = pl.BlockSpec((1, tm, d), lambda bi, i: (bi, i, 0))
    mod = pl.BlockSpec((1, 1, d), lambda bi, i: (bi, 0, 0))
    sm = pl.BlockSpec((1, tm, LANES), lambda bi, i: (bi, i, 0))
    return pl.pallas_call(
        _route_body,
        grid=(b, l // tm),
        in_specs=[blk, pl.BlockSpec((1, d), lambda bi, i: (0, 0)), mod, mod,
                  pl.BlockSpec((d, LANES), lambda bi, i: (0, 0)),
                  pl.BlockSpec((1, LANES), lambda bi, i: (0, 0))],
        out_specs=[pl.BlockSpec((tm * (d // LANES), LANES), lambda bi, i: (bi * (l // tm) + i, 0)), sm, sm],
        out_shape=[jax.ShapeDtypeStruct((b * l * (d // LANES), LANES), F32),
                   jax.ShapeDtypeStruct((b, l, LANES), jnp.int32),
                   jax.ShapeDtypeStruct((b, l, LANES), F32)],
        compiler_params=_params(("parallel", "parallel")),
        name="route",
    )(x, gain.reshape(1, d), scale, shift, w_router, b_router)


def _row_scatter_body(idx_ref, x_ref, dst, buf, sem, *, pitch):
    i = pl.program_id(0)
    n = pl.num_programs(0)
    tm = idx_ref.shape[2]
    slot = i % 2

    def wait(s):
        pltpu.make_async_copy(buf.at[s], dst.at[pl.ds(0, tm * pitch)], sem.at[s]).wait()

    @pl.when(i >= 2)
    def _():
        wait(slot)

    buf[slot] = x_ref[...]
    for r in range(tm):
        pltpu.make_async_copy(buf.at[slot, pl.ds(r * pitch, pitch)], dst.at[pl.ds(idx_ref[0, 0, r], pitch)],
                              sem.at[slot]).start(priority=r % 2)

    @pl.when(i == n - 1)
    def _():
        wait(slot)

        @pl.when(n >= 2)
        def _():
            wait(1 - slot)


def _row_scatter(src, idx, pitch):
    nblk, tm = idx.shape
    rows = tm * pitch
    return pl.pallas_call(
        functools.partial(_row_scatter_body, pitch=pitch),
        grid=(nblk,),
        in_specs=[pl.BlockSpec((1, 1, tm), lambda i: (i, 0, 0), memory_space=pltpu.SMEM),
                  pl.BlockSpec((rows, LANES), lambda i: (i, 0))],
        out_specs=pl.BlockSpec(memory_space=pl.ANY),
        out_shape=jax.ShapeDtypeStruct(src.shape, src.dtype),
        scratch_shapes=[pltpu.VMEM((2, rows, LANES), src.dtype), pltpu.SemaphoreType.DMA((2,))],
        compiler_params=_params(("arbitrary",)),
        name="row_scatter",
    )(idx.reshape(nblk, 1, tm), src)


def _ffn_body(blk_e_ref, nused_ref, gi_ref, gn_ref, sw_ref, w1_ref, w3_ref, w2_ref, h_hbm, o_ref, xbuf, sem, *, pitch):
    i = pl.program_id(0)
    n = pl.num_programs(0)
    tm = sw_ref.shape[0]
    slot = i % 2

    def gather(idx_ref, s):
        for r in range(tm):
            pltpu.make_async_copy(h_hbm.at[pl.ds(idx_ref[0, 0, r], pitch)], xbuf.at[s, pl.ds(r * pitch, pitch)],
                                  sem.at[s]).start(priority=r % 2)

    @pl.when(i == 0)
    def _():
        gather(gi_ref, 0)

    @pl.when(i + 1 < n)
    def _():
        gather(gn_ref, 1 - slot)

    pltpu.make_async_copy(h_hbm.at[pl.ds(0, tm * pitch)], xbuf.at[slot], sem.at[slot]).wait()

    @pl.when(i < nused_ref[0])
    def _():
        x = _load_token_major(xbuf.at[slot], tm).astype(BF16)
        a = _dot(x, w1_ref[0, 0])
        h = ((a * jax.nn.sigmoid(a)) * _dot(x, w3_ref[0, 0])).astype(BF16)
        _store_token_major(o_ref, _dot(h, w2_ref[0, 0]) * sw_ref[...])

    @pl.when(i >= nused_ref[0])
    def _():
        o_ref[...] = jnp.zeros_like(o_ref)


def _moe_experts(h2t, blk_e, nused, gidx, sidx, slot_w, w1, w3, w2, layer):
    nblk, tm = gidx.shape
    d = w1.shape[2]
    ff = w1.shape[3]
    pitch = d // LANES
    rows = tm * pitch
    smem_blk = lambda f: pl.BlockSpec((1, 1, tm), f, memory_space=pltpu.SMEM)
    grid_spec = pltpu.PrefetchScalarGridSpec(
        num_scalar_prefetch=2,
        grid=(nblk,),
        in_specs=[smem_blk(lambda i, be, nu: (i, 0, 0)),
                  smem_blk(lambda i, be, nu: (jnp.minimum(i + 1, nblk - 1), 0, 0)),
                  pl.BlockSpec((tm, 1), lambda i, be, nu: (i, 0)),
                  pl.BlockSpec((1, 1, d, ff), lambda i, be, nu: (layer, be[i], 0, 0)),
                  pl.BlockSpec((1, 1, d, ff), lambda i, be, nu: (layer, be[i], 0, 0)),
                  pl.BlockSpec((1, 1, ff, d), lambda i, be, nu: (layer, be[i], 0, 0)),
                  pl.BlockSpec(memory_space=pl.ANY)],
        out_specs=pl.BlockSpec((rows, LANES), lambda i, be, nu: (i, 0)),
        scratch_shapes=[pltpu.VMEM((2, rows, LANES), F32), pltpu.SemaphoreType.DMA((2,))],
    )
    gidx3 = (gidx * pitch).reshape(nblk, 1, tm)
    yb = pl.pallas_call(
        functools.partial(_ffn_body, pitch=pitch),
        grid_spec=grid_spec,
        out_shape=jax.ShapeDtypeStruct((nblk * rows, LANES), F32),
        compiler_params=_params(("arbitrary",), 56),
        name="moe_ffn",
    )(blk_e, nused, gidx3, gidx3, slot_w, w1, w3, w2, h2t)
    return _row_scatter(yb, sidx * pitch, pitch)


def _moe_plan(e_idx, e_w, tm):
    n = e_idx.shape[0]
    m = n * TOP_K
    e_flat = e_idx.reshape(m)
    w_flat = e_w.reshape(m)
    order = jnp.argsort(e_flat).astype(jnp.int32)
    experts = jnp.arange(N_EXPERTS, dtype=jnp.int32)
    counts = jnp.sum((e_flat[:, None] == experts[None, :]).astype(jnp.int32), axis=0)
    padded = (counts + tm - 1) // tm * tm
    pad_end = jnp.cumsum(padded)
    pad_start = pad_end - padded
    cnt_start = jnp.cumsum(counts) - counts
    nblk = -(-(m + N_EXPERTS * (tm - 1)) // tm)
    blk_first = jnp.arange(nblk, dtype=jnp.int32) * tm
    blk_e = jnp.minimum(jnp.sum((pad_end[None, :] <= blk_first[:, None]).astype(jnp.int32), axis=1), N_EXPERTS - 1)
    rank = blk_first[:, None] + jnp.arange(tm, dtype=jnp.int32)[None, :] - pad_start[blk_e][:, None]
    valid = rank < counts[blk_e][:, None]
    src = order[jnp.clip(cnt_start[blk_e][:, None] + rank, 0, m - 1)]
    tok = jnp.where(valid, src // TOP_K, 0)
    slot_w = jnp.where(valid, w_flat[src], 0.0)
    pad_rank = (jnp.cumsum(jnp.logical_not(valid).reshape(-1).astype(jnp.int32)) - 1).reshape(nblk, tm)
    dst = jnp.where(valid, (src % TOP_K) * n + src // TOP_K, m + pad_rank)
    nused = (pad_end[-1] // tm).astype(jnp.int32).reshape(1)
    return blk_e, nused, tok.astype(jnp.int32), dst.astype(jnp.int32), slot_w.reshape(nblk * tm, 1).astype(F32)


def _combine_body(x_ref, y0_ref, y1_ref, g_ref, o_ref):
    tm = x_ref.shape[1]
    o_ref[0] = x_ref[0] + g_ref[0] * (_load_token_major(y0_ref, tm) + _load_token_major(y1_ref, tm))


def _moe_combine(x, yk, gate, row_off, n_tok, tm=256):
    b, l, d = x.shape
    tm = min(tm, l)
    assert l % tm == 0 and row_off % tm == 0 and n_tok % tm == 0
    nl = l // tm
    rows = tm * (d // LANES)
    blk = pl.BlockSpec((1, tm, d), lambda bi, i: (bi, i, 0))
    o0 = row_off // tm
    o1 = (n_tok + row_off) // tm
    return pl.pallas_call(
        _combine_body,
        grid=(b, nl),
        in_specs=[blk,
                  pl.BlockSpec((rows, LANES), lambda bi, i: (o0 + bi * nl + i, 0)),
                  pl.BlockSpec((rows, LANES), lambda bi, i: (o1 + bi * nl + i, 0)),
                  pl.BlockSpec((1, 1, d), lambda bi, i: (bi, 0, 0))],
        out_specs=blk,
        out_shape=jax.ShapeDtypeStruct((b, l, d), F32),
        compiler_params=_params(("parallel", "parallel")),
        name="moe_combine",
    )(x, yk, yk, gate)


def _mixers(ux, ucx, p, lam_init, latent):
    b, l, _ = ux.shape
    rope_q = _rope_tables(l, latent)
    no_rope = _rope_tables(ucx.shape[1], False)
    scale = HEAD_DIM ** -0.5
    qw = _qk_prep(ux, OFF_WQ, 512, p['win_qnorm'], rope_q, scale, pad_heads=True)
    qd = _qk_prep(ux, OFF_DQ, 512, p['diff_qnorm'], rope_q, scale * math.log2(math.e))
    kcw = _qk_prep(ucx, OFF_WK, 128, p['win_knorm'], no_rope, 1.0)
    vcw, vcws = _cast_cols(ucx, OFF_WV, 128, swapped=True)
    kcd = _qk_prep(ucx, OFF_DK, 512, p['diff_knorm'], no_rope, 1.0)
    vcd = _cast_cols(ucx, OFF_DV, 512)
    if latent:
        kw = _qk_prep(ux, OFF_WK, 128, p['win_knorm'], rope_q, 1.0)
        vw, vws = _cast_cols(ux, OFF_WV, 128, swapped=True)
        kd = _qk_prep(ux, OFF_DK, 512, p['diff_knorm'], rope_q, 1.0)
        vd = _cast_cols(ux, OFF_DV, 512)
        y_win = _win_attn(qw, kw, vw, vws, kcw, vcw, vcws, p['win_sink'])
        k_all = jnp.concatenate([kd, kcd], axis=1)
        v_all = jnp.concatenate([vd, vcd], axis=1)
    else:
        y_win = _ctx_win_attn(qw, kcw, vcw, vcws, p['win_sink'])
        k_all, v_all = kcd, vcd
    y_diff = _diff_attn(qd, k_all, v_all, p['diff_lambda'], p['diff_subln'], lam_init, 512)
    y_hy = _hyena(ux, p)
    y_pool = _pool_mix(ux, p['pool_proj'], p['pool_scale'])
    return y_win, y_diff, y_hy, y_pool


def _layer(x, ctx, mod, p, layer_idx, update_ctx):
    b, s, d = x.shape
    c = ctx.shape[1]
    lam_init = 0.8 - 0.6 * math.exp(-0.3 * layer_idx)
    chunks = [mod[:, i * d:(i + 1) * d] for i in range(6)]
    sh1, sc1, g1, sh2, sc2, g2 = [t[:b].reshape(b, 1, d) for t in chunks]
    csh1, csc1, cg1, csh2, csc2, cg2 = [jnp.broadcast_to(t[b:b + 1].reshape(1, 1, d), (b, 1, d)) for t in chunks]

    w_in = _cast_weight(p['w_in'], layer_idx, 7 * LANES)
    w_branch = p['w_branch']
    w_out = p['w_out']
    tn_in = OFF_GT // 2

    hx = _prenorm(x, p['norm1'], sc1, sh1)
    hc = _prenorm(ctx, p['norm1'], csc1, csh1)
    ux = _matmul(hx.reshape(b * s, d), w_in, OFF_GT, 512, tn_in).reshape(b, s, OFF_GT)
    ucx = _matmul(hc.reshape(b * c, d), w_in, OFF_GT, 512, tn_in).reshape(b, c, OFF_GT)

    ys = _mixers(ux, ucx, p, lam_init, True)
    acc = _merge1(hx.reshape(b * s, d), [y.reshape(b * s, BRANCH_W) for y in ys], w_in, OFF_GT, w_branch, layer_idx)
    x = _merge2(acc.reshape(b, s, d), w_out, layer_idx, x, g1)
    if update_ctx:
        ycs = _mixers(ucx, ucx, p, lam_init, False)
        acc_c = _merge1(hc.reshape(b * c, d), [y.reshape(b * c, BRANCH_W) for y in ycs], w_in, OFF_GT, w_branch, layer_idx)
        ctx = _merge2(acc_c.reshape(b, c, d), w_out, layer_idx, ctx, cg1)

    w_router = jnp.zeros((d, LANES), F32).at[:, :N_GROUPS].set(p['r_group_w'])
    w_router = w_router.at[:, N_GROUPS:N_GROUPS + N_EXPERTS].set(p['r_expert_w'])
    b_router = jnp.zeros((1, LANES), F32).at[0, :N_GROUPS].set(p['r_group_b'])
    b_router = b_router.at[0, N_GROUPS:N_GROUPS + N_EXPERTS].set(p['r_expert_b'])
    h2, e_idx, e_w = _route(x, p['norm2'], sc2, sh2, w_router, b_router)
    e_idx = e_idx.reshape(b * s, LANES)[:, :TOP_K]
    e_w = e_w.reshape(b * s, LANES)[:, :TOP_K]
    if update_ctx:
        h2c, ec_idx, ec_w = _route(ctx, p['norm2'], csc2, csh2, w_router, b_router)
        h2 = jnp.concatenate([h2, h2c], axis=0)
        e_idx = jnp.concatenate([e_idx, ec_idx.reshape(b * c, LANES)[:, :TOP_K]], axis=0)
        e_w = jnp.concatenate([e_w, ec_w.reshape(b * c, LANES)[:, :TOP_K]], axis=0)
    n_tok = e_idx.shape[0]
    blk_e, nused, gidx, sidx, slot_w = _moe_plan(e_idx, e_w, MOE_TM)
    yk = _moe_experts(h2, blk_e, nused, gidx, sidx, slot_w, p['e_w1'], p['e_w3'], p['e_w2'], layer_idx)
    x = _moe_combine(x, yk, g2, 0, n_tok)
    if update_ctx:
        ctx = _moe_combine(ctx, yk, cg2, b * s, n_tok)
    return x, ctx


def kernel(x, c, ctx, c_ctx, w_mod, b_mod, norm1, norm2, w_in, win_sink, win_qnorm, win_knorm, diff_qnorm, diff_knorm, diff_lambda, diff_subln, hy_conv_w, hy_conv_b, hy_w1, hy_b1, hy_w2, hy_b2, hy_w3, hy_freq, hy_decay, hy_skip, pool_proj, pool_scale, w_branch, w_out, r_group_w, r_group_b, r_expert_w, r_expert_b, e_w1, e_w3, e_w2):
    depth = w_mod.shape[0]
    b, _, d = x.shape
    cs = jnp.zeros((8, d), F32).at[:b].set(c).at[b].set(c_ctx)
    stacked = dict(w_in=w_in, w_branch=w_branch.astype(BF16), w_out=w_out.astype(BF16),
                   e_w1=e_w1.astype(BF16), e_w3=e_w3.astype(BF16), e_w2=e_w2.astype(BF16))
    for l in range(depth):
        p = dict(norm1=norm1[l], norm2=norm2[l],
                 win_sink=win_sink[l], win_qnorm=win_qnorm[l], win_knorm=win_knorm[l],
                 diff_qnorm=diff_qnorm[l], diff_knorm=diff_knorm[l], diff_lambda=diff_lambda[l],
                 diff_subln=diff_subln[l], hy_conv_w=hy_conv_w[l], hy_conv_b=hy_conv_b[l],
                 hy_w1=hy_w1[l], hy_b1=hy_b1[l], hy_w2=hy_w2[l], hy_b2=hy_b2[l], hy_w3=hy_w3[l],
                 hy_freq=hy_freq[l], hy_decay=hy_decay[l], hy_skip=hy_skip[l],
                 pool_proj=pool_proj[l], pool_scale=pool_scale[l],
                 r_group_w=r_group_w[l], r_group_b=r_group_b[l], r_expert_w=r_expert_w[l],
                 r_expert_b=r_expert_b[l], **stacked)
        mod = _modulation(cs, w_mod, b_mod, l)
        x, ctx = _layer(x, ctx, mod, p, l, l < depth - 1)
    return x
```

```python
import functools
import math

import jax
import jax.numpy as jnp
import numpy as np
from jax import lax
from jax.experimental import pallas as pl
from jax.experimental.pallas import tpu as pltpu

F32 = jnp.float32
BF16 = jnp.bfloat16

GRID_W = 64
HEAD_DIM = 64
ROPE_THETA = 10000.0
EPS = 1e-6
NEG_INF = -1e30
BLOCK = 128
WINDOW = 128
LANES = 128

WIN_HEADS = 8
WIN_KV_HEADS = 2
WIN_GROUP = WIN_HEADS // WIN_KV_HEADS
DIFF_HEADS = 4
HY_WIDTH = 512
HY_EMB = 33
HY_BANDS = 16
POOL_WINDOWS = (2, 4, 8, 16)
N_BRANCH = 4
BRANCH_W = 512

OFF_WQ = 0
OFF_WK = 512
OFF_WV = 640
OFF_DQ = 768
OFF_DK = 1280
OFF_DV = 1792
OFF_HY = 2304
OFF_PL = 3840
OFF_GT = 4352

N_GROUPS = 4
EXPERTS_PER_GROUP = 8
N_EXPERTS = 32
TOP_K = 2
MOE_TM = 256

FFT_NA = 64
FFT_NB = 128
FFT_N = FFT_NA * FFT_NB


def _params(sem, mib=48):
    return pltpu.CompilerParams(dimension_semantics=sem, vmem_limit_bytes=mib * 2**20)


def _dot(a, b):
    return jnp.dot(a, b, preferred_element_type=F32)


def _split(a):
    hi = a.astype(BF16)
    lo = (a - hi.astype(F32)).astype(BF16)
    return hi, lo


def _dot3(a, b):
    ah, al = _split(a)
    bh, bl = _split(b)
    return _dot(ah, bh) + (_dot(ah, bl) + _dot(al, bh))


def _mod_body(c_ref, w_ref, b_ref, o_ref):
    c = c_ref[...]
    a = (c * jax.nn.sigmoid(c)).astype(BF16)
    o_ref[...] = _dot(a, w_ref[0].astype(BF16)) + b_ref[0]


def _modulation(cs, w_mod, b_mod, layer):
    depth, d, n = w_mod.shape
    tn = 1024
    return pl.pallas_call(
        _mod_body,
        grid=(n // tn,),
        in_specs=[pl.BlockSpec((8, d), lambda j: (0, 0)),
                  pl.BlockSpec((1, d, tn), lambda j: (layer, 0, j)),
                  pl.BlockSpec((1, 1, tn), lambda j: (layer, 0, j))],
        out_specs=pl.BlockSpec((8, tn), lambda j: (0, j)),
        out_shape=jax.ShapeDtypeStruct((8, n), F32),
        compiler_params=_params(("parallel",)),
        name="modulation",
    )(cs, w_mod, b_mod.reshape(depth, 1, n))


def _prenorm_body(x_ref, g_ref, sc_ref, sh_ref, o_ref):
    x = x_ref[0]
    ms = jnp.mean(x * x, axis=-1, keepdims=True)
    y = x * lax.rsqrt(ms + EPS) * g_ref[...]
    o_ref[0] = (y * (1.0 + sc_ref[0]) + sh_ref[0]).astype(o_ref.dtype)


def _prenorm(x, gain, scale, shift, out_dtype=BF16):
    b, l, d = x.shape
    ts = min(l, 512)
    return pl.pallas_call(
        _prenorm_body,
        grid=(b, l // ts),
        in_specs=[pl.BlockSpec((1, ts, d), lambda bi, i: (bi, i, 0)),
                  pl.BlockSpec((1, d), lambda bi, i: (0, 0)),
                  pl.BlockSpec((1, 1, d), lambda bi, i: (bi, 0, 0)),
                  pl.BlockSpec((1, 1, d), lambda bi, i: (bi, 0, 0))],
        out_specs=pl.BlockSpec((1, ts, d), lambda bi, i: (bi, i, 0)),
        out_shape=jax.ShapeDtypeStruct((b, l, d), out_dtype),
        compiler_params=_params(("parallel", "parallel")),
        name="prenorm",
    )(x, gain.reshape(1, d), scale, shift)


def _mm_body(a_ref, w_ref, o_ref):
    o_ref[...] = _dot(a_ref[...], w_ref[...]).astype(o_ref.dtype)


def _cast_body(x_ref, o_ref):
    o_ref[...] = x_ref[...].astype(o_ref.dtype)


def _cast_layer_body(x_ref, o_ref):
    o_ref[...] = x_ref[0].astype(o_ref.dtype)


def _cast_weight(w, layer, tn):
    _, k, n = w.shape
    return pl.pallas_call(
        _cast_layer_body,
        grid=(n // tn,),
        in_specs=[pl.BlockSpec((1, k, tn), lambda j: (layer, 0, j))],
        out_specs=pl.BlockSpec((k, tn), lambda j: (0, j)),
        out_shape=jax.ShapeDtypeStruct((k, n), BF16),
        compiler_params=_params(("parallel",)),
        name="cast_weight",
    )(w)


def _matmul(a, w, n, tm, tn, out_dtype=F32):
    m, k = a.shape
    tm = min(tm, m)
    return pl.pallas_call(
        _mm_body,
        grid=(n // tn, m // tm),
        in_specs=[pl.BlockSpec((tm, k), lambda j, i: (i, 0)),
                  pl.BlockSpec((k, tn), lambda j, i: (0, j))],
        out_specs=pl.BlockSpec((tm, tn), lambda j, i: (i, j)),
        out_shape=jax.ShapeDtypeStruct((m, n), out_dtype),
        compiler_params=_params(("parallel", "parallel")),
        name="in_proj",
    )(a, w)


def _rope_tables(n_tok, rope):
    if not rope:
        return jnp.ones((n_tok, LANES), F32), jnp.zeros((n_tok, LANES), F32)
    rows = n_tok // GRID_W
    row = jnp.repeat(jnp.arange(rows), GRID_W).astype(F32)
    col = jnp.tile(jnp.arange(GRID_W), rows).astype(F32)
    quarter = HEAD_DIM // 4
    inv = ROPE_THETA ** (-jnp.arange(quarter, dtype=F32) / quarter)
    ar = row[:, None] * inv
    ac = col[:, None] * inv
    cos64 = jnp.concatenate([jnp.cos(ar), jnp.cos(ar), jnp.cos(ac), jnp.cos(ac)], axis=-1)
    sin64 = jnp.concatenate([-jnp.sin(ar), jnp.sin(ar), -jnp.sin(ac), jnp.sin(ac)], axis=-1)
    return jnp.tile(cos64, (1, 2)), jnp.tile(sin64, (1, 2))


def _head_ones():
    i = np.arange(LANES)
    return jnp.asarray((i[:, None] // HEAD_DIM) == (i[None, :] // HEAD_DIM), BF16)


def _qk_prep_body(x_ref, g_ref, cos_ref, sin_ref, ones_ref, o_ref, *, scale):
    x = x_ref[0]
    hi, lo = _split(x * x)
    ssum = _dot(hi, ones_ref[...]) + _dot(lo, ones_ref[...])
    y = x * lax.rsqrt(ssum * (1.0 / HEAD_DIM) + EPS) * g_ref[...]
    lane = lax.broadcasted_iota(jnp.int32, y.shape, 1)
    quarter = HEAD_DIM // 4
    partner = jnp.where((lane & quarter) == 0, pltpu.roll(y, LANES - quarter, 1), pltpu.roll(y, quarter, 1))
    o = (y * cos_ref[...] + partner * sin_ref[...]) * scale
    if o_ref.shape[2] == LANES:
        o_ref[0] = o.astype(o_ref.dtype)
    else:
        low = lane < HEAD_DIM
        even = jnp.where(low, o, 0.0)
        odd = jnp.where(low, 0.0, o)
        kv_head = pl.program_id(1) // (WIN_GROUP // 2)
        first = jnp.where(kv_head == 0, even, pltpu.roll(even, HEAD_DIM, 1))
        second = jnp.where(kv_head == 0, pltpu.roll(odd, HEAD_DIM, 1), odd)
        o_ref[0] = jnp.concatenate([first, second], axis=-1).astype(o_ref.dtype)


def _qk_prep(ux, col_off, width, gain, tables, scale, pad_heads=False):
    b, l, _ = ux.shape
    cos_t, sin_t = tables
    c0 = col_off // LANES
    ow = 2 * LANES if pad_heads else LANES
    g = jnp.tile(gain.reshape(1, HEAD_DIM), (1, LANES // HEAD_DIM))
    return pl.pallas_call(
        functools.partial(_qk_prep_body, scale=scale),
        grid=(b, width // LANES),
        in_specs=[pl.BlockSpec((1, l, LANES), lambda bi, j: (bi, 0, c0 + j)),
                  pl.BlockSpec((1, LANES), lambda bi, j: (0, 0)),
                  pl.BlockSpec((l, LANES), lambda bi, j: (0, 0)),
                  pl.BlockSpec((l, LANES), lambda bi, j: (0, 0)),
                  pl.BlockSpec((LANES, LANES), lambda bi, j: (0, 0))],
        out_specs=pl.BlockSpec((1, l, ow), lambda bi, j: (bi, 0, j)),
        out_shape=jax.ShapeDtypeStruct((b, l, width // LANES * ow), BF16),
        compiler_params=_params(("parallel", "parallel")),
        name="qk_prep",
    )(ux, g, cos_t, sin_t, _head_ones())


def _cast_swap_body(x_ref, o_ref, s_ref):
    x = x_ref[0]
    o_ref[0] = x.astype(o_ref.dtype)
    s_ref[0] = pltpu.roll(x, HEAD_DIM, 1).astype(s_ref.dtype)


def _cast_cols(ux, col_off, width, swapped=False):
    b, l, _ = ux.shape
    c0 = col_off // LANES
    blk = pl.BlockSpec((1, l, LANES), lambda bi, j: (bi, 0, j))
    shp = jax.ShapeDtypeStruct((b, l, width), BF16)
    return pl.pallas_call(
        _cast_swap_body if swapped else _cast_body,
        grid=(b, width // LANES),
        in_specs=[pl.BlockSpec((1, l, LANES), lambda bi, j: (bi, 0, c0 + j))],
        out_specs=[blk, blk] if swapped else blk,
        out_shape=[shp, shp] if swapped else shp,
        compiler_params=_params(("parallel", "parallel")),
        name="cast_cols",
    )(ux)


def _nt_dot(a, b):
    return lax.dot_general(a, b, (((1,), (1,)), ((), ())), preferred_element_type=F32)


_WIN_STACK = (0, 2, 1, 3)


def _win_heads(q, sink_ref, h, n):
    heads = [h * WIN_GROUP + g for g in _WIN_STACK]
    qh = jnp.concatenate([q[:, a * LANES:(a + 1) * LANES] for a in heads], axis=0)
    sk = jnp.concatenate([jnp.full((n, 1), sink_ref[a], F32) for a in heads], axis=0)
    return qh, sk


def _win_tiles(o_low, o_high, n):
    low = lax.broadcasted_iota(jnp.int32, (n, LANES), 1) < HEAD_DIM
    return [jnp.where(low, o_low[t * n:(t + 1) * n], o_high[t * n:(t + 1) * n]) for t in range(2)]


WIN_STEP_BLOCKS = 8


def _win_attn_body(sink_ref, q_ref, km_ref, k0_ref, kp_ref, vm_ref, v0_ref, vp_ref, wm_ref, w0_ref, wp_ref,
                   kc_ref, vc_ref, wc_ref, o_ref, *, seq, nsub):
    i = pl.program_id(1)
    kslab = jnp.concatenate([km_ref[0], k0_ref[0], kp_ref[0]], axis=0)
    vslab = jnp.concatenate([vm_ref[0], v0_ref[0], vp_ref[0]], axis=0)
    wslab = jnp.concatenate([wm_ref[0], w0_ref[0], wp_ref[0]], axis=0)
    kc = kc_ref[0]
    qi = lax.broadcasted_iota(jnp.int32, (BLOCK, 3 * BLOCK), 0)
    kj = lax.broadcasted_iota(jnp.int32, (BLOCK, 3 * BLOCK), 1)
    band = jnp.abs(kj - BLOCK - qi) <= WINDOW
    half = WIN_GROUP // 2 * BLOCK
    chains = [(j, h) for j in range(nsub) for h in range(WIN_KV_HEADS)]
    scores = []
    for j, h in chains:
        qh, sk = _win_heads(q_ref[0, j * BLOCK:(j + 1) * BLOCK], sink_ref, h, BLOCK)
        kpos = (i * nsub + j - 1) * BLOCK + kj
        valid = jnp.concatenate([band & (kpos >= 0) & (kpos < seq)] * WIN_GROUP, axis=0)
        s_loc = jnp.where(valid, _nt_dot(qh, kslab[j * BLOCK:(j + 3) * BLOCK]), NEG_INF)
        scores.append((s_loc, _nt_dot(qh, kc), sk))
    maxes = [jnp.maximum(jnp.maximum(jnp.max(s_loc, axis=-1, keepdims=True),
                                     jnp.max(s_ctx, axis=-1, keepdims=True)), sk) for s_loc, s_ctx, sk in scores]
    probs = [(jnp.exp(s_loc - m), jnp.exp(s_ctx - m), jnp.exp(sk - m)) for (s_loc, s_ctx, sk), m in zip(scores, maxes)]
    denoms = [jnp.sum(p_loc, axis=-1, keepdims=True) + jnp.sum(p_ctx, axis=-1, keepdims=True) + p_sink
              for p_loc, p_ctx, p_sink in probs]
    tiles = {}
    for (j, h), (p_loc, p_ctx, _), denom in zip(chains, probs, denoms):
        ws = slice(j * BLOCK, (j + 3) * BLOCK)
        v_nat = (vslab[ws], vc_ref[0])
        v_swp = (wslab[ws], wc_ref[0])
        v_low, v_high = (v_nat, v_swp) if h == 0 else (v_swp, v_nat)
        p_loc = p_loc.astype(BF16)
        p_ctx = p_ctx.astype(BF16)
        o_low = (_dot(p_loc[:half], v_low[0]) + _dot(p_ctx[:half], v_low[1])) / denom[:half]
        o_high = (_dot(p_loc[half:], v_high[0]) + _dot(p_ctx[half:], v_high[1])) / denom[half:]
        tiles[(j, h)] = _win_tiles(o_low, o_high, BLOCK)
    for j in range(nsub):
        row = [t for h in range(WIN_KV_HEADS) for t in tiles[(j, h)]]
        o_ref[0, j * BLOCK:(j + 1) * BLOCK] = jnp.concatenate(row, axis=-1).astype(o_ref.dtype)


def _win_attn(q, k, v, vs, kc, vc, vcs, sink):
    b, s, _ = q.shape
    c = kc.shape[1]
    nb = s // BLOCK
    nsub = math.gcd(nb, WIN_STEP_BLOCKS)
    rows = nsub * BLOCK
    kvw = WIN_KV_HEADS * HEAD_DIM
    cur = lambda bi, i: (bi, i, 0)
    prev = pl.BlockSpec((1, BLOCK, kvw), lambda bi, i: (bi, jnp.maximum(i * nsub - 1, 0), 0))
    main = pl.BlockSpec((1, rows, kvw), cur)
    nxt = pl.BlockSpec((1, BLOCK, kvw), lambda bi, i: (bi, jnp.minimum((i + 1) * nsub, nb - 1), 0))
    cspec = pl.BlockSpec((1, c, kvw), lambda bi, i: (bi, 0, 0))
    return pl.pallas_call(
        functools.partial(_win_attn_body, seq=s, nsub=nsub),
        grid=(b, nb // nsub),
        in_specs=[pl.BlockSpec(memory_space=pltpu.SMEM),
                  pl.BlockSpec((1, rows, WIN_HEADS * LANES), cur),
                  prev, main, nxt, prev, main, nxt, prev, main, nxt,
                  cspec, cspec, cspec],
        out_specs=pl.BlockSpec((1, rows, WIN_HEADS * HEAD_DIM), cur),
        out_shape=jax.ShapeDtypeStruct((b, s, WIN_HEADS * HEAD_DIM), BF16),
        compiler_params=_params(("parallel", "parallel")),
        name="win_attn",
    )(sink.astype(F32), q, k, k, k, v, v, v, vs, vs, vs, kc, vc, vcs)


def _ctx_win_attn_body(sink_ref, q_ref, k_ref, v_ref, w_ref, o_ref):
    q = q_ref[0]
    k = k_ref[0]
    c = q.shape[0]
    half = WIN_GROUP // 2 * c
    tiles = []
    for h in range(WIN_KV_HEADS):
        qh, sk = _win_heads(q, sink_ref, h, c)
        s = _nt_dot(qh, k)
        m = jnp.maximum(jnp.max(s, axis=-1, keepdims=True), sk)
        p = jnp.exp(s - m)
        denom = jnp.sum(p, axis=-1, keepdims=True) + jnp.exp(sk - m)
        p = p.astype(BF16)
        v_low, v_high = (v_ref[0], w_ref[0]) if h == 0 else (w_ref[0], v_ref[0])
        tiles.extend(_win_tiles(_dot(p[:half], v_low) / denom[:half], _dot(p[half:], v_high) / denom[half:], c))
    o_ref[0] = jnp.concatenate(tiles, axis=-1).astype(o_ref.dtype)


def _ctx_win_attn(q, k, v, vs, sink):
    b, c, _ = q.shape
    kvw = WIN_KV_HEADS * HEAD_DIM
    cspec = pl.BlockSpec((1, c, kvw), lambda bi: (bi, 0, 0))
    return pl.pallas_call(
        _ctx_win_attn_body,
        grid=(b,),
        in_specs=[pl.BlockSpec(memory_space=pltpu.SMEM),
                  pl.BlockSpec((1, c, WIN_HEADS * LANES), lambda bi: (bi, 0, 0)),
                  cspec, cspec, cspec],
        out_specs=pl.BlockSpec((1, c, WIN_HEADS * HEAD_DIM), lambda bi: (bi, 0, 0)),
        out_shape=jax.ShapeDtypeStruct((b, c, WIN_HEADS * HEAD_DIM), BF16),
        compiler_params=_params(("parallel",)),
        name="ctx_win_attn",
    )(sink.astype(F32), q, k, v, vs)


DIFF_CHAINS = 2


def _diff_attn_body(lam_ref, q_ref, k_ref, v_ref, g_ref, o_ref, *, lam_init):
    q = q_ref[0]
    k = k_ref[0]
    v = v_ref[0]
    lp = lam_ref[...]
    lam = (jnp.exp(jnp.sum(lp[0:1] * lp[1:2], axis=-1, keepdims=True))
           - jnp.exp(jnp.sum(lp[2:3] * lp[3:4], axis=-1, keepdims=True)) + lam_init)
    lane = lax.broadcasted_iota(jnp.int32, q.shape, 1)
    zero = jnp.zeros_like(q)
    q1 = jnp.where(lane < HEAD_DIM, q, zero)
    q2 = jnp.where(lane < HEAD_DIM, zero, q)
    rows = q.shape[0] // DIFF_CHAINS
    for t in range(DIFF_CHAINS):
        rs = slice(t * rows, (t + 1) * rows)
        s1 = _nt_dot(q1[rs], k)
        s2 = _nt_dot(q2[rs], k)
        p1 = jnp.exp2(s1 - jnp.max(s1, axis=-1, keepdims=True))
        p2 = jnp.exp2(s2 - jnp.max(s2, axis=-1, keepdims=True))
        l1 = jnp.sum(p1, axis=-1, keepdims=True)
        r21 = lam * l1 / jnp.sum(p2, axis=-1, keepdims=True)
        a = (p1 - p2 * r21).astype(BF16)
        o = _dot(a, v) / l1
        ms = jnp.mean(o * o, axis=-1, keepdims=True)
        o_ref[0, rs] = (o * lax.rsqrt(ms + EPS) * g_ref[...] * (1.0 - lam_init)).astype(o_ref.dtype)


def _diff_attn(q, k, v, lam_params, subln, lam_init, tq):
    b, nq, w = q.shape
    nk = k.shape[1]
    tq = min(tq, nq)
    hw = 2 * HEAD_DIM
    return pl.pallas_call(
        functools.partial(_diff_attn_body, lam_init=lam_init),
        grid=(b, DIFF_HEADS, nq // tq),
        in_specs=[pl.BlockSpec((4, HEAD_DIM), lambda bi, h, t: (0, 0)),
                  pl.BlockSpec((1, tq, hw), lambda bi, h, t: (bi, t, h)),
                  pl.BlockSpec((1, nk, hw), lambda bi, h, t: (bi, 0, h)),
                  pl.BlockSpec((1, nk, hw), lambda bi, h, t: (bi, 0, h)),
                  pl.BlockSpec((1, hw), lambda bi, h, t: (0, 0))],
        out_specs=pl.BlockSpec((1, tq, hw), lambda bi, h, t: (bi, t, h)),
        out_shape=jax.ShapeDtypeStruct((b, nq, w), BF16),
        compiler_params=_params(("parallel", "parallel", "parallel"), 56),
        name="diff_attn",
    )(lam_params, q, k, v, subln.reshape(1, hw))


def _hy_pre_body(x0_ref, x1_ref, v_ref, w0_ref, w1_ref, wv_ref, b0_ref, b1_ref, bv_ref, z_ref, x0c_ref):
    n = x0_ref.shape[1]
    row = lax.broadcasted_iota(jnp.int32, (n, 1), 0)

    def sconv(u_ref, w_ref, b_ref):
        u = u_ref[0]
        w = w_ref[...]
        up = jnp.where(row == 0, 0.0, pltpu.roll(u, 1, 0))
        un = jnp.where(row == n - 1, 0.0, pltpu.roll(u, n - 1, 0))
        return ((b_ref[...] + up * w[0:1]) + u * w[1:2]) + un * w[2:3]

    x0c_ref[0] = sconv(x0_ref, w0_ref, b0_ref)
    z_ref[0] = sconv(v_ref, wv_ref, bv_ref) * sconv(x1_ref, w1_ref, b1_ref)


def _hy_pre(ux, conv_w, conv_b):
    b, l, _ = ux.shape
    nc = HY_WIDTH // LANES
    c0 = OFF_HY // LANES
    ub = lambda seg: pl.BlockSpec((1, l, LANES), lambda bi, j: (bi, 0, c0 + seg * nc + j))
    wb = lambda seg: pl.BlockSpec((3, LANES), lambda bi, j: (0, seg * nc + j))
    bb = lambda seg: pl.BlockSpec((1, LANES), lambda bi, j: (0, seg * nc + j))
    out = pl.BlockSpec((1, l, LANES), lambda bi, j: (bi, 0, j))
    shp = jax.ShapeDtypeStruct((b, l, HY_WIDTH), F32)
    cb = conv_b.reshape(1, 3 * HY_WIDTH)
    return pl.pallas_call(
        _hy_pre_body,
        grid=(b, nc),
        in_specs=[ub(0), ub(1), ub(2), wb(0), wb(1), wb(2), bb(0), bb(1), bb(2)],
        out_specs=[out, out],
        out_shape=[shp, shp],
        compiler_params=_params(("parallel", "parallel")),
        name="hy_pre",
    )(ux, ux, ux, conv_w, conv_w, conv_w, cb, cb, cb)


def _hy_features(l):
    n = jnp.arange(l, dtype=F32)
    pos = jnp.concatenate([n, l - n])
    t = pos / max(l - 1, 1)
    w = 2.0 * math.pi * pos / l
    f = jnp.linspace(1e-4, HY_BANDS - 1, HY_BANDS, dtype=F32)
    flag = jnp.ones((2 * l, 1), F32).at[l].set(0.0)
    feat = jnp.concatenate([t[:, None], jnp.cos(w[:, None] * f), -jnp.sin(w[:, None] * f), flag], axis=-1)
    return jnp.pad(feat, ((0, 0), (0, LANES - HY_EMB - 1)))


def _hy_filter_body(f_ref, w1_ref, b1_ref, w2_ref, b2_ref, w3_ref, fr_ref, dec_ref, k_ref, sum_ref):
    first = (pl.program_id(0) == 0) & (pl.program_id(1) == 0)

    @pl.when(first)
    def _():
        sum_ref[...] = jnp.zeros_like(sum_ref)

    f = f_ref[...]
    fr = fr_ref[...]
    h = jnp.sin(fr[0:1] * (_dot3(f, w1_ref[...]) + b1_ref[...]))
    h = jnp.sin(fr[1:2] * (_dot3(h, w2_ref[...]) + b2_ref[...]))
    h = _dot3(h, w3_ref[...]) * jnp.exp(-f[:, 0:1] * jnp.abs(dec_ref[...]))
    h = h * f[:, HY_EMB:HY_EMB + 1]
    k_ref[...] = h
    sum_ref[...] += jnp.sum(jnp.abs(h), axis=0, keepdims=True)


def _hy_filter(l, p):
    tl = min(l, 1024)
    nt = l // tl
    order = p['hy_w1'].shape[1]
    w1 = jnp.zeros((LANES, order), F32).at[:HY_EMB].set(p['hy_w1'])
    const = lambda d, i: (0, 0)
    return pl.pallas_call(
        _hy_filter_body,
        grid=(2, nt),
        in_specs=[pl.BlockSpec((tl, LANES), lambda d, i: (d * nt + i, 0)),
                  pl.BlockSpec((LANES, order), const),
                  pl.BlockSpec((1, order), const),
                  pl.BlockSpec((order, order), const),
                  pl.BlockSpec((1, order), const),
                  pl.BlockSpec((order, HY_WIDTH), lambda d, i: (0, d)),
                  pl.BlockSpec((2, order), const),
                  pl.BlockSpec((1, HY_WIDTH), lambda d, i: (0, d))],
        out_specs=[pl.BlockSpec((tl, HY_WIDTH), lambda d, i: (d * nt + i, 0)),
                   pl.BlockSpec((1, HY_WIDTH), const)],
        out_shape=[jax.ShapeDtypeStruct((2 * l, HY_WIDTH), F32), jax.ShapeDtypeStruct((1, HY_WIDTH), F32)],
        compiler_params=_params(("arbitrary", "arbitrary")),
        name="hy_filter",
    )(_hy_features(l), w1, p['hy_b1'].reshape(1, order), p['hy_w2'], p['hy_b2'].reshape(1, order),
      p['hy_w3'], p['hy_freq'], p['hy_decay'].reshape(1, 2 * HY_WIDTH))


def _fft_consts(n_la):
    ka = np.arange(FFT_NA)[:, None]
    la = np.arange(n_la)[None, :]
    ang = 2.0 * math.pi * ka * la / FFT_NA
    f1 = np.concatenate([np.cos(ang), -np.sin(ang)], axis=0)
    lb = np.arange(FFT_NB)[None, :]
    th = 2.0 * math.pi * ka * lb / FFT_N
    tc, ts = np.cos(th), np.sin(th)
    a2 = 2.0 * math.pi * np.arange(FFT_NB)[:, None] * np.arange(FFT_NB)[None, :] / FFT_NB
    gre, gim = np.cos(a2), -np.sin(a2)
    g2 = np.block([[gre, gim], [-gim, gre]])
    g2i = np.block([[gre, -gim], [gim, gre]])
    lo = np.arange(FFT_NA // 2)[:, None]
    ph = 2.0 * math.pi * lo * np.arange(FFT_NA)[None, :] / FFT_NA
    f1i = np.concatenate([np.cos(ph), -np.sin(ph)], axis=1) / FFT_N
    return f1, tc, ts, g2, g2i, f1i


def _fft_fwd(x_ref, f1_ref, tc_ref, ts_ref, g2_ref, p2_ref, cc, dot):
    rhs = jnp.concatenate([x_ref[0, c] for c in range(cc)], axis=1)
    a = dot(f1_ref[...].astype(p2_ref.dtype), rhs.astype(p2_ref.dtype))
    tc = tc_ref[...]
    ts = ts_ref[...]
    for c in range(cc):
        are = a[0:FFT_NA, c * FFT_NB:(c + 1) * FFT_NB]
        aim = a[FFT_NA:2 * FFT_NA, c * FFT_NB:(c + 1) * FFT_NB]
        p2_ref[c * FFT_NA:(c + 1) * FFT_NA, 0:FFT_NB] = (are * tc + aim * ts).astype(p2_ref.dtype)
        p2_ref[c * FFT_NA:(c + 1) * FFT_NA, FFT_NB:2 * FFT_NB] = (aim * tc - are * ts).astype(p2_ref.dtype)
    return dot(p2_ref[...], g2_ref[...].astype(p2_ref.dtype))


def _fft_spec_body(x_ref, f1_ref, tc_ref, ts_ref, g2_ref, o_ref, p2_ref, *, cc):
    o_ref[...] = _fft_fwd(x_ref, f1_ref, tc_ref, ts_ref, g2_ref, p2_ref, cc, _dot3)


def _fft_conv_body(x_ref, kf_ref, f1_ref, tc_ref, ts_ref, g2_ref, g2i_ref, f1i_ref, o_ref, p2_ref, r2_ref, *, cc):
    z = _fft_fwd(x_ref, f1_ref, tc_ref, ts_ref, g2_ref, p2_ref, cc, _dot)
    kf = kf_ref[...]
    zre, zim = z[:, :FFT_NB], z[:, FFT_NB:]
    kre, kim = kf[:, :FFT_NB], kf[:, FFT_NB:]
    y = jnp.concatenate([zre * kre - zim * kim, zre * kim + zim * kre], axis=1).astype(BF16)
    u = _dot(y, g2i_ref[...].astype(BF16))
    tc = tc_ref[...]
    ts = ts_ref[...]
    for c in range(cc):
        ure = u[c * FFT_NA:(c + 1) * FFT_NA, 0:FFT_NB]
        uim = u[c * FFT_NA:(c + 1) * FFT_NA, FFT_NB:2 * FFT_NB]
        r2_ref[0:FFT_NA, c * FFT_NB:(c + 1) * FFT_NB] = (ure * tc - uim * ts).astype(BF16)
        r2_ref[FFT_NA:2 * FFT_NA, c * FFT_NB:(c + 1) * FFT_NB] = (ure * ts + uim * tc).astype(BF16)
    out = _dot(f1i_ref[...].astype(BF16), r2_ref[...])
    for c in range(cc):
        o_ref[0, c] = out[:, c * FFT_NB:(c + 1) * FFT_NB]


FFT_CC = 32


def _fft_spectrum(kt):
    nch, n_la, _ = kt.shape
    cc = FFT_CC
    f1, tc, ts, g2, _, _ = _fft_consts(n_la)
    const = lambda j: (0, 0)
    return pl.pallas_call(
        functools.partial(_fft_spec_body, cc=cc),
        grid=(nch // cc,),
        in_specs=[pl.BlockSpec((1, cc, n_la, FFT_NB), lambda j: (0, j, 0, 0)),
                  pl.BlockSpec((2 * FFT_NA, n_la), const),
                  pl.BlockSpec((FFT_NA, FFT_NB), const),
                  pl.BlockSpec((FFT_NA, FFT_NB), const),
                  pl.BlockSpec((2 * FFT_NB, 2 * FFT_NB), const)],
        out_specs=pl.BlockSpec((cc * FFT_NA, 2 * FFT_NB), lambda j: (j, 0)),
        out_shape=jax.ShapeDtypeStruct((nch * FFT_NA, 2 * FFT_NB), F32),
        scratch_shapes=[pltpu.VMEM((cc * FFT_NA, 2 * FFT_NB), F32)],
        compiler_params=_params(("parallel",)),
        name="fft_spectrum",
    )(kt[None], jnp.asarray(f1, F32), jnp.asarray(tc, F32), jnp.asarray(ts, F32), jnp.asarray(g2, F32))


def _fft_conv(zt, kf):
    b, nch, n_la, _ = zt.shape
    cc = FFT_CC
    f1, tc, ts, g2, g2i, f1i = _fft_consts(n_la)
    const = lambda j, bi: (0, 0)
    return pl.pallas_call(
        functools.partial(_fft_conv_body, cc=cc),
        grid=(nch // cc, b),
        in_specs=[pl.BlockSpec((1, cc, n_la, FFT_NB), lambda j, bi: (bi, j, 0, 0)),
                  pl.BlockSpec((cc * FFT_NA, 2 * FFT_NB), lambda j, bi: (j, 0)),
                  pl.BlockSpec((2 * FFT_NA, n_la), const),
                  pl.BlockSpec((FFT_NA, FFT_NB), const),
                  pl.BlockSpec((FFT_NA, FFT_NB), const),
                  pl.BlockSpec((2 * FFT_NB, 2 * FFT_NB), const),
                  pl.BlockSpec((2 * FFT_NB, 2 * FFT_NB), const),
                  pl.BlockSpec((FFT_NA // 2, 2 * FFT_NA), const)],
        out_specs=pl.BlockSpec((1, cc, n_la, FFT_NB), lambda j, bi: (bi, j, 0, 0)),
        out_shape=jax.ShapeDtypeStruct((b, nch, n_la, FFT_NB), F32),
        scratch_shapes=[pltpu.VMEM((cc * FFT_NA, 2 * FFT_NB), BF16),
                        pltpu.VMEM((2 * FFT_NA, cc * FFT_NB), BF16)],
        compiler_params=_params(("parallel", "parallel")),
        name="fft_conv",
    )(zt, kf, jnp.asarray(f1, F32), jnp.asarray(tc, F32), jnp.asarray(ts, F32),
      jnp.asarray(g2, F32), jnp.asarray(g2i, F32), jnp.asarray(f1i, F32))


def _dft_conv_body(z_ref, k_ref, fw_ref, fi_ref, o_ref):
    l = z_ref.shape[1]
    fw = fw_ref[...]
    n = fw.shape[1]
    kf = _dot3(fw, k_ref[...])
    zf = _dot3(fw[:, :l], z_ref[0])
    kre, kim = kf[:n], kf[n:]
    zre, zim = zf[:n], zf[n:]
    y = jnp.concatenate([zre * kre - zim * kim, zre * kim + zim * kre], axis=0)
    o_ref[0] = _dot3(fi_ref[...], y)


def _dft_conv(z, k):
    b, l, w = z.shape
    n = 2 * l
    ang = 2.0 * math.pi * np.arange(n)[:, None] * np.arange(n)[None, :] / n
    fw = np.concatenate([np.cos(ang), -np.sin(ang)], axis=0)
    ai = ang[:l]
    fi = np.concatenate([np.cos(ai), -np.sin(ai)], axis=1) / n
    return pl.pallas_call(
        _dft_conv_body,
        grid=(b,),
        in_specs=[pl.BlockSpec((1, l, w), lambda bi: (bi, 0, 0)),
                  pl.BlockSpec((n, w), lambda bi: (0, 0)),
                  pl.BlockSpec((2 * n, n), lambda bi: (0, 0)),
                  pl.BlockSpec((l, 2 * n), lambda bi: (0, 0))],
        out_specs=pl.BlockSpec((1, l, w), lambda bi: (bi, 0, 0)),
        out_shape=jax.ShapeDtypeStruct((b, l, w), F32),
        compiler_params=_params(("parallel",)),
        name="dft_conv",
    )(z, k, jnp.asarray(fw, F32), jnp.asarray(fi, F32))


def _hy_post_body(y_ref, z_ref, x0_ref, ks_ref, sk_ref, o_ref):
    y = y_ref[0] / (ks_ref[...] + EPS) + sk_ref[...] * z_ref[0]
    o_ref[0] = (y * x0_ref[0]).astype(o_ref.dtype)


def _hy_post(yraw, z, x0c, ksum, skip):
    b, l, w = z.shape
    ts = min(l, 1024)
    blk = pl.BlockSpec((1, ts, w), lambda bi, i: (bi, i, 0))
    vec = pl.BlockSpec((1, w), lambda bi, i: (0, 0))
    return pl.pallas_call(
        _hy_post_body,
        grid=(b, l // ts),
        in_specs=[blk, blk, blk, vec, vec],
        out_specs=blk,
        out_shape=jax.ShapeDtypeStruct((b, l, w), BF16),
        compiler_params=_params(("parallel", "parallel")),
        name="hy_post",
    )(yraw, z, x0c, ksum, skip.reshape(1, w))


def _hyena(ux, p):
    b, l, _ = ux.shape
    z, x0c = _hy_pre(ux, p['hy_conv_w'], p['hy_conv_b'])
    k_raw, ksum = _hy_filter(l, p)
    if 2 * l == FFT_N:
        kt = k_raw.T.reshape(HY_WIDTH, FFT_NA, FFT_NB)
        kf = _fft_spectrum(kt)
        zt = jnp.swapaxes(z, 1, 2).reshape(b, HY_WIDTH, l // FFT_NB, FFT_NB)
        yt = _fft_conv(zt, kf)
        yraw = jnp.swapaxes(yt.reshape(b, HY_WIDTH, l), 1, 2)
    else:
        yraw = _dft_conv(z, k_raw)
    return _hy_post(yraw, z, x0c, ksum, p['hy_skip'])


def _pool_body(u_ref, w_ref, sc_ref, o_ref, *, win):
    u = u_ref[0]
    n = u.shape[0]
    lo = win // 2
    hi = win - 1 - lo
    row = lax.broadcasted_iota(jnp.int32, (n, 1), 0)
    acc = u
    for j in range(1, lo + 1):
        acc = acc + jnp.where(row >= j, pltpu.roll(u, j, 0), 0.0)
    for j in range(1, hi + 1):
        acc = acc + jnp.where(row < n - j, pltpu.roll(u, n - j, 0), 0.0)
    cnt = (jnp.minimum(row + hi + 1, n) - jnp.maximum(row - lo, 0)).astype(F32)
    d = acc / cnt - u
    o_ref[0] = (_dot(d.astype(BF16), w_ref[0].astype(BF16)) * sc_ref[...]).astype(o_ref.dtype)


def _pool_mix(ux, w_grp, scale):
    b, l, _ = ux.shape
    c0 = OFF_PL // LANES
    outs = []
    for gi, win in enumerate(POOL_WINDOWS):
        outs.append(pl.pallas_call(
            functools.partial(_pool_body, win=win),
            grid=(b,),
            in_specs=[pl.BlockSpec((1, l, LANES), lambda bi, gi=gi: (bi, 0, c0 + gi)),
                      pl.BlockSpec((1, LANES, LANES), lambda bi, gi=gi: (gi, 0, 0)),
                      pl.BlockSpec((1, LANES), lambda bi, gi=gi: (0, gi))],
            out_specs=pl.BlockSpec((1, l, LANES), lambda bi: (bi, 0, 0)),
            out_shape=jax.ShapeDtypeStruct((b, l, LANES), BF16),
            compiler_params=_params(("parallel",)),
            name=f"pool_mix{win}",
        )(ux, w_grp, scale.reshape(1, -1)))
    return jnp.concatenate(outs, axis=-1)


def _merge1_body(hx_ref, y0, y1, y2, y3, g0, g1, g2, g3, b0, b1, b2, b3, o_ref):
    hx = hx_ref[...]
    acc = None
    for y_ref, g_ref, b_ref in ((y0, g0, b0), (y1, g1, b1), (y2, g2, b2), (y3, g3, b3)):
        t = jax.nn.sigmoid(_dot(hx, g_ref[...])) * _dot(y_ref[...], b_ref[0, 0])
        acc = t if acc is None else acc + t
    o_ref[...] = acc.astype(o_ref.dtype)


def _merge1(hx, ys, w_in, gate_off, w_branch, layer, tm=1024, tn=256):
    m, d = hx.shape
    tm = min(tm, m)
    nj = d // tn
    g0 = gate_off // tn
    row = lambda i, j: (i, 0)
    gspec = lambda br: pl.BlockSpec((d, tn), lambda i, j, br=br: (0, g0 + br * nj + j))
    bspec = lambda br: pl.BlockSpec((1, 1, BRANCH_W, tn), lambda i, j, br=br: (layer, br, 0, j))
    return pl.pallas_call(
        _merge1_body,
        grid=(m // tm, nj),
        in_specs=([pl.BlockSpec((tm, d), row)] + [pl.BlockSpec((tm, BRANCH_W), row)] * N_BRANCH
                  + [gspec(br) for br in range(N_BRANCH)] + [bspec(br) for br in range(N_BRANCH)]),
        out_specs=pl.BlockSpec((tm, tn), lambda i, j: (i, j)),
        out_shape=jax.ShapeDtypeStruct((m, d), BF16),
        compiler_params=_params(("parallel", "parallel")),
        name="merge_gate",
    )(hx, *ys, w_in, w_in, w_in, w_in, w_branch, w_branch, w_branch, w_branch)


def _merge2_body(a_ref, w_ref, x_ref, g_ref, o_ref):
    o_ref[0] = x_ref[0] + g_ref[0] * _dot(a_ref[0], w_ref[0])


def _merge2(acc, w_out, layer, x, gate, tm=512):
    b, l, d = x.shape
    tm = min(tm, l)
    blk = pl.BlockSpec((1, tm, d), lambda bi, i: (bi, i, 0))
    return pl.pallas_call(
        _merge2_body,
        grid=(b, l // tm),
        in_specs=[blk, pl.BlockSpec((1, d, d), lambda bi, i: (layer, 0, 0)), blk,
                  pl.BlockSpec((1, 1, d), lambda bi, i: (bi, 0, 0))],
        out_specs=blk,
        out_shape=jax.ShapeDtypeStruct((b, l, d), F32),
        compiler_params=_params(("parallel", "parallel")),
        name="merge_out",
    )(acc, w_out, x, gate)


def _store_token_major(ref, x):
    n, d = x.shape
    pitch = d // LANES
    for j in range(pitch):
        ref[pl.ds(j, n, stride=pitch), :] = x[:, j * LANES:(j + 1) * LANES]


def _load_token_major(ref, n):
    pitch = ref.shape[0] // n
    return jnp.concatenate([ref[pl.ds(j, n, stride=pitch), :] for j in range(pitch)], axis=1)


def _route_body(x_ref, g_ref, sc_ref, sh_ref, wr_ref, br_ref, h_ref, e_ref, w_ref):
    x = x_ref[0]
    ms = jnp.mean(x * x, axis=-1, keepdims=True)
    h = (x * lax.rsqrt(ms + EPS) * g_ref[...]) * (1.0 + sc_ref[0]) + sh_ref[0]
    _store_token_major(h_ref, h)
    logits = _dot3(h, wr_ref[...]) + br_ref[...]
    lane = lax.broadcasted_iota(jnp.int32, logits.shape, 1)
    big = jnp.int32(LANES)
    gl = jnp.where(lane < N_GROUPS, logits, -jnp.inf)
    ge = jnp.exp(gl - jnp.max(gl, axis=-1, keepdims=True))
    pg = ge / jnp.sum(ge, axis=-1, keepdims=True)
    pg_top = jnp.max(pg, axis=-1, keepdims=True)
    g_idx = jnp.min(jnp.where((pg == pg_top) & (lane < N_GROUPS), lane, big), axis=-1, keepdims=True)
    first = N_GROUPS + EXPERTS_PER_GROUP * g_idx
    le = jnp.where((lane >= first) & (lane < first + EXPERTS_PER_GROUP), logits, -jnp.inf)
    v1 = jnp.max(le, axis=-1, keepdims=True)
    i1 = jnp.min(jnp.where(le == v1, lane, big), axis=-1, keepdims=True)
    le2 = jnp.where(lane == i1, -jnp.inf, le)
    v2 = jnp.max(le2, axis=-1, keepdims=True)
    i2 = jnp.min(jnp.where(le2 == v2, lane, big), axis=-1, keepdims=True)
    e2 = jnp.exp(v2 - v1)
    den = 1.0 + e2
    e_ref[0] = jnp.where(lane == 0, i1 - N_GROUPS, jnp.where(lane == 1, i2 - N_GROUPS, 0))
    w_ref[0] = jnp.where(lane == 0, (1.0 / den) * pg_top, jnp.where(lane == 1, (e2 / den) * pg_top, 0.0))


def _route(x, gain, scale, shift, w_router, b_router, tm=512):
    b, l, d = x.shape
    tm = min(tm, l)
    blk = pl.BlockSpec((1, tm, d), lambda bi, i: (bi, i, 0))
    mod = pl.BlockSpec((1, 1, d), lambda bi, i: (bi, 0, 0))
    sm = pl.BlockSpec((1, tm, LANES), lambda bi, i: (bi, i, 0))
    return pl.pallas_call(
        _route_body,
        grid=(b, l // tm),
        in_specs=[blk, pl.BlockSpec((1, d), lambda bi, i: (0, 0)), mod, mod,
                  pl.BlockSpec((d, LANES), lambda bi, i: (0, 0)),
                  pl.BlockSpec((1, LANES), lambda bi, i: (0, 0))],
        out_specs=[pl.BlockSpec((tm * (d // LANES), LANES), lambda bi, i: (bi * (l // tm) + i, 0)), sm, sm],
        out_shape=[jax.ShapeDtypeStruct((b * l * (d // LANES), LANES), F32),
                   jax.ShapeDtypeStruct((b, l, LANES), jnp.int32),
                   jax.ShapeDtypeStruct((b, l, LANES), F32)],
        compiler_params=_params(("parallel", "parallel")),
        name="route",
    )(x, gain.reshape(1, d), scale, shift, w_router, b_router)


def _row_scatter_body(idx_ref, x_ref, dst, buf, sem, *, pitch):
    i = pl.program_id(0)
    n = pl.num_programs(0)
    tm = idx_ref.shape[2]
    slot = i % 2

    def wait(s):
        pltpu.make_async_copy(buf.at[s], dst.at[pl.ds(0, tm * pitch)], sem.at[s]).wait()

    @pl.when(i >= 2)
    def _():
        wait(slot)

    buf[slot] = x_ref[...]
    for r in range(tm):
        pltpu.make_async_copy(buf.at[slot, pl.ds(r * pitch, pitch)], dst.at[pl.ds(idx_ref[0, 0, r], pitch)],
                              sem.at[slot]).start(priority=r % 2)

    @pl.when(i == n - 1)
    def _():
        wait(slot)

        @pl.when(n >= 2)
        def _():
            wait(1 - slot)


def _row_scatter(src, idx, pitch):
    nblk, tm = idx.shape
    rows = tm * pitch
    return pl.pallas_call(
        functools.partial(_row_scatter_body, pitch=pitch),
        grid=(nblk,),
        in_specs=[pl.BlockSpec((1, 1, tm), lambda i: (i, 0, 0), memory_space=pltpu.SMEM),
                  pl.BlockSpec((rows, LANES), lambda i: (i, 0))],
        out_specs=pl.BlockSpec(memory_space=pl.ANY),
        out_shape=jax.ShapeDtypeStruct(src.shape, src.dtype),
        scratch_shapes=[pltpu.VMEM((2, rows, LANES), src.dtype), pltpu.SemaphoreType.DMA((2,))],
        compiler_params=_params(("arbitrary",)),
        name="row_scatter",
    )(idx.reshape(nblk, 1, tm), src)


MOE_GATHER_CHUNK = 32


def _ffn_body(blk_e_ref, nused_ref, nval_ref, gi_ref, gn_ref, sw_ref, w1_ref, w3_ref, w2_ref, h_hbm, o_ref, xbuf, sem,
              *, pitch):
    i = pl.program_id(0)
    n = pl.num_programs(0)
    tm = sw_ref.shape[0]
    slot = i % 2
    chunk = MOE_GATHER_CHUNK

    def gather(idx_ref, s, nv):
        for c in range(tm // chunk):
            @pl.when(c * chunk < nv)
            def _():
                for r in range(c * chunk, (c + 1) * chunk):
                    pltpu.make_async_copy(h_hbm.at[pl.ds(idx_ref[0, 0, r], pitch)],
                                          xbuf.at[s, pl.ds(r * pitch, pitch)], sem.at[s]).start(priority=r % 2)

    @pl.when(i == 0)
    def _():
        xbuf[...] = jnp.zeros_like(xbuf)
        gather(gi_ref, 0, nval_ref[0])

    @pl.when(i + 1 < n)
    def _():
        gather(gn_ref, 1 - slot, nval_ref[jnp.minimum(i + 1, n - 1)])

    for c in range(tm // chunk):
        @pl.when(c * chunk < nval_ref[i])
        def _():
            pltpu.make_async_copy(h_hbm.at[pl.ds(0, chunk * pitch)],
                                  xbuf.at[slot, pl.ds(c * chunk * pitch, chunk * pitch)], sem.at[slot]).wait()

    @pl.when(i < nused_ref[0])
    def _():
        x = _load_token_major(xbuf.at[slot], tm).astype(BF16)
        a = _dot(x, w1_ref[0, 0])
        h = ((a * jax.nn.sigmoid(a)) * _dot(x, w3_ref[0, 0])).astype(BF16)
        _store_token_major(o_ref, _dot(h, w2_ref[0, 0]) * sw_ref[...])

    @pl.when(i >= nused_ref[0])
    def _():
        o_ref[...] = jnp.zeros_like(o_ref)


def _moe_experts(h2t, blk_e, nused, nval, gidx, sidx, slot_w, w1, w3, w2, layer):
    nblk, tm = gidx.shape
    d = w1.shape[2]
    ff = w1.shape[3]
    pitch = d // LANES
    rows = tm * pitch
    smem_blk = lambda f: pl.BlockSpec((1, 1, tm), f, memory_space=pltpu.SMEM)
    grid_spec = pltpu.PrefetchScalarGridSpec(
        num_scalar_prefetch=3,
        grid=(nblk,),
        in_specs=[smem_blk(lambda i, be, nu, nv: (i, 0, 0)),
                  smem_blk(lambda i, be, nu, nv: (jnp.minimum(i + 1, nblk - 1), 0, 0)),
                  pl.BlockSpec((tm, 1), lambda i, be, nu, nv: (i, 0)),
                  pl.BlockSpec((1, 1, d, ff), lambda i, be, nu, nv: (layer, be[i], 0, 0)),
                  pl.BlockSpec((1, 1, d, ff), lambda i, be, nu, nv: (layer, be[i], 0, 0)),
                  pl.BlockSpec((1, 1, ff, d), lambda i, be, nu, nv: (layer, be[i], 0, 0)),
                  pl.BlockSpec(memory_space=pl.ANY)],
        out_specs=pl.BlockSpec((rows, LANES), lambda i, be, nu, nv: (i, 0)),
        scratch_shapes=[pltpu.VMEM((2, rows, LANES), F32), pltpu.SemaphoreType.DMA((2,))],
    )
    gidx3 = (gidx * pitch).reshape(nblk, 1, tm)
    yb = pl.pallas_call(
        functools.partial(_ffn_body, pitch=pitch),
        grid_spec=grid_spec,
        out_shape=jax.ShapeDtypeStruct((nblk * rows, LANES), F32),
        compiler_params=_params(("arbitrary",), 56),
        name="moe_ffn",
    )(blk_e, nused, nval, gidx3, gidx3, slot_w, w1, w3, w2, h2t)
    return _row_scatter(yb, sidx * pitch, pitch)


def _moe_plan(e_idx, e_w, tm):
    n = e_idx.shape[0]
    m = n * TOP_K
    e_flat = e_idx.reshape(m)
    w_flat = e_w.reshape(m)
    order = jnp.argsort(e_flat).astype(jnp.int32)
    experts = jnp.arange(N_EXPERTS, dtype=jnp.int32)
    counts = jnp.sum((e_flat[:, None] == experts[None, :]).astype(jnp.int32), axis=0)
    padded = (counts + tm - 1) // tm * tm
    pad_end = jnp.cumsum(padded)
    pad_start = pad_end - padded
    cnt_start = jnp.cumsum(counts) - counts
    nblk = -(-(m + N_EXPERTS * (tm - 1)) // tm)
    blk_first = jnp.arange(nblk, dtype=jnp.int32) * tm
    blk_e = jnp.minimum(jnp.sum((pad_end[None, :] <= blk_first[:, None]).astype(jnp.int32), axis=1), N_EXPERTS - 1)
    rank = blk_first[:, None] + jnp.arange(tm, dtype=jnp.int32)[None, :] - pad_start[blk_e][:, None]
    valid = rank < counts[blk_e][:, None]
    src = order[jnp.clip(cnt_start[blk_e][:, None] + rank, 0, m - 1)]
    tok = jnp.where(valid, src // TOP_K, 0)
    slot_w = jnp.where(valid, w_flat[src], 0.0)
    pad_rank = (jnp.cumsum(jnp.logical_not(valid).reshape(-1).astype(jnp.int32)) - 1).reshape(nblk, tm)
    dst = jnp.where(valid, (src % TOP_K) * n + src // TOP_K, m + pad_rank)
    nused = (pad_end[-1] // tm).astype(jnp.int32).reshape(1)
    nval = jnp.sum(valid.astype(jnp.int32), axis=1)
    return (blk_e, nused, nval, tok.astype(jnp.int32), dst.astype(jnp.int32),
            slot_w.reshape(nblk * tm, 1).astype(F32))


def _combine_body(x_ref, y0_ref, y1_ref, g_ref, o_ref):
    tm = x_ref.shape[1]
    o_ref[0] = x_ref[0] + g_ref[0] * (_load_token_major(y0_ref, tm) + _load_token_major(y1_ref, tm))


def _moe_combine(x, yk, gate, row_off, n_tok, tm=256):
    b, l, d = x.shape
    tm = min(tm, l)
    assert l % tm == 0 and row_off % tm == 0 and n_tok % tm == 0
    nl = l // tm
    rows = tm * (d // LANES)
    blk = pl.BlockSpec((1, tm, d), lambda bi, i: (bi, i, 0))
    o0 = row_off // tm
    o1 = (n_tok + row_off) // tm
    return pl.pallas_call(
        _combine_body,
        grid=(b, nl),
        in_specs=[blk,
                  pl.BlockSpec((rows, LANES), lambda bi, i: (o0 + bi * nl + i, 0)),
                  pl.BlockSpec((rows, LANES), lambda bi, i: (o1 + bi * nl + i, 0)),
                  pl.BlockSpec((1, 1, d), lambda bi, i: (bi, 0, 0))],
        out_specs=blk,
        out_shape=jax.ShapeDtypeStruct((b, l, d), F32),
        compiler_params=_params(("parallel", "parallel")),
        name="moe_combine",
    )(x, yk, yk, gate)


def _mixers(ux, ucx, p, lam_init, latent):
    b, l, _ = ux.shape
    rope_q = _rope_tables(l, latent)
    no_rope = _rope_tables(ucx.shape[1], False)
    scale = HEAD_DIM ** -0.5
    qw = _qk_prep(ux, OFF_WQ, 512, p['win_qnorm'], rope_q, scale, pad_heads=True)
    qd = _qk_prep(ux, OFF_DQ, 512, p['diff_qnorm'], rope_q, scale * math.log2(math.e))
    kcw = _qk_prep(ucx, OFF_WK, 128, p['win_knorm'], no_rope, 1.0)
    vcw, vcws = _cast_cols(ucx, OFF_WV, 128, swapped=True)
    kcd = _qk_prep(ucx, OFF_DK, 512, p['diff_knorm'], no_rope, 1.0)
    vcd = _cast_cols(ucx, OFF_DV, 512)
    if latent:
        kw = _qk_prep(ux, OFF_WK, 128, p['win_knorm'], rope_q, 1.0)
        vw, vws = _cast_cols(ux, OFF_WV, 128, swapped=True)
        kd = _qk_prep(ux, OFF_DK, 512, p['diff_knorm'], rope_q, 1.0)
        vd = _cast_cols(ux, OFF_DV, 512)
        y_win = _win_attn(qw, kw, vw, vws, kcw, vcw, vcws, p['win_sink'])
        k_all = jnp.concatenate([kd, kcd], axis=1)
        v_all = jnp.concatenate([vd, vcd], axis=1)
    else:
        y_win = _ctx_win_attn(qw, kcw, vcw, vcws, p['win_sink'])
        k_all, v_all = kcd, vcd
    y_diff = _diff_attn(qd, k_all, v_all, p['diff_lambda'], p['diff_subln'], lam_init, 512)
    y_hy = _hyena(ux, p)
    y_pool = _pool_mix(ux, p['pool_proj'], p['pool_scale'])
    return y_win, y_diff, y_hy, y_pool


def _layer(x, ctx, mod, p, layer_idx, update_ctx):
    b, s, d = x.shape
    c = ctx.shape[1]
    lam_init = 0.8 - 0.6 * math.exp(-0.3 * layer_idx)
    chunks = [mod[:, i * d:(i + 1) * d] for i in range(6)]
    sh1, sc1, g1, sh2, sc2, g2 = [t[:b].reshape(b, 1, d) for t in chunks]
    csh1, csc1, cg1, csh2, csc2, cg2 = [jnp.broadcast_to(t[b:b + 1].reshape(1, 1, d), (b, 1, d)) for t in chunks]

    w_in = _cast_weight(p['w_in'], layer_idx, 7 * LANES)
    w_branch = p['w_branch']
    w_out = p['w_out']
    tn_in = OFF_GT // 2

    hx = _prenorm(x, p['norm1'], sc1, sh1)
    hc = _prenorm(ctx, p['norm1'], csc1, csh1)
    ux = _matmul(hx.reshape(b * s, d), w_in, OFF_GT, 512, tn_in).reshape(b, s, OFF_GT)
    ucx = _matmul(hc.reshape(b * c, d), w_in, OFF_GT, 512, tn_in).reshape(b, c, OFF_GT)

    ys = _mixers(ux, ucx, p, lam_init, True)
    acc = _merge1(hx.reshape(b * s, d), [y.reshape(b * s, BRANCH_W) for y in ys], w_in, OFF_GT, w_branch, layer_idx)
    x = _merge2(acc.reshape(b, s, d), w_out, layer_idx, x, g1)
    if update_ctx:
        ycs = _mixers(ucx, ucx, p, lam_init, False)
        acc_c = _merge1(hc.reshape(b * c, d), [y.reshape(b * c, BRANCH_W) for y in ycs], w_in, OFF_GT, w_branch, layer_idx)
        ctx = _merge2(acc_c.reshape(b, c, d), w_out, layer_idx, ctx, cg1)

    w_router = jnp.zeros((d, LANES), F32).at[:, :N_GROUPS].set(p['r_group_w'])
    w_router = w_router.at[:, N_GROUPS:N_GROUPS + N_EXPERTS].set(p['r_expert_w'])
    b_router = jnp.zeros((1, LANES), F32).at[0, :N_GROUPS].set(p['r_group_b'])
    b_router = b_router.at[0, N_GROUPS:N_GROUPS + N_EXPERTS].set(p['r_expert_b'])
    h2, e_idx, e_w = _route(x, p['norm2'], sc2, sh2, w_router, b_router)
    e_idx = e_idx.reshape(b * s, LANES)[:, :TOP_K]
    e_w = e_w.reshape(b * s, LANES)[:, :TOP_K]
    if update_ctx:
        h2c, ec_idx, ec_w = _route(ctx, p['norm2'], csc2, csh2, w_router, b_router)
        h2 = jnp.concatenate([h2, h2c], axis=0)
        e_idx = jnp.concatenate([e_idx, ec_idx.reshape(b * c, LANES)[:, :TOP_K]], axis=0)
        e_w = jnp.concatenate([e_w, ec_w.reshape(b * c, LANES)[:, :TOP_K]], axis=0)
    n_tok = e_idx.shape[0]
    blk_e, nused, nval, gidx, sidx, slot_w = _moe_plan(e_idx, e_w, MOE_TM)
    yk = _moe_experts(h2, blk_e, nused, nval, gidx, sidx, slot_w, p['e_w1'], p['e_w3'], p['e_w2'], layer_idx)
    x = _moe_combine(x, yk, g2, 0, n_tok)
    if update_ctx:
        ctx = _moe_combine(ctx, yk, cg2, b * s, n_tok)
    return x, ctx


def kernel(x, c, ctx, c_ctx, w_mod, b_mod, norm1, norm2, w_in, win_sink, win_qnorm, win_knorm, diff_qnorm, diff_knorm, diff_lambda, diff_subln, hy_conv_w, hy_conv_b, hy_w1, hy_b1, hy_w2, hy_b2, hy_w3, hy_freq, hy_decay, hy_skip, pool_proj, pool_scale, w_branch, w_out, r_group_w, r_group_b, r_expert_w, r_expert_b, e_w1, e_w3, e_w2):
    depth = w_mod.shape[0]
    b, _, d = x.shape
    cs = jnp.zeros((8, d), F32).at[:b].set(c).at[b].set(c_ctx)
    stacked = dict(w_in=w_in, w_branch=w_branch.astype(BF16), w_out=w_out.astype(BF16),
                   e_w1=e_w1.astype(BF16), e_w3=e_w3.astype(BF16), e_w2=e_w2.astype(BF16))
    for l in range(depth):
        p = dict(norm1=norm1[l], norm2=norm2[l],
                 win_sink=win_sink[l], win_qnorm=win_qnorm[l], win_knorm=win_knorm[l],
                 diff_qnorm=diff_qnorm[l], diff_knorm=diff_knorm[l], diff_lambda=diff_lambda[l],
                 diff_subln=diff_subln[l], hy_conv_w=hy_conv_w[l], hy_conv_b=hy_conv_b[l],
                 hy_w1=hy_w1[l], hy_b1=hy_b1[l], hy_w2=hy_w2[l], hy_b2=hy_b2[l], hy_w3=hy_w3[l],
                 hy_freq=hy_freq[l], hy_decay=hy_decay[l], hy_skip=hy_skip[l],
                 pool_proj=pool_proj[l], pool_scale=pool_scale[l],
                 r_group_w=r_group_w[l], r_group_b=r_group_b[l], r_expert_w=r_expert_w[l],
                 r_expert_b=r_expert_b[l], **stacked)
        mod = _modulation(cs, w_mod, b_mod, l)
        x, ctx = _layer(x, ctx, mod, p, l, l < depth - 1)
    return x
```

```python
import functools
import math

import jax
import jax.numpy as jnp
import numpy as np
from jax import lax
from jax.experimental import pallas as pl
from jax.experimental.pallas import tpu as pltpu

F32 = jnp.float32
BF16 = jnp.bfloat16

GRID_W = 64
HEAD_DIM = 64
ROPE_THETA = 10000.0
EPS = 1e-6
NEG_INF = -1e30
BLOCK = 128
WINDOW = 128
LANES = 128

WIN_HEADS = 8
WIN_KV_HEADS = 2
WIN_GROUP = WIN_HEADS // WIN_KV_HEADS
DIFF_HEADS = 4
HY_WIDTH = 512
HY_EMB = 33
HY_BANDS = 16
POOL_WINDOWS = (2, 4, 8, 16)
N_BRANCH = 4
BRANCH_W = 512

OFF_WQ = 0
OFF_WK = 512
OFF_WV = 640
OFF_DQ = 768
OFF_DK = 1280
OFF_DV = 1792
OFF_HY = 2304
OFF_PL = 3840
OFF_GT = 4352

N_GROUPS = 4
EXPERTS_PER_GROUP = 8
N_EXPERTS = 32
TOP_K = 2
MOE_TM = 256

FFT_NA = 64
FFT_NB = 128
FFT_N = FFT_NA * FFT_NB


def _params(sem, mib=48):
    return pltpu.CompilerParams(dimension_semantics=sem, vmem_limit_bytes=mib * 2**20)


def _dot(a, b):
    return jnp.dot(a, b, preferred_element_type=F32)


def _split(a):
    hi = a.astype(BF16)
    lo = (a - hi.astype(F32)).astype(BF16)
    return hi, lo


def _dot3(a, b):
    ah, al = _split(a)
    bh, bl = _split(b)
    return _dot(ah, bh) + (_dot(ah, bl) + _dot(al, bh))


def _mod_body(c_ref, w_ref, b_ref, o_ref):
    c = c_ref[...]
    a = (c * jax.nn.sigmoid(c)).astype(BF16)
    o_ref[...] = _dot(a, w_ref[0].astype(BF16)) + b_ref[0]


def _modulation(cs, w_mod, b_mod, layer):
    depth, d, n = w_mod.shape
    tn = 1024
    return pl.pallas_call(
        _mod_body,
        grid=(n // tn,),
        in_specs=[pl.BlockSpec((8, d), lambda j: (0, 0)),
                  pl.BlockSpec((1, d, tn), lambda j: (layer, 0, j)),
                  pl.BlockSpec((1, 1, tn), lambda j: (layer, 0, j))],
        out_specs=pl.BlockSpec((8, tn), lambda j: (0, j)),
        out_shape=jax.ShapeDtypeStruct((8, n), F32),
        compiler_params=_params(("parallel",)),
        name="modulation",
    )(cs, w_mod, b_mod.reshape(depth, 1, n))


def _prenorm_body(x_ref, g_ref, sc_ref, sh_ref, o_ref):
    x = x_ref[0]
    ms = jnp.mean(x * x, axis=-1, keepdims=True)
    y = x * lax.rsqrt(ms + EPS) * g_ref[...]
    o_ref[0] = (y * (1.0 + sc_ref[0]) + sh_ref[0]).astype(o_ref.dtype)


def _prenorm(x, gain, scale, shift, out_dtype=BF16):
    b, l, d = x.shape
    ts = min(l, 512)
    return pl.pallas_call(
        _prenorm_body,
        grid=(b, l // ts),
        in_specs=[pl.BlockSpec((1, ts, d), lambda bi, i: (bi, i, 0)),
                  pl.BlockSpec((1, d), lambda bi, i: (0, 0)),
                  pl.BlockSpec((1, 1, d), lambda bi, i: (bi, 0, 0)),
                  pl.BlockSpec((1, 1, d), lambda bi, i: (bi, 0, 0))],
        out_specs=pl.BlockSpec((1, ts, d), lambda bi, i: (bi, i, 0)),
        out_shape=jax.ShapeDtypeStruct((b, l, d), out_dtype),
        compiler_params=_params(("parallel", "parallel")),
        name="prenorm",
    )(x, gain.reshape(1, d), scale, shift)


def _mm_body(a_ref, w_ref, o_ref):
    o_ref[...] = _dot(a_ref[...], w_ref[...]).astype(o_ref.dtype)


def _cast_body(x_ref, o_ref):
    o_ref[...] = x_ref[...].astype(o_ref.dtype)


def _cast_layer_body(x_ref, o_ref):
    o_ref[...] = x_ref[0].astype(o_ref.dtype)


def _cast_weight(w, layer, tn):
    _, k, n = w.shape
    return pl.pallas_call(
        _cast_layer_body,
        grid=(n // tn,),
        in_specs=[pl.BlockSpec((1, k, tn), lambda j: (layer, 0, j))],
        out_specs=pl.BlockSpec((k, tn), lambda j: (0, j)),
        out_shape=jax.ShapeDtypeStruct((k, n), BF16),
        compiler_params=_params(("parallel",)),
        name="cast_weight",
    )(w)


def _matmul(a, w, n, tm, tn, out_dtype=F32):
    m, k = a.shape
    tm = min(tm, m)
    return pl.pallas_call(
        _mm_body,
        grid=(n // tn, m // tm),
        in_specs=[pl.BlockSpec((tm, k), lambda j, i: (i, 0)),
                  pl.BlockSpec((k, tn), lambda j, i: (0, j))],
        out_specs=pl.BlockSpec((tm, tn), lambda j, i: (i, j)),
        out_shape=jax.ShapeDtypeStruct((m, n), out_dtype),
        compiler_params=_params(("parallel", "parallel")),
        name="in_proj",
    )(a, w)


def _rope_tables(n_tok, rope):
    if not rope:
        return jnp.ones((n_tok, LANES), F32), jnp.zeros((n_tok, LANES), F32)
    rows = n_tok // GRID_W
    row = jnp.repeat(jnp.arange(rows), GRID_W).astype(F32)
    col = jnp.tile(jnp.arange(GRID_W), rows).astype(F32)
    quarter = HEAD_DIM // 4
    inv = ROPE_THETA ** (-jnp.arange(quarter, dtype=F32) / quarter)
    ar = row[:, None] * inv
    ac = col[:, None] * inv
    cos64 = jnp.concatenate([jnp.cos(ar), jnp.cos(ar), jnp.cos(ac), jnp.cos(ac)], axis=-1)
    sin64 = jnp.concatenate([-jnp.sin(ar), jnp.sin(ar), -jnp.sin(ac), jnp.sin(ac)], axis=-1)
    return jnp.tile(cos64, (1, 2)), jnp.tile(sin64, (1, 2))


def _head_ones():
    i = np.arange(LANES)
    return jnp.asarray((i[:, None] // HEAD_DIM) == (i[None, :] // HEAD_DIM), BF16)


def _qk_prep_body(x_ref, g_ref, cos_ref, sin_ref, ones_ref, o_ref, *, scale):
    x = x_ref[0]
    hi, lo = _split(x * x)
    ssum = _dot(hi, ones_ref[...]) + _dot(lo, ones_ref[...])
    y = x * lax.rsqrt(ssum * (1.0 / HEAD_DIM) + EPS) * g_ref[...]
    lane = lax.broadcasted_iota(jnp.int32, y.shape, 1)
    quarter = HEAD_DIM // 4
    partner = jnp.where((lane & quarter) == 0, pltpu.roll(y, LANES - quarter, 1), pltpu.roll(y, quarter, 1))
    o = (y * cos_ref[...] + partner * sin_ref[...]) * scale
    if o_ref.shape[2] == LANES:
        o_ref[0] = o.astype(o_ref.dtype)
    else:
        low = lane < HEAD_DIM
        even = jnp.where(low, o, 0.0)
        odd = jnp.where(low, 0.0, o)
        kv_head = pl.program_id(1) // (WIN_GROUP // 2)
        first = jnp.where(kv_head == 0, even, pltpu.roll(even, HEAD_DIM, 1))
        second = jnp.where(kv_head == 0, pltpu.roll(odd, HEAD_DIM, 1), odd)
        o_ref[0] = jnp.concatenate([first, second], axis=-1).astype(o_ref.dtype)


def _qk_prep(ux, col_off, width, gain, tables, scale, pad_heads=False):
    b, l, _ = ux.shape
    cos_t, sin_t = tables
    c0 = col_off // LANES
    ow = 2 * LANES if pad_heads else LANES
    g = jnp.tile(gain.reshape(1, HEAD_DIM), (1, LANES // HEAD_DIM))
    return pl.pallas_call(
        functools.partial(_qk_prep_body, scale=scale),
        grid=(b, width // LANES),
        in_specs=[pl.BlockSpec((1, l, LANES), lambda bi, j: (bi, 0, c0 + j)),
                  pl.BlockSpec((1, LANES), lambda bi, j: (0, 0)),
                  pl.BlockSpec((l, LANES), lambda bi, j: (0, 0)),
                  pl.BlockSpec((l, LANES), lambda bi, j: (0, 0)),
                  pl.BlockSpec((LANES, LANES), lambda bi, j: (0, 0))],
        out_specs=pl.BlockSpec((1, l, ow), lambda bi, j: (bi, 0, j)),
        out_shape=jax.ShapeDtypeStruct((b, l, width // LANES * ow), BF16),
        compiler_params=_params(("parallel", "parallel")),
        name="qk_prep",
    )(ux, g, cos_t, sin_t, _head_ones())


def _cast_swap_body(x_ref, o_ref, s_ref):
    x = x_ref[0]
    o_ref[0] = x.astype(o_ref.dtype)
    s_ref[0] = pltpu.roll(x, HEAD_DIM, 1).astype(s_ref.dtype)


def _cast_cols(ux, col_off, width, swapped=False):
    b, l, _ = ux.shape
    c0 = col_off // LANES
    blk = pl.BlockSpec((1, l, LANES), lambda bi, j: (bi, 0, j))
    shp = jax.ShapeDtypeStruct((b, l, width), BF16)
    return pl.pallas_call(
        _cast_swap_body if swapped else _cast_body,
        grid=(b, width // LANES),
        in_specs=[pl.BlockSpec((1, l, LANES), lambda bi, j: (bi, 0, c0 + j))],
        out_specs=[blk, blk] if swapped else blk,
        out_shape=[shp, shp] if swapped else shp,
        compiler_params=_params(("parallel", "parallel")),
        name="cast_cols",
    )(ux)


def _nt_dot(a, b):
    return lax.dot_general(a, b, (((1,), (1,)), ((), ())), preferred_element_type=F32)


_WIN_STACK = (0, 2, 1, 3)


def _win_heads(q, sink_ref, h, n):
    heads = [h * WIN_GROUP + g for g in _WIN_STACK]
    qh = jnp.concatenate([q[:, a * LANES:(a + 1) * LANES] for a in heads], axis=0)
    sk = jnp.concatenate([jnp.full((n, 1), sink_ref[a], F32) for a in heads], axis=0)
    return qh, sk


def _win_tiles(o_low, o_high, n):
    low = lax.broadcasted_iota(jnp.int32, (n, LANES), 1) < HEAD_DIM
    return [jnp.where(low, o_low[t * n:(t + 1) * n], o_high[t * n:(t + 1) * n]) for t in range(2)]


WIN_STEP_BLOCKS = 8


def _win_attn_body(sink_ref, q_ref, km_ref, k0_ref, kp_ref, vm_ref, v0_ref, vp_ref, wm_ref, w0_ref, wp_ref,
                   kc_ref, vc_ref, wc_ref, o_ref, *, seq, nsub):
    i = pl.program_id(1)
    kslab = jnp.concatenate([km_ref[0], k0_ref[0], kp_ref[0]], axis=0)
    vslab = jnp.concatenate([vm_ref[0], v0_ref[0], vp_ref[0]], axis=0)
    wslab = jnp.concatenate([wm_ref[0], w0_ref[0], wp_ref[0]], axis=0)
    kc = kc_ref[0]
    qi = lax.broadcasted_iota(jnp.int32, (BLOCK, 3 * BLOCK), 0)
    kj = lax.broadcasted_iota(jnp.int32, (BLOCK, 3 * BLOCK), 1)
    band = jnp.abs(kj - BLOCK - qi) <= WINDOW
    half = WIN_GROUP // 2 * BLOCK
    chains = [(j, h) for j in range(nsub) for h in range(WIN_KV_HEADS)]
    scores = []
    for j, h in chains:
        qh, sk = _win_heads(q_ref[0, j * BLOCK:(j + 1) * BLOCK], sink_ref, h, BLOCK)
        kpos = (i * nsub + j - 1) * BLOCK + kj
        valid = jnp.concatenate([band & (kpos >= 0) & (kpos < seq)] * WIN_GROUP, axis=0)
        s_loc = jnp.where(valid, _nt_dot(qh, kslab[j * BLOCK:(j + 3) * BLOCK]), NEG_INF)
        scores.append((s_loc, _nt_dot(qh, kc), sk))
    maxes = [jnp.maximum(jnp.maximum(jnp.max(s_loc, axis=-1, keepdims=True),
                                     jnp.max(s_ctx, axis=-1, keepdims=True)), sk) for s_loc, s_ctx, sk in scores]
    probs = [(jnp.exp(s_loc - m), jnp.exp(s_ctx - m), jnp.exp(sk - m)) for (s_loc, s_ctx, sk), m in zip(scores, maxes)]
    denoms = [jnp.sum(p_loc, axis=-1, keepdims=True) + jnp.sum(p_ctx, axis=-1, keepdims=True) + p_sink
              for p_loc, p_ctx, p_sink in probs]
    tiles = {}
    for (j, h), (p_loc, p_ctx, _), denom in zip(chains, probs, denoms):
        ws = slice(j * BLOCK, (j + 3) * BLOCK)
        v_nat = (vslab[ws], vc_ref[0])
        v_swp = (wslab[ws], wc_ref[0])
        v_low, v_high = (v_nat, v_swp) if h == 0 else (v_swp, v_nat)
        p_loc = p_loc.astype(BF16)
        p_ctx = p_ctx.astype(BF16)
        o_low = (_dot(p_loc[:half], v_low[0]) + _dot(p_ctx[:half], v_low[1])) / denom[:half]
        o_high = (_dot(p_loc[half:], v_high[0]) + _dot(p_ctx[half:], v_high[1])) / denom[half:]
        tiles[(j, h)] = _win_tiles(o_low, o_high, BLOCK)
    for j in range(nsub):
        row = [t for h in range(WIN_KV_HEADS) for t in tiles[(j, h)]]
        o_ref[0, j * BLOCK:(j + 1) * BLOCK] = jnp.concatenate(row, axis=-1).astype(o_ref.dtype)


def _win_attn(q, k, v, vs, kc, vc, vcs, sink):
    b, s, _ = q.shape
    c = kc.shape[1]
    nb = s // BLOCK
    nsub = math.gcd(nb, WIN_STEP_BLOCKS)
    rows = nsub * BLOCK
    kvw = WIN_KV_HEADS * HEAD_DIM
    cur = lambda bi, i: (bi, i, 0)
    prev = pl.BlockSpec((1, BLOCK, kvw), lambda bi, i: (bi, jnp.maximum(i * nsub - 1, 0), 0))
    main = pl.BlockSpec((1, rows, kvw), cur)
    nxt = pl.BlockSpec((1, BLOCK, kvw), lambda bi, i: (bi, jnp.minimum((i + 1) * nsub, nb - 1), 0))
    cspec = pl.BlockSpec((1, c, kvw), lambda bi, i: (bi, 0, 0))
    return pl.pallas_call(
        functools.partial(_win_attn_body, seq=s, nsub=nsub),
        grid=(b, nb // nsub),
        in_specs=[pl.BlockSpec(memory_space=pltpu.SMEM),
                  pl.BlockSpec((1, rows, WIN_HEADS * LANES), cur),
                  prev, main, nxt, prev, main, nxt, prev, main, nxt,
                  cspec, cspec, cspec],
        out_specs=pl.BlockSpec((1, rows, WIN_HEADS * HEAD_DIM), cur),
        out_shape=jax.ShapeDtypeStruct((b, s, WIN_HEADS * HEAD_DIM), BF16),
        compiler_params=_params(("parallel", "parallel")),
        name="win_attn",
    )(sink.astype(F32), q, k, k, k, v, v, v, vs, vs, vs, kc, vc, vcs)


def _ctx_win_attn_body(sink_ref, q_ref, k_ref, v_ref, w_ref, o_ref):
    q = q_ref[0]
    k = k_ref[0]
    c = q.shape[0]
    half = WIN_GROUP // 2 * c
    tiles = []
    for h in range(WIN_KV_HEADS):
        qh, sk = _win_heads(q, sink_ref, h, c)
        s = _nt_dot(qh, k)
        m = jnp.maximum(jnp.max(s, axis=-1, keepdims=True), sk)
        p = jnp.exp(s - m)
        denom = jnp.sum(p, axis=-1, keepdims=True) + jnp.exp(sk - m)
        p = p.astype(BF16)
        v_low, v_high = (v_ref[0], w_ref[0]) if h == 0 else (w_ref[0], v_ref[0])
        tiles.extend(_win_tiles(_dot(p[:half], v_low) / denom[:half], _dot(p[half:], v_high) / denom[half:], c))
    o_ref[0] = jnp.concatenate(tiles, axis=-1).astype(o_ref.dtype)


def _ctx_win_attn(q, k, v, vs, sink):
    b, c, _ = q.shape
    kvw = WIN_KV_HEADS * HEAD_DIM
    cspec = pl.BlockSpec((1, c, kvw), lambda bi: (bi, 0, 0))
    return pl.pallas_call(
        _ctx_win_attn_body,
        grid=(b,),
        in_specs=[pl.BlockSpec(memory_space=pltpu.SMEM),
                  pl.BlockSpec((1, c, WIN_HEADS * LANES), lambda bi: (bi, 0, 0)),
                  cspec, cspec, cspec],
        out_specs=pl.BlockSpec((1, c, WIN_HEADS * HEAD_DIM), lambda bi: (bi, 0, 0)),
        out_shape=jax.ShapeDtypeStruct((b, c, WIN_HEADS * HEAD_DIM), BF16),
        compiler_params=_params(("parallel",)),
        name="ctx_win_attn",
    )(sink.astype(F32), q, k, v, vs)


DIFF_CHAINS = 2


def _diff_attn_body(lam_ref, q_ref, k_ref, v_ref, g_ref, o_ref, *, lam_init):
    q = q_ref[0]
    k = k_ref[0]
    v = v_ref[0]
    lp = lam_ref[...]
    lam = (jnp.exp(jnp.sum(lp[0:1] * lp[1:2], axis=-1, keepdims=True))
           - jnp.exp(jnp.sum(lp[2:3] * lp[3:4], axis=-1, keepdims=True)) + lam_init)
    lane = lax.broadcasted_iota(jnp.int32, q.shape, 1)
    zero = jnp.zeros_like(q)
    q1 = jnp.where(lane < HEAD_DIM, q, zero)
    q2 = jnp.where(lane < HEAD_DIM, zero, q)
    rows = q.shape[0] // DIFF_CHAINS
    for t in range(DIFF_CHAINS):
        rs = slice(t * rows, (t + 1) * rows)
        s1 = _nt_dot(q1[rs], k)
        s2 = _nt_dot(q2[rs], k)
        p1 = jnp.exp2(s1 - jnp.max(s1, axis=-1, keepdims=True))
        p2 = jnp.exp2(s2 - jnp.max(s2, axis=-1, keepdims=True))
        l1 = jnp.sum(p1, axis=-1, keepdims=True)
        r21 = lam * l1 / jnp.sum(p2, axis=-1, keepdims=True)
        a = (p1 - p2 * r21).astype(BF16)
        o = _dot(a, v) / l1
        ms = jnp.mean(o * o, axis=-1, keepdims=True)
        o_ref[0, rs] = (o * lax.rsqrt(ms + EPS) * g_ref[...] * (1.0 - lam_init)).astype(o_ref.dtype)


def _diff_attn(q, k, v, lam_params, subln, lam_init, tq):
    b, nq, w = q.shape
    nk = k.shape[1]
    tq = min(tq, nq)
    hw = 2 * HEAD_DIM
    return pl.pallas_call(
        functools.partial(_diff_attn_body, lam_init=lam_init),
        grid=(b, DIFF_HEADS, nq // tq),
        in_specs=[pl.BlockSpec((4, HEAD_DIM), lambda bi, h, t: (0, 0)),
                  pl.BlockSpec((1, tq, hw), lambda bi, h, t: (bi, t, h)),
                  pl.BlockSpec((1, nk, hw), lambda bi, h, t: (bi, 0, h)),
                  pl.BlockSpec((1, nk, hw), lambda bi, h, t: (bi, 0, h)),
                  pl.BlockSpec((1, hw), lambda bi, h, t: (0, 0))],
        out_specs=pl.BlockSpec((1, tq, hw), lambda bi, h, t: (bi, t, h)),
        out_shape=jax.ShapeDtypeStruct((b, nq, w), BF16),
        compiler_params=_params(("parallel", "parallel", "parallel"), 56),
        name="diff_attn",
    )(lam_params, q, k, v, subln.reshape(1, hw))


def _hy_pre_body(x0_ref, x1_ref, v_ref, w0_ref, w1_ref, wv_ref, b0_ref, b1_ref, bv_ref, z_ref, x0c_ref):
    n = x0_ref.shape[1]
    row = lax.broadcasted_iota(jnp.int32, (n, 1), 0)

    def sconv(u_ref, w_ref, b_ref):
        u = u_ref[0]
        w = w_ref[...]
        up = jnp.where(row == 0, 0.0, pltpu.roll(u, 1, 0))
        un = jnp.where(row == n - 1, 0.0, pltpu.roll(u, n - 1, 0))
        return ((b_ref[...] + up * w[0:1]) + u * w[1:2]) + un * w[2:3]

    x0c_ref[0] = sconv(x0_ref, w0_ref, b0_ref)
    z_ref[0] = sconv(v_ref, wv_ref, bv_ref) * sconv(x1_ref, w1_ref, b1_ref)


def _hy_pre(ux, conv_w, conv_b):
    b, l, _ = ux.shape
    nc = HY_WIDTH // LANES
    c0 = OFF_HY // LANES
    ub = lambda seg: pl.BlockSpec((1, l, LANES), lambda bi, j: (bi, 0, c0 + seg * nc + j))
    wb = lambda seg: pl.BlockSpec((3, LANES), lambda bi, j: (0, seg * nc + j))
    bb = lambda seg: pl.BlockSpec((1, LANES), lambda bi, j: (0, seg * nc + j))
    out = pl.BlockSpec((1, l, LANES), lambda bi, j: (bi, 0, j))
    shp = jax.ShapeDtypeStruct((b, l, HY_WIDTH), F32)
    cb = conv_b.reshape(1, 3 * HY_WIDTH)
    return pl.pallas_call(
        _hy_pre_body,
        grid=(b, nc),
        in_specs=[ub(0), ub(1), ub(2), wb(0), wb(1), wb(2), bb(0), bb(1), bb(2)],
        out_specs=[out, out],
        out_shape=[shp, shp],
        compiler_params=_params(("parallel", "parallel")),
        name="hy_pre",
    )(ux, ux, ux, conv_w, conv_w, conv_w, cb, cb, cb)


def _hy_features(l):
    n = jnp.arange(l, dtype=F32)
    pos = jnp.concatenate([n, l - n])
    t = pos / max(l - 1, 1)
    w = 2.0 * math.pi * pos / l
    f = jnp.linspace(1e-4, HY_BANDS - 1, HY_BANDS, dtype=F32)
    flag = jnp.ones((2 * l, 1), F32).at[l].set(0.0)
    feat = jnp.concatenate([t[:, None], jnp.cos(w[:, None] * f), -jnp.sin(w[:, None] * f), flag], axis=-1)
    return jnp.pad(feat, ((0, 0), (0, LANES - HY_EMB - 1)))


def _hy_filter_body(f_ref, w1_ref, b1_ref, w2_ref, b2_ref, w3_ref, fr_ref, dec_ref, k_ref, sum_ref):
    first = (pl.program_id(0) == 0) & (pl.program_id(1) == 0)

    @pl.when(first)
    def _():
        sum_ref[...] = jnp.zeros_like(sum_ref)

    f = f_ref[...]
    fr = fr_ref[...]
    h = jnp.sin(fr[0:1] * (_dot3(f, w1_ref[...]) + b1_ref[...]))
    h = jnp.sin(fr[1:2] * (_dot3(h, w2_ref[...]) + b2_ref[...]))
    h = _dot3(h, w3_ref[...]) * jnp.exp(-f[:, 0:1] * jnp.abs(dec_ref[...]))
    h = h * f[:, HY_EMB:HY_EMB + 1]
    k_ref[...] = h
    sum_ref[...] += jnp.sum(jnp.abs(h), axis=0, keepdims=True)


def _hy_filter(l, p):
    tl = min(l, 1024)
    nt = l // tl
    order = p['hy_w1'].shape[1]
    w1 = jnp.zeros((LANES, order), F32).at[:HY_EMB].set(p['hy_w1'])
    const = lambda d, i: (0, 0)
    return pl.pallas_call(
        _hy_filter_body,
        grid=(2, nt),
        in_specs=[pl.BlockSpec((tl, LANES), lambda d, i: (d * nt + i, 0)),
                  pl.BlockSpec((LANES, order), const),
                  pl.BlockSpec((1, order), const),
                  pl.BlockSpec((order, order), const),
                  pl.BlockSpec((1, order), const),
                  pl.BlockSpec((order, HY_WIDTH), lambda d, i: (0, d)),
                  pl.BlockSpec((2, order), const),
                  pl.BlockSpec((1, HY_WIDTH), lambda d, i: (0, d))],
        out_specs=[pl.BlockSpec((tl, HY_WIDTH), lambda d, i: (d * nt + i, 0)),
                   pl.BlockSpec((1, HY_WIDTH), const)],
        out_shape=[jax.ShapeDtypeStruct((2 * l, HY_WIDTH), F32), jax.ShapeDtypeStruct((1, HY_WIDTH), F32)],
        compiler_params=_params(("arbitrary", "arbitrary")),
        name="hy_filter",
    )(_hy_features(l), w1, p['hy_b1'].reshape(1, order), p['hy_w2'], p['hy_b2'].reshape(1, order),
      p['hy_w3'], p['hy_freq'], p['hy_decay'].reshape(1, 2 * HY_WIDTH))


def _fft_consts(n_la):
    ka = np.arange(FFT_NA)[:, None]
    la = np.arange(n_la)[None, :]
    ang = 2.0 * math.pi * ka * la / FFT_NA
    f1 = np.concatenate([np.cos(ang), -np.sin(ang)], axis=0)
    lb = np.arange(FFT_NB)[None, :]
    th = 2.0 * math.pi * ka * lb / FFT_N
    tc, ts = np.cos(th), np.sin(th)
    a2 = 2.0 * math.pi * np.arange(FFT_NB)[:, None] * np.arange(FFT_NB)[None, :] / FFT_NB
    gre, gim = np.cos(a2), -np.sin(a2)
    g2 = np.block([[gre, gim], [-gim, gre]])
    g2i = np.block([[gre, -gim], [gim, gre]])
    lo = np.arange(FFT_NA // 2)[:, None]
    ph = 2.0 * math.pi * lo * np.arange(FFT_NA)[None, :] / FFT_NA
    f1i = np.concatenate([np.cos(ph), -np.sin(ph)], axis=1) / FFT_N
    return f1, tc, ts, g2, g2i, f1i


def _fft_fwd(x_ref, f1_ref, tc_ref, ts_ref, g2_ref, p2_ref, cc, dot):
    rhs = jnp.concatenate([x_ref[0, c] for c in range(cc)], axis=1)
    a = dot(f1_ref[...].astype(p2_ref.dtype), rhs.astype(p2_ref.dtype))
    tc = tc_ref[...]
    ts = ts_ref[...]
    for c in range(cc):
        are = a[0:FFT_NA, c * FFT_NB:(c + 1) * FFT_NB]
        aim = a[FFT_NA:2 * FFT_NA, c * FFT_NB:(c + 1) * FFT_NB]
        p2_ref[c * FFT_NA:(c + 1) * FFT_NA, 0:FFT_NB] = (are * tc + aim * ts).astype(p2_ref.dtype)
        p2_ref[c * FFT_NA:(c + 1) * FFT_NA, FFT_NB:2 * FFT_NB] = (aim * tc - are * ts).astype(p2_ref.dtype)
    return dot(p2_ref[...], g2_ref[...].astype(p2_ref.dtype))


def _fft_spec_body(x_ref, f1_ref, tc_ref, ts_ref, g2_ref, o_ref, p2_ref, *, cc):
    o_ref[...] = _fft_fwd(x_ref, f1_ref, tc_ref, ts_ref, g2_ref, p2_ref, cc, _dot3)


def _fft_conv_body(x_ref, kf_ref, f1_ref, tc_ref, ts_ref, g2_ref, g2i_ref, f1i_ref, o_ref, p2_ref, r2_ref, *, cc):
    z = _fft_fwd(x_ref, f1_ref, tc_ref, ts_ref, g2_ref, p2_ref, cc, _dot)
    kf = kf_ref[...]
    zre, zim = z[:, :FFT_NB], z[:, FFT_NB:]
    kre, kim = kf[:, :FFT_NB], kf[:, FFT_NB:]
    y = jnp.concatenate([zre * kre - zim * kim, zre * kim + zim * kre], axis=1).astype(BF16)
    u = _dot(y, g2i_ref[...].astype(BF16))
    tc = tc_ref[...]
    ts = ts_ref[...]
    for c in range(cc):
        ure = u[c * FFT_NA:(c + 1) * FFT_NA, 0:FFT_NB]
        uim = u[c * FFT_NA:(c + 1) * FFT_NA, FFT_NB:2 * FFT_NB]
        r2_ref[0:FFT_NA, c * FFT_NB:(c + 1) * FFT_NB] = (ure * tc - uim * ts).astype(BF16)
        r2_ref[FFT_NA:2 * FFT_NA, c * FFT_NB:(c + 1) * FFT_NB] = (ure * ts + uim * tc).astype(BF16)
    out = _dot(f1i_ref[...].astype(BF16), r2_ref[...])
    for c in range(cc):
        o_ref[0, c] = out[:, c * FFT_NB:(c + 1) * FFT_NB]


FFT_CC = 32


def _fft_spectrum(kt):
    nch, n_la, _ = kt.shape
    cc = FFT_CC
    f1, tc, ts, g2, _, _ = _fft_consts(n_la)
    const = lambda j: (0, 0)
    return pl.pallas_call(
        functools.partial(_fft_spec_body, cc=cc),
        grid=(nch // cc,),
        in_specs=[pl.BlockSpec((1, cc, n_la, FFT_NB), lambda j: (0, j, 0, 0)),
                  pl.BlockSpec((2 * FFT_NA, n_la), const),
                  pl.BlockSpec((FFT_NA, FFT_NB), const),
                  pl.BlockSpec((FFT_NA, FFT_NB), const),
                  pl.BlockSpec((2 * FFT_NB, 2 * FFT_NB), const)],
        out_specs=pl.BlockSpec((cc * FFT_NA, 2 * FFT_NB), lambda j: (j, 0)),
        out_shape=jax.ShapeDtypeStruct((nch * FFT_NA, 2 * FFT_NB), F32),
        scratch_shapes=[pltpu.VMEM((cc * FFT_NA, 2 * FFT_NB), F32)],
        compiler_params=_params(("parallel",)),
        name="fft_spectrum",
    )(kt[None], jnp.asarray(f1, F32), jnp.asarray(tc, F32), jnp.asarray(ts, F32), jnp.asarray(g2, F32))


def _fft_conv(zt, kf):
    b, nch, n_la, _ = zt.shape
    cc = FFT_CC
    f1, tc, ts, g2, g2i, f1i = _fft_consts(n_la)
    const = lambda j, bi: (0, 0)
    return pl.pallas_call(
        functools.partial(_fft_conv_body, cc=cc),
        grid=(nch // cc, b),
        in_specs=[pl.BlockSpec((1, cc, n_la, FFT_NB), lambda j, bi: (bi, j, 0, 0)),
                  pl.BlockSpec((cc * FFT_NA, 2 * FFT_NB), lambda j, bi: (j, 0)),
                  pl.BlockSpec((2 * FFT_NA, n_la), const),
                  pl.BlockSpec((FFT_NA, FFT_NB), const),
                  pl.BlockSpec((FFT_NA, FFT_NB), const),
                  pl.BlockSpec((2 * FFT_NB, 2 * FFT_NB), const),
                  pl.BlockSpec((2 * FFT_NB, 2 * FFT_NB), const),
                  pl.BlockSpec((FFT_NA // 2, 2 * FFT_NA), const)],
        out_specs=pl.BlockSpec((1, cc, n_la, FFT_NB), lambda j, bi: (bi, j, 0, 0)),
        out_shape=jax.ShapeDtypeStruct((b, nch, n_la, FFT_NB), F32),
        scratch_shapes=[pltpu.VMEM((cc * FFT_NA, 2 * FFT_NB), BF16),
                        pltpu.VMEM((2 * FFT_NA, cc * FFT_NB), BF16)],
        compiler_params=_params(("parallel", "parallel")),
        name="fft_conv",
    )(zt, kf, jnp.asarray(f1, F32), jnp.asarray(tc, F32), jnp.asarray(ts, F32),
      jnp.asarray(g2, F32), jnp.asarray(g2i, F32), jnp.asarray(f1i, F32))


def _dft_conv_body(z_ref, k_ref, fw_ref, fi_ref, o_ref):
    l = z_ref.shape[1]
    fw = fw_ref[...]
    n = fw.shape[1]
    kf = _dot3(fw, k_ref[...])
    zf = _dot3(fw[:, :l], z_ref[0])
    kre, kim = kf[:n], kf[n:]
    zre, zim = zf[:n], zf[n:]
    y = jnp.concatenate([zre * kre - zim * kim, zre * kim + zim * kre], axis=0)
    o_ref[0] = _dot3(fi_ref[...], y)


def _dft_conv(z, k):
    b, l, w = z.shape
    n = 2 * l
    ang = 2.0 * math.pi * np.arange(n)[:, None] * np.arange(n)[None, :] / n
    fw = np.concatenate([np.cos(ang), -np.sin(ang)], axis=0)
    ai = ang[:l]
    fi = np.concatenate([np.cos(ai), -np.sin(ai)], axis=1) / n
    return pl.pallas_call(
        _dft_conv_body,
        grid=(b,),
        in_specs=[pl.BlockSpec((1, l, w), lambda bi: (bi, 0, 0)),
                  pl.BlockSpec((n, w), lambda bi: (0, 0)),
                  pl.BlockSpec((2 * n, n), lambda bi: (0, 0)),
                  pl.BlockSpec((l, 2 * n), lambda bi: (0, 0))],
        out_specs=pl.BlockSpec((1, l, w), lambda bi: (bi, 0, 0)),
        out_shape=jax.ShapeDtypeStruct((b, l, w), F32),
        compiler_params=_params(("parallel",)),
        name="dft_conv",
    )(z, k, jnp.asarray(fw, F32), jnp.asarray(fi, F32))


def _hy_post_body(y_ref, z_ref, x0_ref, ks_ref, sk_ref, o_ref):
    y = y_ref[0] / (ks_ref[...] + EPS) + sk_ref[...] * z_ref[0]
    o_ref[0] = (y * x0_ref[0]).astype(o_ref.dtype)


def _hy_post(yraw, z, x0c, ksum, skip):
    b, l, w = z.shape
    ts = min(l, 1024)
    blk = pl.BlockSpec((1, ts, w), lambda bi, i: (bi, i, 0))
    vec = pl.BlockSpec((1, w), lambda bi, i: (0, 0))
    return pl.pallas_call(
        _hy_post_body,
        grid=(b, l // ts),
        in_specs=[blk, blk, blk, vec, vec],
        out_specs=blk,
        out_shape=jax.ShapeDtypeStruct((b, l, w), BF16),
        compiler_params=_params(("parallel", "parallel")),
        name="hy_post",
    )(yraw, z, x0c, ksum, skip.reshape(1, w))


def _hyena(ux, p):
    b, l, _ = ux.shape
    z, x0c = _hy_pre(ux, p['hy_conv_w'], p['hy_conv_b'])
    k_raw, ksum = _hy_filter(l, p)
    if 2 * l == FFT_N:
        kt = k_raw.T.reshape(HY_WIDTH, FFT_NA, FFT_NB)
        kf = _fft_spectrum(kt)
        zt = jnp.swapaxes(z, 1, 2).reshape(b, HY_WIDTH, l // FFT_NB, FFT_NB)
        yt = _fft_conv(zt, kf)
        yraw = jnp.swapaxes(yt.reshape(b, HY_WIDTH, l), 1, 2)
    else:
        yraw = _dft_conv(z, k_raw)
    return _hy_post(yraw, z, x0c, ksum, p['hy_skip'])


def _pool_body(u_ref, w_ref, sc_ref, o_ref, *, win):
    u = u_ref[0]
    n = u.shape[0]
    lo = win // 2
    hi = win - 1 - lo
    row = lax.broadcasted_iota(jnp.int32, (n, 1), 0)
    acc = u
    for j in range(1, lo + 1):
        acc = acc + jnp.where(row >= j, pltpu.roll(u, j, 0), 0.0)
    for j in range(1, hi + 1):
        acc = acc + jnp.where(row < n - j, pltpu.roll(u, n - j, 0), 0.0)
    cnt = (jnp.minimum(row + hi + 1, n) - jnp.maximum(row - lo, 0)).astype(F32)
    d = acc / cnt - u
    o_ref[0] = (_dot(d.astype(BF16), w_ref[0].astype(BF16)) * sc_ref[...]).astype(o_ref.dtype)


def _pool_mix(ux, w_grp, scale):
    b, l, _ = ux.shape
    c0 = OFF_PL // LANES
    outs = []
    for gi, win in enumerate(POOL_WINDOWS):
        outs.append(pl.pallas_call(
            functools.partial(_pool_body, win=win),
            grid=(b,),
            in_specs=[pl.BlockSpec((1, l, LANES), lambda bi, gi=gi: (bi, 0, c0 + gi)),
                      pl.BlockSpec((1, LANES, LANES), lambda bi, gi=gi: (gi, 0, 0)),
                      pl.BlockSpec((1, LANES), lambda bi, gi=gi: (0, gi))],
            out_specs=pl.BlockSpec((1, l, LANES), lambda bi: (bi, 0, 0)),
            out_shape=jax.ShapeDtypeStruct((b, l, LANES), BF16),
            compiler_params=_params(("parallel",)),
            name=f"pool_mix{win}",
        )(ux, w_grp, scale.reshape(1, -1)))
    return jnp.concatenate(outs, axis=-1)


def _merge1_body(hx_ref, y0, y1, y2, y3, g0, g1, g2, g3, b0, b1, b2, b3, o_ref):
    hx = hx_ref[...]
    acc = None
    for y_ref, g_ref, b_ref in ((y0, g0, b0), (y1, g1, b1), (y2, g2, b2), (y3, g3, b3)):
        t = jax.nn.sigmoid(_dot(hx, g_ref[...])) * _dot(y_ref[...], b_ref[0, 0])
        acc = t if acc is None else acc + t
    o_ref[...] = acc.astype(o_ref.dtype)


def _merge1(hx, ys, w_in, gate_off, w_branch, layer, tm=1024, tn=256):
    m, d = hx.shape
    tm = min(tm, m)
    nj = d // tn
    g0 = gate_off // tn
    row = lambda i, j: (i, 0)
    gspec = lambda br: pl.BlockSpec((d, tn), lambda i, j, br=br: (0, g0 + br * nj + j))
    bspec = lambda br: pl.BlockSpec((1, 1, BRANCH_W, tn), lambda i, j, br=br: (layer, br, 0, j))
    return pl.pallas_call(
        _merge1_body,
        grid=(m // tm, nj),
        in_specs=([pl.BlockSpec((tm, d), row)] + [pl.BlockSpec((tm, BRANCH_W), row)] * N_BRANCH
                  + [gspec(br) for br in range(N_BRANCH)] + [bspec(br) for br in range(N_BRANCH)]),
        out_specs=pl.BlockSpec((tm, tn), lambda i, j: (i, j)),
        out_shape=jax.ShapeDtypeStruct((m, d), BF16),
        compiler_params=_params(("parallel", "parallel")),
        name="merge_gate",
    )(hx, *ys, w_in, w_in, w_in, w_in, w_branch, w_branch, w_branch, w_branch)


def _merge2_body(a_ref, w_ref, x_ref, g_ref, n2_ref, sc_ref, sh_ref, wr_ref, br_ref, o_ref, h_ref, e_ref, ew_ref):
    x = x_ref[0] + g_ref[0] * _dot(a_ref[0], w_ref[0])
    o_ref[0] = x
    _route_rows(x, n2_ref, sc_ref, sh_ref, wr_ref, br_ref, h_ref, e_ref, ew_ref)


def _merge2(acc, w_out, layer, x, gate, norm2, scale2, shift2, w_router, b_router, tm=512):
    b, l, d = x.shape
    tm = min(tm, l)
    nl = l // tm
    blk = pl.BlockSpec((1, tm, d), lambda bi, i: (bi, i, 0))
    mod = pl.BlockSpec((1, 1, d), lambda bi, i: (bi, 0, 0))
    sm = pl.BlockSpec((1, tm, LANES), lambda bi, i: (bi, i, 0))
    const = lambda bi, i: (0, 0)
    return pl.pallas_call(
        _merge2_body,
        grid=(b, nl),
        in_specs=[blk, pl.BlockSpec((1, d, d), lambda bi, i: (layer, 0, 0)), blk, mod,
                  pl.BlockSpec((1, d), const), mod, mod,
                  pl.BlockSpec((d, LANES), const), pl.BlockSpec((1, LANES), const)],
        out_specs=[blk, pl.BlockSpec((tm * (d // LANES), LANES), lambda bi, i: (bi * nl + i, 0)), sm, sm],
        out_shape=[jax.ShapeDtypeStruct((b, l, d), F32),
                   jax.ShapeDtypeStruct((b * l * (d // LANES), LANES), F32),
                   jax.ShapeDtypeStruct((b, l, LANES), jnp.int32),
                   jax.ShapeDtypeStruct((b, l, LANES), F32)],
        compiler_params=_params(("parallel", "parallel"), 56),
        name="merge_out",
    )(acc, w_out, x, gate, norm2.reshape(1, d), scale2, shift2, w_router, b_router)


def _store_token_major(ref, x):
    n, d = x.shape
    pitch = d // LANES
    for j in range(pitch):
        ref[pl.ds(j, n, stride=pitch), :] = x[:, j * LANES:(j + 1) * LANES]


def _load_token_major(ref, n):
    pitch = ref.shape[0] // n
    return jnp.concatenate([ref[pl.ds(j, n, stride=pitch), :] for j in range(pitch)], axis=1)


def _route_rows(x, g_ref, sc_ref, sh_ref, wr_ref, br_ref, h_ref, e_ref, w_ref):
    ms = jnp.mean(x * x, axis=-1, keepdims=True)
    h = (x * lax.rsqrt(ms + EPS) * g_ref[...]) * (1.0 + sc_ref[0]) + sh_ref[0]
    _store_token_major(h_ref, h)
    logits = _dot3(h, wr_ref[...]) + br_ref[...]
    lane = lax.broadcasted_iota(jnp.int32, logits.shape, 1)
    big = jnp.int32(LANES)
    gl = jnp.where(lane < N_GROUPS, logits, -jnp.inf)
    ge = jnp.exp(gl - jnp.max(gl, axis=-1, keepdims=True))
    pg = ge / jnp.sum(ge, axis=-1, keepdims=True)
    pg_top = jnp.max(pg, axis=-1, keepdims=True)
    g_idx = jnp.min(jnp.where((pg == pg_top) & (lane < N_GROUPS), lane, big), axis=-1, keepdims=True)
    first = N_GROUPS + EXPERTS_PER_GROUP * g_idx
    le = jnp.where((lane >= first) & (lane < first + EXPERTS_PER_GROUP), logits, -jnp.inf)
    v1 = jnp.max(le, axis=-1, keepdims=True)
    i1 = jnp.min(jnp.where(le == v1, lane, big), axis=-1, keepdims=True)
    le2 = jnp.where(lane == i1, -jnp.inf, le)
    v2 = jnp.max(le2, axis=-1, keepdims=True)
    i2 = jnp.min(jnp.where(le2 == v2, lane, big), axis=-1, keepdims=True)
    e2 = jnp.exp(v2 - v1)
    den = 1.0 + e2
    e_ref[0] = jnp.where(lane == 0, i1 - N_GROUPS, jnp.where(lane == 1, i2 - N_GROUPS, 0))
    w_ref[0] = jnp.where(lane == 0, (1.0 / den) * pg_top, jnp.where(lane == 1, (e2 / den) * pg_top, 0.0))


def _row_scatter_body(idx_ref, x_ref, dst, buf, sem, *, pitch):
    i = pl.program_id(0)
    n = pl.num_programs(0)
    tm = idx_ref.shape[2]
    slot = i % 2

    def wait(s):
        pltpu.make_async_copy(buf.at[s], dst.at[pl.ds(0, tm * pitch)], sem.at[s]).wait()

    @pl.when(i >= 2)
    def _():
        wait(slot)

    buf[slot] = x_ref[...]
    for r in range(tm):
        pltpu.make_async_copy(buf.at[slot, pl.ds(r * pitch, pitch)], dst.at[pl.ds(idx_ref[0, 0, r], pitch)],
                              sem.at[slot]).start(priority=r % 2)

    @pl.when(i == n - 1)
    def _():
        wait(slot)

        @pl.when(n >= 2)
        def _():
            wait(1 - slot)


def _row_scatter(src, idx, pitch):
    nblk, tm = idx.shape
    rows = tm * pitch
    return pl.pallas_call(
        functools.partial(_row_scatter_body, pitch=pitch),
        grid=(nblk,),
        in_specs=[pl.BlockSpec((1, 1, tm), lambda i: (i, 0, 0), memory_space=pltpu.SMEM),
                  pl.BlockSpec((rows, LANES), lambda i: (i, 0))],
        out_specs=pl.BlockSpec(memory_space=pl.ANY),
        out_shape=jax.ShapeDtypeStruct(src.shape, src.dtype),
        scratch_shapes=[pltpu.VMEM((2, rows, LANES), src.dtype), pltpu.SemaphoreType.DMA((2,))],
        compiler_params=_params(("arbitrary",)),
        name="row_scatter",
    )(idx.reshape(nblk, 1, tm), src)


MOE_GATHER_CHUNK = 32


def _ffn_body(blk_e_ref, nused_ref, nval_ref, gi_ref, gn_ref, si_ref, sw_ref, w1_ref, w3_ref, w2_ref, h_hbm, y_hbm,
              xbuf, obuf, sem, ssem, *, pitch):
    i = pl.program_id(0)
    n = pl.num_programs(0)
    tm = sw_ref.shape[0]
    slot = i % 2
    chunk = MOE_GATHER_CHUNK

    def gather(idx_ref, s, nv):
        for c in range(tm // chunk):
            @pl.when(c * chunk < nv)
            def _():
                for r in range(c * chunk, (c + 1) * chunk):
                    pltpu.make_async_copy(h_hbm.at[pl.ds(idx_ref[0, 0, r], pitch)],
                                          xbuf.at[s, pl.ds(r * pitch, pitch)], sem.at[s]).start(priority=r % 2)

    @pl.when(i == 0)
    def _():
        xbuf[...] = jnp.zeros_like(xbuf)
        gather(gi_ref, 0, nval_ref[0])

    @pl.when(i + 1 < n)
    def _():
        gather(gn_ref, 1 - slot, nval_ref[jnp.minimum(i + 1, n - 1)])

    for c in range(tm // chunk):
        @pl.when(c * chunk < nval_ref[i])
        def _():
            pltpu.make_async_copy(h_hbm.at[pl.ds(0, chunk * pitch)],
                                  xbuf.at[slot, pl.ds(c * chunk * pitch, chunk * pitch)], sem.at[slot]).wait()

    def wait_scatter(s):
        pltpu.make_async_copy(obuf.at[s], y_hbm.at[pl.ds(0, tm * pitch)], ssem.at[s]).wait()

    @pl.when(i >= 2)
    def _():
        wait_scatter(slot)

    @pl.when(i < nused_ref[0])
    def _():
        x = _load_token_major(xbuf.at[slot], tm).astype(BF16)
        a = _dot(x, w1_ref[0, 0])
        h = ((a * jax.nn.sigmoid(a)) * _dot(x, w3_ref[0, 0])).astype(BF16)
        _store_token_major(obuf.at[slot], _dot(h, w2_ref[0, 0]) * sw_ref[...])

    @pl.when(i >= nused_ref[0])
    def _():
        obuf[slot] = jnp.zeros(obuf.shape[1:], F32)

    for r in range(tm):
        pltpu.make_async_copy(obuf.at[slot, pl.ds(r * pitch, pitch)], y_hbm.at[pl.ds(si_ref[0, 0, r], pitch)],
                              ssem.at[slot]).start(priority=r % 2)

    @pl.when(i == n - 1)
    def _():
        wait_scatter(slot)

        @pl.when(n >= 2)
        def _():
            wait_scatter(1 - slot)


def _moe_experts(h2t, blk_e, nused, nval, gidx, sidx, slot_w, w1, w3, w2, layer):
    nblk, tm = gidx.shape
    d = w1.shape[2]
    ff = w1.shape[3]
    pitch = d // LANES
    rows = tm * pitch
    smem_blk = lambda f: pl.BlockSpec((1, 1, tm), f, memory_space=pltpu.SMEM)
    grid_spec = pltpu.PrefetchScalarGridSpec(
        num_scalar_prefetch=3,
        grid=(nblk,),
        in_specs=[smem_blk(lambda i, be, nu, nv: (i, 0, 0)),
                  smem_blk(lambda i, be, nu, nv: (jnp.minimum(i + 1, nblk - 1), 0, 0)),
                  smem_blk(lambda i, be, nu, nv: (i, 0, 0)),
                  pl.BlockSpec((tm, 1), lambda i, be, nu, nv: (i, 0)),
                  pl.BlockSpec((1, 1, d, ff), lambda i, be, nu, nv: (layer, be[i], 0, 0)),
                  pl.BlockSpec((1, 1, d, ff), lambda i, be, nu, nv: (layer, be[i], 0, 0)),
                  pl.BlockSpec((1, 1, ff, d), lambda i, be, nu, nv: (layer, be[i], 0, 0)),
                  pl.BlockSpec(memory_space=pl.ANY)],
        out_specs=pl.BlockSpec(memory_space=pl.ANY),
        scratch_shapes=[pltpu.VMEM((2, rows, LANES), F32), pltpu.VMEM((2, rows, LANES), F32),
                        pltpu.SemaphoreType.DMA((2,)), pltpu.SemaphoreType.DMA((2,))],
    )
    gidx3 = (gidx * pitch).reshape(nblk, 1, tm)
    sidx3 = (sidx * pitch).reshape(nblk, 1, tm)
    return pl.pallas_call(
        functools.partial(_ffn_body, pitch=pitch),
        grid_spec=grid_spec,
        out_shape=jax.ShapeDtypeStruct((nblk * rows, LANES), F32),
        compiler_params=_params(("arbitrary",), 56),
        name="moe_ffn",
    )(blk_e, nused, nval, gidx3, gidx3, sidx3, slot_w, w1, w3, w2, h2t)


def _moe_plan(e_idx, e_w, tm):
    n = e_idx.shape[0]
    m = n * TOP_K
    e_flat = e_idx.reshape(m)
    w_flat = e_w.reshape(m)
    order = jnp.argsort(e_flat).astype(jnp.int32)
    experts = jnp.arange(N_EXPERTS, dtype=jnp.int32)
    counts = jnp.sum((e_flat[:, None] == experts[None, :]).astype(jnp.int32), axis=0)
    padded = (counts + tm - 1) // tm * tm
    pad_end = jnp.cumsum(padded)
    pad_start = pad_end - padded
    cnt_start = jnp.cumsum(counts) - counts
    nblk = -(-(m + N_EXPERTS * (tm - 1)) // tm)
    blk_first = jnp.arange(nblk, dtype=jnp.int32) * tm
    blk_e = jnp.minimum(jnp.sum((pad_end[None, :] <= blk_first[:, None]).astype(jnp.int32), axis=1), N_EXPERTS - 1)
    rank = blk_first[:, None] + jnp.arange(tm, dtype=jnp.int32)[None, :] - pad_start[blk_e][:, None]
    valid = rank < counts[blk_e][:, None]
    src = order[jnp.clip(cnt_start[blk_e][:, None] + rank, 0, m - 1)]
    tok = jnp.where(valid, src // TOP_K, 0)
    slot_w = jnp.where(valid, w_flat[src], 0.0)
    pad_rank = (jnp.cumsum(jnp.logical_not(valid).reshape(-1).astype(jnp.int32)) - 1).reshape(nblk, tm)
    dst = jnp.where(valid, (src % TOP_K) * n + src // TOP_K, m + pad_rank)
    nused = (pad_end[-1] // tm).astype(jnp.int32).reshape(1)
    nval = jnp.sum(valid.astype(jnp.int32), axis=1)
    return (blk_e, nused, nval, tok.astype(jnp.int32), dst.astype(jnp.int32),
            slot_w.reshape(nblk * tm, 1).astype(F32))


def _combine_body(x_ref, y0_ref, y1_ref, g_ref, o_ref):
    tm = x_ref.shape[1]
    o_ref[0] = x_ref[0] + g_ref[0] * (_load_token_major(y0_ref, tm) + _load_token_major(y1_ref, tm))


def _moe_combine(x, yk, gate, row_off, n_tok, tm=256):
    b, l, d = x.shape
    tm = min(tm, l)
    assert l % tm == 0 and row_off % tm == 0 and n_tok % tm == 0
    nl = l // tm
    rows = tm * (d // LANES)
    blk = pl.BlockSpec((1, tm, d), lambda bi, i: (bi, i, 0))
    o0 = row_off // tm
    o1 = (n_tok + row_off) // tm
    return pl.pallas_call(
        _combine_body,
        grid=(b, nl),
        in_specs=[blk,
                  pl.BlockSpec((rows, LANES), lambda bi, i: (o0 + bi * nl + i, 0)),
                  pl.BlockSpec((rows, LANES), lambda bi, i: (o1 + bi * nl + i, 0)),
                  pl.BlockSpec((1, 1, d), lambda bi, i: (bi, 0, 0))],
        out_specs=blk,
        out_shape=jax.ShapeDtypeStruct((b, l, d), F32),
        compiler_params=_params(("parallel", "parallel")),
        name="moe_combine",
    )(x, yk, yk, gate)


def _mixers(ux, ucx, p, lam_init, latent):
    b, l, _ = ux.shape
    rope_q = _rope_tables(l, latent)
    no_rope = _rope_tables(ucx.shape[1], False)
    scale = HEAD_DIM ** -0.5
    qw = _qk_prep(ux, OFF_WQ, 512, p['win_qnorm'], rope_q, scale, pad_heads=True)
    qd = _qk_prep(ux, OFF_DQ, 512, p['diff_qnorm'], rope_q, scale * math.log2(math.e))
    kcw = _qk_prep(ucx, OFF_WK, 128, p['win_knorm'], no_rope, 1.0)
    vcw, vcws = _cast_cols(ucx, OFF_WV, 128, swapped=True)
    kcd = _qk_prep(ucx, OFF_DK, 512, p['diff_knorm'], no_rope, 1.0)
    vcd = _cast_cols(ucx, OFF_DV, 512)
    if latent:
        kw = _qk_prep(ux, OFF_WK, 128, p['win_knorm'], rope_q, 1.0)
        vw, vws = _cast_cols(ux, OFF_WV, 128, swapped=True)
        kd = _qk_prep(ux, OFF_DK, 512, p['diff_knorm'], rope_q, 1.0)
        vd = _cast_cols(ux, OFF_DV, 512)
        y_win = _win_attn(qw, kw, vw, vws, kcw, vcw, vcws, p['win_sink'])
        k_all = jnp.concatenate([kd, kcd], axis=1)
        v_all = jnp.concatenate([vd, vcd], axis=1)
    else:
        y_win = _ctx_win_attn(qw, kcw, vcw, vcws, p['win_sink'])
        k_all, v_all = kcd, vcd
    y_diff = _diff_attn(qd, k_all, v_all, p['diff_lambda'], p['diff_subln'], lam_init, 512)
    y_hy = _hyena(ux, p)
    y_pool = _pool_mix(ux, p['pool_proj'], p['pool_scale'])
    return y_win, y_diff, y_hy, y_pool


def _layer(x, ctx, mod, p, layer_idx, update_ctx):
    b, s, d = x.shape
    c = ctx.shape[1]
    lam_init = 0.8 - 0.6 * math.exp(-0.3 * layer_idx)
    chunks = [mod[:, i * d:(i + 1) * d] for i in range(6)]
    sh1, sc1, g1, sh2, sc2, g2 = [t[:b].reshape(b, 1, d) for t in chunks]
    csh1, csc1, cg1, csh2, csc2, cg2 = [jnp.broadcast_to(t[b:b + 1].reshape(1, 1, d), (b, 1, d)) for t in chunks]

    w_in = _cast_weight(p['w_in'], layer_idx, 7 * LANES)
    w_branch = p['w_branch']
    w_out = p['w_out']
    tn_in = OFF_GT // 2

    hx = _prenorm(x, p['norm1'], sc1, sh1)
    hc = _prenorm(ctx, p['norm1'], csc1, csh1)
    ux = _matmul(hx.reshape(b * s, d), w_in, OFF_GT, 512, tn_in).reshape(b, s, OFF_GT)
    ucx = _matmul(hc.reshape(b * c, d), w_in, OFF_GT, 512, tn_in).reshape(b, c, OFF_GT)

    w_router = jnp.zeros((d, LANES), F32).at[:, :N_GROUPS].set(p['r_group_w'])
    w_router = w_router.at[:, N_GROUPS:N_GROUPS + N_EXPERTS].set(p['r_expert_w'])
    b_router = jnp.zeros((1, LANES), F32).at[0, :N_GROUPS].set(p['r_group_b'])
    b_router = b_router.at[0, N_GROUPS:N_GROUPS + N_EXPERTS].set(p['r_expert_b'])

    ys = _mixers(ux, ucx, p, lam_init, True)
    acc = _merge1(hx.reshape(b * s, d), [y.reshape(b * s, BRANCH_W) for y in ys], w_in, OFF_GT, w_branch, layer_idx)
    x, h2, e_idx, e_w = _merge2(acc.reshape(b, s, d), w_out, layer_idx, x, g1, p['norm2'], sc2, sh2, w_router, b_router)
    e_idx = e_idx.reshape(b * s, LANES)[:, :TOP_K]
    e_w = e_w.reshape(b * s, LANES)[:, :TOP_K]
    if update_ctx:
        ycs = _mixers(ucx, ucx, p, lam_init, False)
        acc_c = _merge1(hc.reshape(b * c, d), [y.reshape(b * c, BRANCH_W) for y in ycs], w_in, OFF_GT, w_branch, layer_idx)
        ctx, h2c, ec_idx, ec_w = _merge2(acc_c.reshape(b, c, d), w_out, layer_idx, ctx, cg1, p['norm2'], csc2, csh2,
                                         w_router, b_router)
        h2 = jnp.concatenate([h2, h2c], axis=0)
        e_idx = jnp.concatenate([e_idx, ec_idx.reshape(b * c, LANES)[:, :TOP_K]], axis=0)
        e_w = jnp.concatenate([e_w, ec_w.reshape(b * c, LANES)[:, :TOP_K]], axis=0)
    n_tok = e_idx.shape[0]
    blk_e, nused, nval, gidx, sidx, slot_w = _moe_plan(e_idx, e_w, MOE_TM)
    yk = _moe_experts(h2, blk_e, nused, nval, gidx, sidx, slot_w, p['e_w1'], p['e_w3'], p['e_w2'], layer_idx)
    x = _moe_combine(x, yk, g2, 0, n_tok)
    if update_ctx:
        ctx = _moe_combine(ctx, yk, cg2, b * s, n_tok)
    return x, ctx


def kernel(x, c, ctx, c_ctx, w_mod, b_mod, norm1, norm2, w_in, win_sink, win_qnorm, win_knorm, diff_qnorm, diff_knorm, diff_lambda, diff_subln, hy_conv_w, hy_conv_b, hy_w1, hy_b1, hy_w2, hy_b2, hy_w3, hy_freq, hy_decay, hy_skip, pool_proj, pool_scale, w_branch, w_out, r_group_w, r_group_b, r_expert_w, r_expert_b, e_w1, e_w3, e_w2):
    depth = w_mod.shape[0]
    b, _, d = x.shape
    cs = jnp.zeros((8, d), F32).at[:b].set(c).at[b].set(c_ctx)
    stacked = dict(w_in=w_in, w_branch=w_branch.astype(BF16), w_out=w_out.astype(BF16),
                   e_w1=e_w1.astype(BF16), e_w3=e_w3.astype(BF16), e_w2=e_w2.astype(BF16))
    for l in range(depth):
        p = dict(norm1=norm1[l], norm2=norm2[l],
                 win_sink=win_sink[l], win_qnorm=win_qnorm[l], win_knorm=win_knorm[l],
                 diff_qnorm=diff_qnorm[l], diff_knorm=diff_knorm[l], diff_lambda=diff_lambda[l],
                 diff_subln=diff_subln[l], hy_conv_w=hy_conv_w[l], hy_conv_b=hy_conv_b[l],
                 hy_w1=hy_w1[l], hy_b1=hy_b1[l], hy_w2=hy_w2[l], hy_b2=hy_b2[l], hy_w3=hy_w3[l],
                 hy_freq=hy_freq[l], hy_decay=hy_decay[l], hy_skip=hy_skip[l],
                 pool_proj=pool_proj[l], pool_scale=pool_scale[l],
                 r_group_w=r_group_w[l], r_group_b=r_group_b[l], r_expert_w=r_expert_w[l],
                 r_expert_b=r_expert_b[l], **stacked)
        mod = _modulation(cs, w_mod, b_mod, l)
        x, ctx = _layer(x, ctx, mod, p, l, l < depth - 1)
    return x
```

```python
import functools
import math

import jax
import jax.numpy as jnp
import numpy as np
from jax import lax
from jax.experimental import pallas as pl
from jax.experimental.pallas import tpu as pltpu

F32 = jnp.float32
BF16 = jnp.bfloat16

GRID_W = 64
HEAD_DIM = 64
ROPE_THETA = 10000.0
EPS = 1e-6
NEG_INF = -1e30
BLOCK = 128
WINDOW = 128
LANES = 128

WIN_HEADS = 8
WIN_KV_HEADS = 2
WIN_GROUP = WIN_HEADS // WIN_KV_HEADS
DIFF_HEADS = 4
HY_WIDTH = 512
HY_EMB = 33
HY_BANDS = 16
POOL_WINDOWS = (2, 4, 8, 16)
N_BRANCH = 4
BRANCH_W = 512

OFF_WQ = 0
OFF_WK = 512
OFF_WV = 640
OFF_DQ = 768
OFF_DK = 1280
OFF_DV = 1792
OFF_HY = 2304
OFF_PL = 3840
OFF_GT = 4352

N_GROUPS = 4
EXPERTS_PER_GROUP = 8
N_EXPERTS = 32
TOP_K = 2
MOE_TM = 256

FFT_NA = 64
FFT_NB = 128
FFT_N = FFT_NA * FFT_NB


def _params(sem, mib=48):
    return pltpu.CompilerParams(dimension_semantics=sem, vmem_limit_bytes=mib * 2**20)


def _dot(a, b):
    return jnp.dot(a, b, preferred_element_type=F32)


def _split(a):
    hi = a.astype(BF16)
    lo = (a - hi.astype(F32)).astype(BF16)
    return hi, lo


def _dot3(a, b):
    ah, al = _split(a)
    bh, bl = _split(b)
    return _dot(ah, bh) + (_dot(ah, bl) + _dot(al, bh))


def _mod_body(c_ref, w_ref, b_ref, o_ref):
    c = c_ref[...]
    a = (c * jax.nn.sigmoid(c)).astype(BF16)
    o_ref[...] = _dot(a, w_ref[0].astype(BF16)) + b_ref[0]


def _modulation(cs, w_mod, b_mod, layer):
    depth, d, n = w_mod.shape
    tn = 1024
    return pl.pallas_call(
        _mod_body,
        grid=(n // tn,),
        in_specs=[pl.BlockSpec((8, d), lambda j: (0, 0)),
                  pl.BlockSpec((1, d, tn), lambda j: (layer, 0, j)),
                  pl.BlockSpec((1, 1, tn), lambda j: (layer, 0, j))],
        out_specs=pl.BlockSpec((8, tn), lambda j: (0, j)),
        out_shape=jax.ShapeDtypeStruct((8, n), F32),
        compiler_params=_params(("parallel",)),
        name="modulation",
    )(cs, w_mod, b_mod.reshape(depth, 1, n))


def _prenorm_body(x_ref, g_ref, sc_ref, sh_ref, o_ref):
    x = x_ref[0]
    ms = jnp.mean(x * x, axis=-1, keepdims=True)
    y = x * lax.rsqrt(ms + EPS) * g_ref[...]
    o_ref[0] = (y * (1.0 + sc_ref[0]) + sh_ref[0]).astype(o_ref.dtype)


def _prenorm(x, gain, scale, shift, out_dtype=BF16):
    b, l, d = x.shape
    ts = min(l, 512)
    return pl.pallas_call(
        _prenorm_body,
        grid=(b, l // ts),
        in_specs=[pl.BlockSpec((1, ts, d), lambda bi, i: (bi, i, 0)),
                  pl.BlockSpec((1, d), lambda bi, i: (0, 0)),
                  pl.BlockSpec((1, 1, d), lambda bi, i: (bi, 0, 0)),
                  pl.BlockSpec((1, 1, d), lambda bi, i: (bi, 0, 0))],
        out_specs=pl.BlockSpec((1, ts, d), lambda bi, i: (bi, i, 0)),
        out_shape=jax.ShapeDtypeStruct((b, l, d), out_dtype),
        compiler_params=_params(("parallel", "parallel")),
        name="prenorm",
    )(x, gain.reshape(1, d), scale, shift)


def _mm_body(a_ref, w_ref, o_ref):
    o_ref[...] = _dot(a_ref[...], w_ref[...]).astype(o_ref.dtype)


def _cast_body(x_ref, o_ref):
    o_ref[...] = x_ref[...].astype(o_ref.dtype)


def _cast_layer_body(x_ref, o_ref):
    o_ref[...] = x_ref[0].astype(o_ref.dtype)


def _cast_weight(w, layer, tn):
    _, k, n = w.shape
    return pl.pallas_call(
        _cast_layer_body,
        grid=(n // tn,),
        in_specs=[pl.BlockSpec((1, k, tn), lambda j: (layer, 0, j))],
        out_specs=pl.BlockSpec((k, tn), lambda j: (0, j)),
        out_shape=jax.ShapeDtypeStruct((k, n), BF16),
        compiler_params=_params(("parallel",)),
        name="cast_weight",
    )(w)


def _matmul(a, w, n, tm, tn, out_dtype=F32):
    m, k = a.shape
    tm = min(tm, m)
    return pl.pallas_call(
        _mm_body,
        grid=(n // tn, m // tm),
        in_specs=[pl.BlockSpec((tm, k), lambda j, i: (i, 0)),
                  pl.BlockSpec((k, tn), lambda j, i: (0, j))],
        out_specs=pl.BlockSpec((tm, tn), lambda j, i: (i, j)),
        out_shape=jax.ShapeDtypeStruct((m, n), out_dtype),
        compiler_params=_params(("parallel", "parallel")),
        name="in_proj",
    )(a, w)


def _rope_tables(n_tok, rope):
    if not rope:
        return jnp.ones((n_tok, LANES), F32), jnp.zeros((n_tok, LANES), F32)
    rows = n_tok // GRID_W
    row = jnp.repeat(jnp.arange(rows), GRID_W).astype(F32)
    col = jnp.tile(jnp.arange(GRID_W), rows).astype(F32)
    quarter = HEAD_DIM // 4
    inv = ROPE_THETA ** (-jnp.arange(quarter, dtype=F32) / quarter)
    ar = row[:, None] * inv
    ac = col[:, None] * inv
    cos64 = jnp.concatenate([jnp.cos(ar), jnp.cos(ar), jnp.cos(ac), jnp.cos(ac)], axis=-1)
    sin64 = jnp.concatenate([-jnp.sin(ar), jnp.sin(ar), -jnp.sin(ac), jnp.sin(ac)], axis=-1)
    return jnp.tile(cos64, (1, 2)), jnp.tile(sin64, (1, 2))


def _head_ones():
    i = np.arange(LANES)
    return jnp.asarray((i[:, None] // HEAD_DIM) == (i[None, :] // HEAD_DIM), BF16)


def _qk_prep_body(x_ref, g_ref, cos_ref, sin_ref, ones_ref, o_ref, *, scale):
    x = x_ref[0]
    hi, lo = _split(x * x)
    ssum = _dot(hi, ones_ref[...]) + _dot(lo, ones_ref[...])
    y = x * lax.rsqrt(ssum * (1.0 / HEAD_DIM) + EPS) * g_ref[...]
    lane = lax.broadcasted_iota(jnp.int32, y.shape, 1)
    quarter = HEAD_DIM // 4
    partner = jnp.where((lane & quarter) == 0, pltpu.roll(y, LANES - quarter, 1), pltpu.roll(y, quarter, 1))
    o = (y * cos_ref[...] + partner * sin_ref[...]) * scale
    if o_ref.shape[2] == LANES:
        o_ref[0] = o.astype(o_ref.dtype)
    else:
        low = lane < HEAD_DIM
        even = jnp.where(low, o, 0.0)
        odd = jnp.where(low, 0.0, o)
        kv_head = pl.program_id(1) // (WIN_GROUP // 2)
        first = jnp.where(kv_head == 0, even, pltpu.roll(even, HEAD_DIM, 1))
        second = jnp.where(kv_head == 0, pltpu.roll(odd, HEAD_DIM, 1), odd)
        o_ref[0] = jnp.concatenate([first, second], axis=-1).astype(o_ref.dtype)


def _qk_prep(ux, col_off, width, gain, tables, scale, pad_heads=False):
    b, l, _ = ux.shape
    cos_t, sin_t = tables
    c0 = col_off // LANES
    ow = 2 * LANES if pad_heads else LANES
    g = jnp.tile(gain.reshape(1, HEAD_DIM), (1, LANES // HEAD_DIM))
    return pl.pallas_call(
        functools.partial(_qk_prep_body, scale=scale),
        grid=(b, width // LANES),
        in_specs=[pl.BlockSpec((1, l, LANES), lambda bi, j: (bi, 0, c0 + j)),
                  pl.BlockSpec((1, LANES), lambda bi, j: (0, 0)),
                  pl.BlockSpec((l, LANES), lambda bi, j: (0, 0)),
                  pl.BlockSpec((l, LANES), lambda bi, j: (0, 0)),
                  pl.BlockSpec((LANES, LANES), lambda bi, j: (0, 0))],
        out_specs=pl.BlockSpec((1, l, ow), lambda bi, j: (bi, 0, j)),
        out_shape=jax.ShapeDtypeStruct((b, l, width // LANES * ow), BF16),
        compiler_params=_params(("parallel", "parallel")),
        name="qk_prep",
    )(ux, g, cos_t, sin_t, _head_ones())


def _cast_swap_body(x_ref, o_ref, s_ref):
    x = x_ref[0]
    o_ref[0] = x.astype(o_ref.dtype)
    s_ref[0] = pltpu.roll(x, HEAD_DIM, 1).astype(s_ref.dtype)


def _cast_cols(ux, col_off, width, swapped=False):
    b, l, _ = ux.shape
    c0 = col_off // LANES
    blk = pl.BlockSpec((1, l, LANES), lambda bi, j: (bi, 0, j))
    shp = jax.ShapeDtypeStruct((b, l, width), BF16)
    return pl.pallas_call(
        _cast_swap_body if swapped else _cast_body,
        grid=(b, width // LANES),
        in_specs=[pl.BlockSpec((1, l, LANES), lambda bi, j: (bi, 0, c0 + j))],
        out_specs=[blk, blk] if swapped else blk,
        out_shape=[shp, shp] if swapped else shp,
        compiler_params=_params(("parallel", "parallel")),
        name="cast_cols",
    )(ux)


def _nt_dot(a, b):
    return lax.dot_general(a, b, (((1,), (1,)), ((), ())), preferred_element_type=F32)


_WIN_STACK = (0, 2, 1, 3)


def _win_heads(q, sink_ref, h, n):
    heads = [h * WIN_GROUP + g for g in _WIN_STACK]
    qh = jnp.concatenate([q[:, a * LANES:(a + 1) * LANES] for a in heads], axis=0)
    sk = jnp.concatenate([jnp.full((n, 1), sink_ref[a], F32) for a in heads], axis=0)
    return qh, sk


def _win_tiles(o_low, o_high, n):
    low = lax.broadcasted_iota(jnp.int32, (n, LANES), 1) < HEAD_DIM
    return [jnp.where(low, o_low[t * n:(t + 1) * n], o_high[t * n:(t + 1) * n]) for t in range(2)]


WIN_STEP_BLOCKS = 8


def _win_attn_body(sink_ref, q_ref, km_ref, k0_ref, kp_ref, vm_ref, v0_ref, vp_ref, wm_ref, w0_ref, wp_ref,
                   kc_ref, vc_ref, wc_ref, o_ref, *, seq, nsub):
    i = pl.program_id(1)
    kslab = jnp.concatenate([km_ref[0], k0_ref[0], kp_ref[0]], axis=0)
    vslab = jnp.concatenate([vm_ref[0], v0_ref[0], vp_ref[0]], axis=0)
    wslab = jnp.concatenate([wm_ref[0], w0_ref[0], wp_ref[0]], axis=0)
    kc = kc_ref[0]
    qi = lax.broadcasted_iota(jnp.int32, (BLOCK, 3 * BLOCK), 0)
    kj = lax.broadcasted_iota(jnp.int32, (BLOCK, 3 * BLOCK), 1)
    band = jnp.abs(kj - BLOCK - qi) <= WINDOW
    half = WIN_GROUP // 2 * BLOCK
    chains = [(j, h) for j in range(nsub) for h in range(WIN_KV_HEADS)]
    scores = []
    for j, h in chains:
        qh, sk = _win_heads(q_ref[0, j * BLOCK:(j + 1) * BLOCK], sink_ref, h, BLOCK)
        kpos = (i * nsub + j - 1) * BLOCK + kj
        valid = jnp.concatenate([band & (kpos >= 0) & (kpos < seq)] * WIN_GROUP, axis=0)
        s_loc = jnp.where(valid, _nt_dot(qh, kslab[j * BLOCK:(j + 3) * BLOCK]), NEG_INF)
        scores.append((s_loc, _nt_dot(qh, kc), sk))
    maxes = [jnp.maximum(jnp.maximum(jnp.max(s_loc, axis=-1, keepdims=True),
                                     jnp.max(s_ctx, axis=-1, keepdims=True)), sk) for s_loc, s_ctx, sk in scores]
    probs = [(jnp.exp(s_loc - m), jnp.exp(s_ctx - m), jnp.exp(sk - m)) for (s_loc, s_ctx, sk), m in zip(scores, maxes)]
    denoms = [jnp.sum(p_loc, axis=-1, keepdims=True) + jnp.sum(p_ctx, axis=-1, keepdims=True) + p_sink
              for p_loc, p_ctx, p_sink in probs]
    tiles = {}
    for (j, h), (p_loc, p_ctx, _), denom in zip(chains, probs, denoms):
        ws = slice(j * BLOCK, (j + 3) * BLOCK)
        v_nat = (vslab[ws], vc_ref[0])
        v_swp = (wslab[ws], wc_ref[0])
        v_low, v_high = (v_nat, v_swp) if h == 0 else (v_swp, v_nat)
        p_loc = p_loc.astype(BF16)
        p_ctx = p_ctx.astype(BF16)
        o_low = (_dot(p_loc[:half], v_low[0]) + _dot(p_ctx[:half], v_low[1])) / denom[:half]
        o_high = (_dot(p_loc[half:], v_high[0]) + _dot(p_ctx[half:], v_high[1])) / denom[half:]
        tiles[(j, h)] = _win_tiles(o_low, o_high, BLOCK)
    for j in range(nsub):
        row = [t for h in range(WIN_KV_HEADS) for t in tiles[(j, h)]]
        o_ref[0, j * BLOCK:(j + 1) * BLOCK] = jnp.concatenate(row, axis=-1).astype(o_ref.dtype)


def _win_attn(q, k, v, vs, kc, vc, vcs, sink):
    b, s, _ = q.shape
    c = kc.shape[1]
    nb = s // BLOCK
    nsub = math.gcd(nb, WIN_STEP_BLOCKS)
    rows = nsub * BLOCK
    kvw = WIN_KV_HEADS * HEAD_DIM
    cur = lambda bi, i: (bi, i, 0)
    prev = pl.BlockSpec((1, BLOCK, kvw), lambda bi, i: (bi, jnp.maximum(i * nsub - 1, 0), 0))
    main = pl.BlockSpec((1, rows, kvw), cur)
    nxt = pl.BlockSpec((1, BLOCK, kvw), lambda bi, i: (bi, jnp.minimum((i + 1) * nsub, nb - 1), 0))
    cspec = pl.BlockSpec((1, c, kvw), lambda bi, i: (bi, 0, 0))
    return pl.pallas_call(
        functools.partial(_win_attn_body, seq=s, nsub=nsub),
        grid=(b, nb // nsub),
        in_specs=[pl.BlockSpec(memory_space=pltpu.SMEM),
                  pl.BlockSpec((1, rows, WIN_HEADS * LANES), cur),
                  prev, main, nxt, prev, main, nxt, prev, main, nxt,
                  cspec, cspec, cspec],
        out_specs=pl.BlockSpec((1, rows, WIN_HEADS * HEAD_DIM), cur),
        out_shape=jax.ShapeDtypeStruct((b, s, WIN_HEADS * HEAD_DIM), BF16),
        compiler_params=_params(("parallel", "parallel")),
        name="win_attn",
    )(sink.astype(F32), q, k, k, k, v, v, v, vs, vs, vs, kc, vc, vcs)


def _ctx_win_attn_body(sink_ref, q_ref, k_ref, v_ref, w_ref, o_ref):
    q = q_ref[0]
    k = k_ref[0]
    c = q.shape[0]
    half = WIN_GROUP // 2 * c
    tiles = []
    for h in range(WIN_KV_HEADS):
        qh, sk = _win_heads(q, sink_ref, h, c)
        s = _nt_dot(qh, k)
        m = jnp.maximum(jnp.max(s, axis=-1, keepdims=True), sk)
        p = jnp.exp(s - m)
        denom = jnp.sum(p, axis=-1, keepdims=True) + jnp.exp(sk - m)
        p = p.astype(BF16)
        v_low, v_high = (v_ref[0], w_ref[0]) if h == 0 else (w_ref[0], v_ref[0])
        tiles.extend(_win_tiles(_dot(p[:half], v_low) / denom[:half], _dot(p[half:], v_high) / denom[half:], c))
    o_ref[0] = jnp.concatenate(tiles, axis=-1).astype(o_ref.dtype)


def _ctx_win_attn(q, k, v, vs, sink):
    b, c, _ = q.shape
    kvw = WIN_KV_HEADS * HEAD_DIM
    cspec = pl.BlockSpec((1, c, kvw), lambda bi: (bi, 0, 0))
    return pl.pallas_call(
        _ctx_win_attn_body,
        grid=(b,),
        in_specs=[pl.BlockSpec(memory_space=pltpu.SMEM),
                  pl.BlockSpec((1, c, WIN_HEADS * LANES), lambda bi: (bi, 0, 0)),
                  cspec, cspec, cspec],
        out_specs=pl.BlockSpec((1, c, WIN_HEADS * HEAD_DIM), lambda bi: (bi, 0, 0)),
        out_shape=jax.ShapeDtypeStruct((b, c, WIN_HEADS * HEAD_DIM), BF16),
        compiler_params=_params(("parallel",)),
        name="ctx_win_attn",
    )(sink.astype(F32), q, k, v, vs)


DIFF_CHAINS = 2


def _diff_attn_body(lam_ref, q_ref, k_ref, v_ref, g_ref, o_ref, *, lam_init):
    q = q_ref[0]
    k = k_ref[0]
    v = v_ref[0]
    lp = lam_ref[...]
    lam = (jnp.exp(jnp.sum(lp[0:1] * lp[1:2], axis=-1, keepdims=True))
           - jnp.exp(jnp.sum(lp[2:3] * lp[3:4], axis=-1, keepdims=True)) + lam_init)
    lane = lax.broadcasted_iota(jnp.int32, q.shape, 1)
    zero = jnp.zeros_like(q)
    q1 = jnp.where(lane < HEAD_DIM, q, zero)
    q2 = jnp.where(lane < HEAD_DIM, zero, q)
    rows = q.shape[0] // DIFF_CHAINS
    for t in range(DIFF_CHAINS):
        rs = slice(t * rows, (t + 1) * rows)
        s1 = _nt_dot(q1[rs], k)
        s2 = _nt_dot(q2[rs], k)
        p1 = jnp.exp2(s1 - jnp.max(s1, axis=-1, keepdims=True))
        p2 = jnp.exp2(s2 - jnp.max(s2, axis=-1, keepdims=True))
        l1 = jnp.sum(p1, axis=-1, keepdims=True)
        r21 = lam * l1 / jnp.sum(p2, axis=-1, keepdims=True)
        a = (p1 - p2 * r21).astype(BF16)
        o = _dot(a, v) / l1
        ms = jnp.mean(o * o, axis=-1, keepdims=True)
        o_ref[0, rs] = (o * lax.rsqrt(ms + EPS) * g_ref[...] * (1.0 - lam_init)).astype(o_ref.dtype)


def _diff_attn(q, k, v, lam_params, subln, lam_init, tq):
    b, nq, w = q.shape
    nk = k.shape[1]
    tq = min(tq, nq)
    hw = 2 * HEAD_DIM
    return pl.pallas_call(
        functools.partial(_diff_attn_body, lam_init=lam_init),
        grid=(b, DIFF_HEADS, nq // tq),
        in_specs=[pl.BlockSpec((4, HEAD_DIM), lambda bi, h, t: (0, 0)),
                  pl.BlockSpec((1, tq, hw), lambda bi, h, t: (bi, t, h)),
                  pl.BlockSpec((1, nk, hw), lambda bi, h, t: (bi, 0, h)),
                  pl.BlockSpec((1, nk, hw), lambda bi, h, t: (bi, 0, h)),
                  pl.BlockSpec((1, hw), lambda bi, h, t: (0, 0))],
        out_specs=pl.BlockSpec((1, tq, hw), lambda bi, h, t: (bi, t, h)),
        out_shape=jax.ShapeDtypeStruct((b, nq, w), BF16),
        compiler_params=_params(("parallel", "parallel", "parallel"), 56),
        name="diff_attn",
    )(lam_params, q, k, v, subln.reshape(1, hw))


def _hy_pre_body(x0_ref, x1_ref, v_ref, w0_ref, w1_ref, wv_ref, b0_ref, b1_ref, bv_ref, z_ref, x0c_ref):
    n = x0_ref.shape[1]
    row = lax.broadcasted_iota(jnp.int32, (n, 1), 0)

    def sconv(u_ref, w_ref, b_ref):
        u = u_ref[0]
        w = w_ref[...]
        up = jnp.where(row == 0, 0.0, pltpu.roll(u, 1, 0))
        un = jnp.where(row == n - 1, 0.0, pltpu.roll(u, n - 1, 0))
        return ((b_ref[...] + up * w[0:1]) + u * w[1:2]) + un * w[2:3]

    x0c_ref[0] = sconv(x0_ref, w0_ref, b0_ref)
    z_ref[0] = sconv(v_ref, wv_ref, bv_ref) * sconv(x1_ref, w1_ref, b1_ref)


def _hy_pre(ux, conv_w, conv_b):
    b, l, _ = ux.shape
    nc = HY_WIDTH // LANES
    c0 = OFF_HY // LANES
    ub = lambda seg: pl.BlockSpec((1, l, LANES), lambda bi, j: (bi, 0, c0 + seg * nc + j))
    wb = lambda seg: pl.BlockSpec((3, LANES), lambda bi, j: (0, seg * nc + j))
    bb = lambda seg: pl.BlockSpec((1, LANES), lambda bi, j: (0, seg * nc + j))
    out = pl.BlockSpec((1, l, LANES), lambda bi, j: (bi, 0, j))
    shp = jax.ShapeDtypeStruct((b, l, HY_WIDTH), F32)
    cb = conv_b.reshape(1, 3 * HY_WIDTH)
    return pl.pallas_call(
        _hy_pre_body,
        grid=(b, nc),
        in_specs=[ub(0), ub(1), ub(2), wb(0), wb(1), wb(2), bb(0), bb(1), bb(2)],
        out_specs=[out, out],
        out_shape=[shp, shp],
        compiler_params=_params(("parallel", "parallel")),
        name="hy_pre",
    )(ux, ux, ux, conv_w, conv_w, conv_w, cb, cb, cb)


def _hy_features(l):
    n = jnp.arange(l, dtype=F32)
    pos = jnp.concatenate([n, l - n])
    t = pos / max(l - 1, 1)
    w = 2.0 * math.pi * pos / l
    f = jnp.linspace(1e-4, HY_BANDS - 1, HY_BANDS, dtype=F32)
    flag = jnp.ones((2 * l, 1), F32).at[l].set(0.0)
    feat = jnp.concatenate([t[:, None], jnp.cos(w[:, None] * f), -jnp.sin(w[:, None] * f), flag], axis=-1)
    return jnp.pad(feat, ((0, 0), (0, LANES - HY_EMB - 1)))


def _hy_filter_body(f_ref, w1_ref, b1_ref, w2_ref, b2_ref, w3_ref, fr_ref, dec_ref, k_ref, sum_ref):
    first = (pl.program_id(0) == 0) & (pl.program_id(1) == 0)

    @pl.when(first)
    def _():
        sum_ref[...] = jnp.zeros_like(sum_ref)

    f = f_ref[...]
    fr = fr_ref[...]
    h = jnp.sin(fr[0:1] * (_dot3(f, w1_ref[...]) + b1_ref[...]))
    h = jnp.sin(fr[1:2] * (_dot3(h, w2_ref[...]) + b2_ref[...]))
    h = _dot3(h, w3_ref[...]) * jnp.exp(-f[:, 0:1] * jnp.abs(dec_ref[...]))
    h = h * f[:, HY_EMB:HY_EMB + 1]
    k_ref[...] = h
    sum_ref[...] += jnp.sum(jnp.abs(h), axis=0, keepdims=True)


def _hy_filter(l, p):
    tl = min(l, 1024)
    nt = l // tl
    order = p['hy_w1'].shape[1]
    w1 = jnp.zeros((LANES, order), F32).at[:HY_EMB].set(p['hy_w1'])
    const = lambda d, i: (0, 0)
    return pl.pallas_call(
        _hy_filter_body,
        grid=(2, nt),
        in_specs=[pl.BlockSpec((tl, LANES), lambda d, i: (d * nt + i, 0)),
                  pl.BlockSpec((LANES, order), const),
                  pl.BlockSpec((1, order), const),
                  pl.BlockSpec((order, order), const),
                  pl.BlockSpec((1, order), const),
                  pl.BlockSpec((order, HY_WIDTH), lambda d, i: (0, d)),
                  pl.BlockSpec((2, order), const),
                  pl.BlockSpec((1, HY_WIDTH), lambda d, i: (0, d))],
        out_specs=[pl.BlockSpec((tl, HY_WIDTH), lambda d, i: (d * nt + i, 0)),
                   pl.BlockSpec((1, HY_WIDTH), const)],
        out_shape=[jax.ShapeDtypeStruct((2 * l, HY_WIDTH), F32), jax.ShapeDtypeStruct((1, HY_WIDTH), F32)],
        compiler_params=_params(("arbitrary", "arbitrary")),
        name="hy_filter",
    )(_hy_features(l), w1, p['hy_b1'].reshape(1, order), p['hy_w2'], p['hy_b2'].reshape(1, order),
      p['hy_w3'], p['hy_freq'], p['hy_decay'].reshape(1, 2 * HY_WIDTH))


def _fft_consts(n_la):
    ka = np.arange(FFT_NA)[:, None]
    la = np.arange(n_la)[None, :]
    ang = 2.0 * math.pi * ka * la / FFT_NA
    f1 = np.concatenate([np.cos(ang), -np.sin(ang)], axis=0)
    lb = np.arange(FFT_NB)[None, :]
    th = 2.0 * math.pi * ka * lb / FFT_N
    tc, ts = np.cos(th), np.sin(th)
    a2 = 2.0 * math.pi * np.arange(FFT_NB)[:, None] * np.arange(FFT_NB)[None, :] / FFT_NB
    gre, gim = np.cos(a2), -np.sin(a2)
    g2 = np.block([[gre, gim], [-gim, gre]])
    g2i = np.block([[gre, -gim], [gim, gre]])
    lo = np.arange(FFT_NA // 2)[:, None]
    ph = 2.0 * math.pi * lo * np.arange(FFT_NA)[None, :] / FFT_NA
    f1i = np.concatenate([np.cos(ph), -np.sin(ph)], axis=1) / FFT_N
    return f1, tc, ts, g2, g2i, f1i


def _fft_fwd(x_ref, f1_ref, tc_ref, ts_ref, g2_ref, p2_ref, cc, dot):
    rhs = jnp.concatenate([x_ref[0, c] for c in range(cc)], axis=1)
    a = dot(f1_ref[...].astype(p2_ref.dtype), rhs.astype(p2_ref.dtype))
    tc = tc_ref[...]
    ts = ts_ref[...]
    for c in range(cc):
        are = a[0:FFT_NA, c * FFT_NB:(c + 1) * FFT_NB]
        aim = a[FFT_NA:2 * FFT_NA, c * FFT_NB:(c + 1) * FFT_NB]
        p2_ref[c * FFT_NA:(c + 1) * FFT_NA, 0:FFT_NB] = (are * tc + aim * ts).astype(p2_ref.dtype)
        p2_ref[c * FFT_NA:(c + 1) * FFT_NA, FFT_NB:2 * FFT_NB] = (aim * tc - are * ts).astype(p2_ref.dtype)
    return dot(p2_ref[...], g2_ref[...].astype(p2_ref.dtype))


def _fft_spec_body(x_ref, f1_ref, tc_ref, ts_ref, g2_ref, o_ref, p2_ref, *, cc):
    o_ref[...] = _fft_fwd(x_ref, f1_ref, tc_ref, ts_ref, g2_ref, p2_ref, cc, _dot3)


def _fft_conv_body(x_ref, kf_ref, f1_ref, tc_ref, ts_ref, g2_ref, g2i_ref, f1i_ref, o_ref, p2_ref, r2_ref, *, cc):
    z = _fft_fwd(x_ref, f1_ref, tc_ref, ts_ref, g2_ref, p2_ref, cc, _dot)
    kf = kf_ref[...]
    zre, zim = z[:, :FFT_NB], z[:, FFT_NB:]
    kre, kim = kf[:, :FFT_NB], kf[:, FFT_NB:]
    y = jnp.concatenate([zre * kre - zim * kim, zre * kim + zim * kre], axis=1).astype(BF16)
    u = _dot(y, g2i_ref[...].astype(BF16))
    tc = tc_ref[...]
    ts = ts_ref[...]
    for c in range(cc):
        ure = u[c * FFT_NA:(c + 1) * FFT_NA, 0:FFT_NB]
        uim = u[c * FFT_NA:(c + 1) * FFT_NA, FFT_NB:2 * FFT_NB]
        r2_ref[0:FFT_NA, c * FFT_NB:(c + 1) * FFT_NB] = (ure * tc - uim * ts).astype(BF16)
        r2_ref[FFT_NA:2 * FFT_NA, c * FFT_NB:(c + 1) * FFT_NB] = (ure * ts + uim * tc).astype(BF16)
    out = _dot(f1i_ref[...].astype(BF16), r2_ref[...])
    for c in range(cc):
        o_ref[0, c] = out[:, c * FFT_NB:(c + 1) * FFT_NB]


FFT_CC = 32


def _fft_spectrum(kt):
    nch, n_la, _ = kt.shape
    cc = FFT_CC
    f1, tc, ts, g2, _, _ = _fft_consts(n_la)
    const = lambda j: (0, 0)
    return pl.pallas_call(
        functools.partial(_fft_spec_body, cc=cc),
        grid=(nch // cc,),
        in_specs=[pl.BlockSpec((1, cc, n_la, FFT_NB), lambda j: (0, j, 0, 0)),
                  pl.BlockSpec((2 * FFT_NA, n_la), const),
                  pl.BlockSpec((FFT_NA, FFT_NB), const),
                  pl.BlockSpec((FFT_NA, FFT_NB), const),
                  pl.BlockSpec((2 * FFT_NB, 2 * FFT_NB), const)],
        out_specs=pl.BlockSpec((cc * FFT_NA, 2 * FFT_NB), lambda j: (j, 0)),
        out_shape=jax.ShapeDtypeStruct((nch * FFT_NA, 2 * FFT_NB), F32),
        scratch_shapes=[pltpu.VMEM((cc * FFT_NA, 2 * FFT_NB), F32)],
        compiler_params=_params(("parallel",)),
        name="fft_spectrum",
    )(kt[None], jnp.asarray(f1, F32), jnp.asarray(tc, F32), jnp.asarray(ts, F32), jnp.asarray(g2, F32))


def _fft_conv(zt, kf):
    b, nch, n_la, _ = zt.shape
    cc = FFT_CC
    f1, tc, ts, g2, g2i, f1i = _fft_consts(n_la)
    const = lambda j, bi: (0, 0)
    return pl.pallas_call(
        functools.partial(_fft_conv_body, cc=cc),
        grid=(nch // cc, b),
        in_specs=[pl.BlockSpec((1, cc, n_la, FFT_NB), lambda j, bi: (bi, j, 0, 0)),
                  pl.BlockSpec((cc * FFT_NA, 2 * FFT_NB), lambda j, bi: (j, 0)),
                  pl.BlockSpec((2 * FFT_NA, n_la), const),
                  pl.BlockSpec((FFT_NA, FFT_NB), const),
                  pl.BlockSpec((FFT_NA, FFT_NB), const),
                  pl.BlockSpec((2 * FFT_NB, 2 * FFT_NB), const),
                  pl.BlockSpec((2 * FFT_NB, 2 * FFT_NB), const),
                  pl.BlockSpec((FFT_NA // 2, 2 * FFT_NA), const)],
        out_specs=pl.BlockSpec((1, cc, n_la, FFT_NB), lambda j, bi: (bi, j, 0, 0)),
        out_shape=jax.ShapeDtypeStruct((b, nch, n_la, FFT_NB), F32),
        scratch_shapes=[pltpu.VMEM((cc * FFT_NA, 2 * FFT_NB), BF16),
                        pltpu.VMEM((2 * FFT_NA, cc * FFT_NB), BF16)],
        compiler_params=_params(("parallel", "parallel")),
        name="fft_conv",
    )(zt, kf, jnp.asarray(f1, F32), jnp.asarray(tc, F32), jnp.asarray(ts, F32),
      jnp.asarray(g2, F32), jnp.asarray(g2i, F32), jnp.asarray(f1i, F32))


def _dft_conv_body(z_ref, k_ref, fw_ref, fi_ref, o_ref):
    l = z_ref.shape[1]
    fw = fw_ref[...]
    n = fw.shape[1]
    kf = _dot3(fw, k_ref[...])
    zf = _dot3(fw[:, :l], z_ref[0])
    kre, kim = kf[:n], kf[n:]
    zre, zim = zf[:n], zf[n:]
    y = jnp.concatenate([zre * kre - zim * kim, zre * kim + zim * kre], axis=0)
    o_ref[0] = _dot3(fi_ref[...], y)


def _dft_conv(z, k):
    b, l, w = z.shape
    n = 2 * l
    ang = 2.0 * math.pi * np.arange(n)[:, None] * np.arange(n)[None, :] / n
    fw = np.concatenate([np.cos(ang), -np.sin(ang)], axis=0)
    ai = ang[:l]
    fi = np.concatenate([np.cos(ai), -np.sin(ai)], axis=1) / n
    return pl.pallas_call(
        _dft_conv_body,
        grid=(b,),
        in_specs=[pl.BlockSpec((1, l, w), lambda bi: (bi, 0, 0)),
                  pl.BlockSpec((n, w), lambda bi: (0, 0)),
                  pl.BlockSpec((2 * n, n), lambda bi: (0, 0)),
                  pl.BlockSpec((l, 2 * n), lambda bi: (0, 0))],
        out_specs=pl.BlockSpec((1, l, w), lambda bi: (bi, 0, 0)),
        out_shape=jax.ShapeDtypeStruct((b, l, w), F32),
        compiler_params=_params(("parallel",)),
        name="dft_conv",
    )(z, k, jnp.asarray(fw, F32), jnp.asarray(fi, F32))


def _hy_post_body(y_ref, z_ref, x0_ref, ks_ref, sk_ref, o_ref):
    y = y_ref[0] / (ks_ref[...] + EPS) + sk_ref[...] * z_ref[0]
    o_ref[0] = (y * x0_ref[0]).astype(o_ref.dtype)


def _hy_post(yraw, z, x0c, ksum, skip):
    b, l, w = z.shape
    ts = min(l, 1024)
    blk = pl.BlockSpec((1, ts, w), lambda bi, i: (bi, i, 0))
    vec = pl.BlockSpec((1, w), lambda bi, i: (0, 0))
    return pl.pallas_call(
        _hy_post_body,
        grid=(b, l // ts),
        in_specs=[blk, blk, blk, vec, vec],
        out_specs=blk,
        out_shape=jax.ShapeDtypeStruct((b, l, w), BF16),
        compiler_params=_params(("parallel", "parallel")),
        name="hy_post",
    )(yraw, z, x0c, ksum, skip.reshape(1, w))


def _hyena(ux, p):
    b, l, _ = ux.shape
    z, x0c = _hy_pre(ux, p['hy_conv_w'], p['hy_conv_b'])
    k_raw, ksum = _hy_filter(l, p)
    if 2 * l == FFT_N:
        kt = k_raw.T.reshape(HY_WIDTH, FFT_NA, FFT_NB)
        kf = _fft_spectrum(kt)
        zt = jnp.swapaxes(z, 1, 2).reshape(b, HY_WIDTH, l // FFT_NB, FFT_NB)
        yt = _fft_conv(zt, kf)
        yraw = jnp.swapaxes(yt.reshape(b, HY_WIDTH, l), 1, 2)
    else:
        yraw = _dft_conv(z, k_raw)
    return _hy_post(yraw, z, x0c, ksum, p['hy_skip'])


def _pool_body(u_ref, w_ref, sc_ref, o_ref, *, win):
    u = u_ref[0]
    n = u.shape[0]
    lo = win // 2
    hi = win - 1 - lo
    row = lax.broadcasted_iota(jnp.int32, (n, 1), 0)
    acc = u
    for j in range(1, lo + 1):
        acc = acc + jnp.where(row >= j, pltpu.roll(u, j, 0), 0.0)
    for j in range(1, hi + 1):
        acc = acc + jnp.where(row < n - j, pltpu.roll(u, n - j, 0), 0.0)
    cnt = (jnp.minimum(row + hi + 1, n) - jnp.maximum(row - lo, 0)).astype(F32)
    d = acc / cnt - u
    o_ref[0] = (_dot(d.astype(BF16), w_ref[0].astype(BF16)) * sc_ref[...]).astype(o_ref.dtype)


def _pool_mix(ux, w_grp, scale):
    b, l, _ = ux.shape
    c0 = OFF_PL // LANES
    outs = []
    for gi, win in enumerate(POOL_WINDOWS):
        outs.append(pl.pallas_call(
            functools.partial(_pool_body, win=win),
            grid=(b,),
            in_specs=[pl.BlockSpec((1, l, LANES), lambda bi, gi=gi: (bi, 0, c0 + gi)),
                      pl.BlockSpec((1, LANES, LANES), lambda bi, gi=gi: (gi, 0, 0)),
                      pl.BlockSpec((1, LANES), lambda bi, gi=gi: (0, gi))],
            out_specs=pl.BlockSpec((1, l, LANES), lambda bi: (bi, 0, 0)),
            out_shape=jax.ShapeDtypeStruct((b, l, LANES), BF16),
            compiler_params=_params(("parallel",)),
            name=f"pool_mix{win}",
        )(ux, w_grp, scale.reshape(1, -1)))
    return jnp.concatenate(outs, axis=-1)


def _merge1_body(hx_ref, y0, y1, y2, y3, g0, g1, g2, g3, b0, b1, b2, b3, o_ref):
    hx = hx_ref[...]
    acc = None
    for y_ref, g_ref, b_ref in ((y0, g0, b0), (y1, g1, b1), (y2, g2, b2), (y3, g3, b3)):
        t = jax.nn.sigmoid(_dot(hx, g_ref[...])) * _dot(y_ref[...], b_ref[0, 0])
        acc = t if acc is None else acc + t
    o_ref[...] = acc.astype(o_ref.dtype)


def _merge1(hx, ys, w_in, gate_off, w_branch, layer, tm=1024, tn=256):
    m, d = hx.shape
    tm = min(tm, m)
    nj = d // tn
    g0 = gate_off // tn
    row = lambda i, j: (i, 0)
    gspec = lambda br: pl.BlockSpec((d, tn), lambda i, j, br=br: (0, g0 + br * nj + j))
    bspec = lambda br: pl.BlockSpec((1, 1, BRANCH_W, tn), lambda i, j, br=br: (layer, br, 0, j))
    return pl.pallas_call(
        _merge1_body,
        grid=(m // tm, nj),
        in_specs=([pl.BlockSpec((tm, d), row)] + [pl.BlockSpec((tm, BRANCH_W), row)] * N_BRANCH
                  + [gspec(br) for br in range(N_BRANCH)] + [bspec(br) for br in range(N_BRANCH)]),
        out_specs=pl.BlockSpec((tm, tn), lambda i, j: (i, j)),
        out_shape=jax.ShapeDtypeStruct((m, d), BF16),
        compiler_params=_params(("parallel", "parallel")),
        name="merge_gate",
    )(hx, *ys, w_in, w_in, w_in, w_in, w_branch, w_branch, w_branch, w_branch)


def _merge2_body(a_ref, w_ref, x_ref, g_ref, n2_ref, sc_ref, sh_ref, wr_ref, br_ref, o_ref, h_ref, e_ref, ew_ref):
    x = x_ref[0] + g_ref[0] * _dot(a_ref[0], w_ref[0])
    o_ref[0] = x
    _route_rows(x, n2_ref, sc_ref, sh_ref, wr_ref, br_ref, h_ref, e_ref, ew_ref)


def _merge2(acc, w_out, layer, x, gate, norm2, scale2, shift2, w_router, b_router, tm=512):
    b, l, d = x.shape
    tm = min(tm, l)
    nl = l // tm
    blk = pl.BlockSpec((1, tm, d), lambda bi, i: (bi, i, 0))
    mod = pl.BlockSpec((1, 1, d), lambda bi, i: (bi, 0, 0))
    sm = pl.BlockSpec((1, tm, LANES), lambda bi, i: (bi, i, 0))
    const = lambda bi, i: (0, 0)
    return pl.pallas_call(
        _merge2_body,
        grid=(b, nl),
        in_specs=[blk, pl.BlockSpec((1, d, d), lambda bi, i: (layer, 0, 0)), blk, mod,
                  pl.BlockSpec((1, d), const), mod, mod,
                  pl.BlockSpec((d, LANES), const), pl.BlockSpec((1, LANES), const)],
        out_specs=[blk, pl.BlockSpec((tm * (d // LANES), LANES), lambda bi, i: (bi * nl + i, 0)), sm, sm],
        out_shape=[jax.ShapeDtypeStruct((b, l, d), F32),
                   jax.ShapeDtypeStruct((b * l * (d // LANES), LANES), F32),
                   jax.ShapeDtypeStruct((b, l, LANES), jnp.int32),
                   jax.ShapeDtypeStruct((b, l, LANES), F32)],
        compiler_params=_params(("parallel", "parallel"), 56),
        name="merge_out",
    )(acc, w_out, x, gate, norm2.reshape(1, d), scale2, shift2, w_router, b_router)


def _store_token_major(ref, x):
    n, d = x.shape
    pitch = d // LANES
    for j in range(pitch):
        ref[pl.ds(j, n, stride=pitch), :] = x[:, j * LANES:(j + 1) * LANES]


def _load_token_major(ref, n):
    pitch = ref.shape[0] // n
    return jnp.concatenate([ref[pl.ds(j, n, stride=pitch), :] for j in range(pitch)], axis=1)


def _route_rows(x, g_ref, sc_ref, sh_ref, wr_ref, br_ref, h_ref, e_ref, w_ref):
    ms = jnp.mean(x * x, axis=-1, keepdims=True)
    h = (x * lax.rsqrt(ms + EPS) * g_ref[...]) * (1.0 + sc_ref[0]) + sh_ref[0]
    _store_token_major(h_ref, h)
    logits = _dot3(h, wr_ref[...]) + br_ref[...]
    lane = lax.broadcasted_iota(jnp.int32, logits.shape, 1)
    big = jnp.int32(LANES)
    gl = jnp.where(lane < N_GROUPS, logits, -jnp.inf)
    ge = jnp.exp(gl - jnp.max(gl, axis=-1, keepdims=True))
    pg = ge / jnp.sum(ge, axis=-1, keepdims=True)
    pg_top = jnp.max(pg, axis=-1, keepdims=True)
    g_idx = jnp.min(jnp.where((pg == pg_top) & (lane < N_GROUPS), lane, big), axis=-1, keepdims=True)
    first = N_GROUPS + EXPERTS_PER_GROUP * g_idx
    le = jnp.where((lane >= first) & (lane < first + EXPERTS_PER_GROUP), logits, -jnp.inf)
    v1 = jnp.max(le, axis=-1, keepdims=True)
    i1 = jnp.min(jnp.where(le == v1, lane, big), axis=-1, keepdims=True)
    le2 = jnp.where(lane == i1, -jnp.inf, le)
    v2 = jnp.max(le2, axis=-1, keepdims=True)
    i2 = jnp.min(jnp.where(le2 == v2, lane, big), axis=-1, keepdims=True)
    e2 = jnp.exp(v2 - v1)
    den = 1.0 + e2
    e_ref[0] = jnp.where(lane == 0, i1 - N_GROUPS, jnp.where(lane == 1, i2 - N_GROUPS, 0))
    w_ref[0] = jnp.where(lane == 0, (1.0 / den) * pg_top, jnp.where(lane == 1, (e2 / den) * pg_top, 0.0))


MOE_GATHER_CHUNK = 32


def _ffn_up_body(blk_e_ref, nused_ref, nval_ref, first_ref, gi_ref, gn_ref, w1_ref, w3_ref, h_hbm, o_ref,
                 xbuf, w1b, w3b, sem, *, pitch):
    i = pl.program_id(0)
    n = pl.num_programs(0)
    tm = o_ref.shape[0]
    slot = i % 2
    chunk = MOE_GATHER_CHUNK

    def gather(idx_ref, s, nv):
        for c in range(tm // chunk):
            @pl.when(c * chunk < nv)
            def _():
                for r in range(c * chunk, (c + 1) * chunk):
                    pltpu.make_async_copy(h_hbm.at[pl.ds(idx_ref[0, 0, r], pitch)],
                                          xbuf.at[s, pl.ds(r * pitch, pitch)], sem.at[s]).start(priority=r % 2)

    @pl.when(i == 0)
    def _():
        xbuf[...] = jnp.zeros_like(xbuf)
        gather(gi_ref, 0, nval_ref[0])

    @pl.when(i + 1 < n)
    def _():
        gather(gn_ref, 1 - slot, nval_ref[jnp.minimum(i + 1, n - 1)])

    for c in range(tm // chunk):
        @pl.when(c * chunk < nval_ref[i])
        def _():
            pltpu.make_async_copy(h_hbm.at[pl.ds(0, chunk * pitch)],
                                  xbuf.at[slot, pl.ds(c * chunk * pitch, chunk * pitch)], sem.at[slot]).wait()

    @pl.when(first_ref[i] == 1)
    def _():
        w1b[...] = w1_ref[0, 0].astype(BF16)
        w3b[...] = w3_ref[0, 0].astype(BF16)

    @pl.when(i < nused_ref[0])
    def _():
        x = _load_token_major(xbuf.at[slot], tm).astype(BF16)
        a = _dot(x, w1b[...])
        o_ref[...] = ((a * jax.nn.sigmoid(a)) * _dot(x, w3b[...])).astype(o_ref.dtype)

    @pl.when(i >= nused_ref[0])
    def _():
        o_ref[...] = jnp.zeros_like(o_ref)


def _ffn_down_body(blk_e_ref, nused_ref, first_ref, h_ref, si_ref, sw_ref, w2_ref, y_hbm, obuf, w2b, ssem, *, pitch):
    i = pl.program_id(0)
    n = pl.num_programs(0)
    tm = sw_ref.shape[0]
    slot = i % 2

    def wait_scatter(s):
        pltpu.make_async_copy(obuf.at[s], y_hbm.at[pl.ds(0, tm * pitch)], ssem.at[s]).wait()

    @pl.when(i >= 2)
    def _():
        wait_scatter(slot)

    @pl.when(first_ref[i] == 1)
    def _():
        w2b[...] = w2_ref[0, 0].astype(BF16)

    @pl.when(i < nused_ref[0])
    def _():
        _store_token_major(obuf.at[slot], _dot(h_ref[...], w2b[...]) * sw_ref[...])

    @pl.when(i >= nused_ref[0])
    def _():
        obuf[slot] = jnp.zeros(obuf.shape[1:], F32)

    for r in range(tm):
        pltpu.make_async_copy(obuf.at[slot, pl.ds(r * pitch, pitch)], y_hbm.at[pl.ds(si_ref[0, 0, r], pitch)],
                              ssem.at[slot]).start(priority=r % 2)

    @pl.when(i == n - 1)
    def _():
        wait_scatter(slot)

        @pl.when(n >= 2)
        def _():
            wait_scatter(1 - slot)


def _moe_experts(h2t, blk_e, nused, nval, gidx, sidx, slot_w, w1, w3, w2, layer):
    nblk, tm = gidx.shape
    d = w1.shape[2]
    ff = w1.shape[3]
    pitch = d // LANES
    rows = tm * pitch
    smem_blk = lambda f: pl.BlockSpec((1, 1, tm), f, memory_space=pltpu.SMEM)
    first = jnp.concatenate([jnp.ones((1,), jnp.int32), (blk_e[1:] != blk_e[:-1]).astype(jnp.int32)])
    gidx3 = (gidx * pitch).reshape(nblk, 1, tm)
    sidx3 = (sidx * pitch).reshape(nblk, 1, tm)
    up_spec = pltpu.PrefetchScalarGridSpec(
        num_scalar_prefetch=4,
        grid=(nblk,),
        in_specs=[smem_blk(lambda i, be, nu, nv, fi: (i, 0, 0)),
                  smem_blk(lambda i, be, nu, nv, fi: (jnp.minimum(i + 1, nblk - 1), 0, 0)),
                  pl.BlockSpec((1, 1, d, ff), lambda i, be, nu, nv, fi: (layer, be[i], 0, 0)),
                  pl.BlockSpec((1, 1, d, ff), lambda i, be, nu, nv, fi: (layer, be[i], 0, 0)),
                  pl.BlockSpec(memory_space=pl.ANY)],
        out_specs=pl.BlockSpec((tm, ff), lambda i, be, nu, nv, fi: (i, 0)),
        scratch_shapes=[pltpu.VMEM((2, rows, LANES), F32), pltpu.VMEM((d, ff), BF16), pltpu.VMEM((d, ff), BF16),
                        pltpu.SemaphoreType.DMA((2,))],
    )
    hb = pl.pallas_call(
        functools.partial(_ffn_up_body, pitch=pitch),
        grid_spec=up_spec,
        out_shape=jax.ShapeDtypeStruct((nblk * tm, ff), BF16),
        compiler_params=_params(("arbitrary",), 56),
        name="moe_ffn_up",
    )(blk_e, nused, nval, first, gidx3, gidx3, w1, w3, h2t)
    down_spec = pltpu.PrefetchScalarGridSpec(
        num_scalar_prefetch=3,
        grid=(nblk,),
        in_specs=[pl.BlockSpec((tm, ff), lambda i, be, nu, fi: (i, 0)),
                  smem_blk(lambda i, be, nu, fi: (i, 0, 0)),
                  pl.BlockSpec((tm, 1), lambda i, be, nu, fi: (i, 0)),
                  pl.BlockSpec((1, 1, ff, d), lambda i, be, nu, fi: (layer, be[i], 0, 0))],
        out_specs=pl.BlockSpec(memory_space=pl.ANY),
        scratch_shapes=[pltpu.VMEM((2, rows, LANES), F32), pltpu.VMEM((ff, d), BF16), pltpu.SemaphoreType.DMA((2,))],
    )
    return pl.pallas_call(
        functools.partial(_ffn_down_body, pitch=pitch),
        grid_spec=down_spec,
        out_shape=jax.ShapeDtypeStruct((nblk * rows, LANES), F32),
        compiler_params=_params(("arbitrary",), 48),
        name="moe_ffn_down",
    )(blk_e, nused, first, hb, sidx3, slot_w, w2)


def _moe_plan(e_idx, e_w, tm):
    n = e_idx.shape[0]
    m = n * TOP_K
    e_flat = e_idx.reshape(m)
    w_flat = e_w.reshape(m)
    order = jnp.argsort(e_flat).astype(jnp.int32)
    experts = jnp.arange(N_EXPERTS, dtype=jnp.int32)
    counts = jnp.sum((e_flat[:, None] == experts[None, :]).astype(jnp.int32), axis=0)
    padded = (counts + tm - 1) // tm * tm
    pad_end = jnp.cumsum(padded)
    pad_start = pad_end - padded
    cnt_start = jnp.cumsum(counts) - counts
    nblk = -(-(m + N_EXPERTS * (tm - 1)) // tm)
    blk_first = jnp.arange(nblk, dtype=jnp.int32) * tm
    blk_e = jnp.minimum(jnp.sum((pad_end[None, :] <= blk_first[:, None]).astype(jnp.int32), axis=1), N_EXPERTS - 1)
    rank = blk_first[:, None] + jnp.arange(tm, dtype=jnp.int32)[None, :] - pad_start[blk_e][:, None]
    valid = rank < counts[blk_e][:, None]
    src = order[jnp.clip(cnt_start[blk_e][:, None] + rank, 0, m - 1)]
    tok = jnp.where(valid, src // TOP_K, 0)
    slot_w = jnp.where(valid, w_flat[src], 0.0)
    pad_rank = (jnp.cumsum(jnp.logical_not(valid).reshape(-1).astype(jnp.int32)) - 1).reshape(nblk, tm)
    dst = jnp.where(valid, (src % TOP_K) * n + src // TOP_K, m + pad_rank)
    nused = (pad_end[-1] // tm).astype(jnp.int32).reshape(1)
    nval = jnp.sum(valid.astype(jnp.int32), axis=1)
    return (blk_e, nused, nval, tok.astype(jnp.int32), dst.astype(jnp.int32),
            slot_w.reshape(nblk * tm, 1).astype(F32))


def _combine_body(x_ref, y0_ref, y1_ref, g_ref, o_ref):
    tm = x_ref.shape[1]
    o_ref[0] = x_ref[0] + g_ref[0] * (_load_token_major(y0_ref, tm) + _load_token_major(y1_ref, tm))


def _moe_combine(x, yk, gate, row_off, n_tok, tm=256):
    b, l, d = x.shape
    tm = min(tm, l)
    assert l % tm == 0 and row_off % tm == 0 and n_tok % tm == 0
    nl = l // tm
    rows = tm * (d // LANES)
    blk = pl.BlockSpec((1, tm, d), lambda bi, i: (bi, i, 0))
    o0 = row_off // tm
    o1 = (n_tok + row_off) // tm
    return pl.pallas_call(
        _combine_body,
        grid=(b, nl),
        in_specs=[blk,
                  pl.BlockSpec((rows, LANES), lambda bi, i: (o0 + bi * nl + i, 0)),
                  pl.BlockSpec((rows, LANES), lambda bi, i: (o1 + bi * nl + i, 0)),
                  pl.BlockSpec((1, 1, d), lambda bi, i: (bi, 0, 0))],
        out_specs=blk,
        out_shape=jax.ShapeDtypeStruct((b, l, d), F32),
        compiler_params=_params(("parallel", "parallel")),
        name="moe_combine",
    )(x, yk, yk, gate)


def _mixers(ux, ucx, p, lam_init, latent):
    b, l, _ = ux.shape
    rope_q = _rope_tables(l, latent)
    no_rope = _rope_tables(ucx.shape[1], False)
    scale = HEAD_DIM ** -0.5
    qw = _qk_prep(ux, OFF_WQ, 512, p['win_qnorm'], rope_q, scale, pad_heads=True)
    qd = _qk_prep(ux, OFF_DQ, 512, p['diff_qnorm'], rope_q, scale * math.log2(math.e))
    kcw = _qk_prep(ucx, OFF_WK, 128, p['win_knorm'], no_rope, 1.0)
    vcw, vcws = _cast_cols(ucx, OFF_WV, 128, swapped=True)
    kcd = _qk_prep(ucx, OFF_DK, 512, p['diff_knorm'], no_rope, 1.0)
    vcd = _cast_cols(ucx, OFF_DV, 512)
    if latent:
        kw = _qk_prep(ux, OFF_WK, 128, p['win_knorm'], rope_q, 1.0)
        vw, vws = _cast_cols(ux, OFF_WV, 128, swapped=True)
        kd = _qk_prep(ux, OFF_DK, 512, p['diff_knorm'], rope_q, 1.0)
        vd = _cast_cols(ux, OFF_DV, 512)
        y_win = _win_attn(qw, kw, vw, vws, kcw, vcw, vcws, p['win_sink'])
        k_all = jnp.concatenate([kd, kcd], axis=1)
        v_all = jnp.concatenate([vd, vcd], axis=1)
    else:
        y_win = _ctx_win_attn(qw, kcw, vcw, vcws, p['win_sink'])
        k_all, v_all = kcd, vcd
    y_diff = _diff_attn(qd, k_all, v_all, p['diff_lambda'], p['diff_subln'], lam_init, 512)
    y_hy = _hyena(ux, p)
    y_pool = _pool_mix(ux, p['pool_proj'], p['pool_scale'])
    return y_win, y_diff, y_hy, y_pool


def _layer(x, ctx, mod, p, layer_idx, update_ctx):
    b, s, d = x.shape
    c = ctx.shape[1]
    lam_init = 0.8 - 0.6 * math.exp(-0.3 * layer_idx)
    chunks = [mod[:, i * d:(i + 1) * d] for i in range(6)]
    sh1, sc1, g1, sh2, sc2, g2 = [t[:b].reshape(b, 1, d) for t in chunks]
    csh1, csc1, cg1, csh2, csc2, cg2 = [jnp.broadcast_to(t[b:b + 1].reshape(1, 1, d), (b, 1, d)) for t in chunks]

    w_in = _cast_weight(p['w_in'], layer_idx, 7 * LANES)
    w_branch = p['w_branch']
    w_out = p['w_out']
    tn_in = OFF_GT // 2

    hx = _prenorm(x, p['norm1'], sc1, sh1)
    hc = _prenorm(ctx, p['norm1'], csc1, csh1)
    ux = _matmul(hx.reshape(b * s, d), w_in, OFF_GT, 512, tn_in).reshape(b, s, OFF_GT)
    ucx = _matmul(hc.reshape(b * c, d), w_in, OFF_GT, 512, tn_in).reshape(b, c, OFF_GT)

    w_router = jnp.zeros((d, LANES), F32).at[:, :N_GROUPS].set(p['r_group_w'])
    w_router = w_router.at[:, N_GROUPS:N_GROUPS + N_EXPERTS].set(p['r_expert_w'])
    b_router = jnp.zeros((1, LANES), F32).at[0, :N_GROUPS].set(p['r_group_b'])
    b_router = b_router.at[0, N_GROUPS:N_GROUPS + N_EXPERTS].set(p['r_expert_b'])

    ys = _mixers(ux, ucx, p, lam_init, True)
    acc = _merge1(hx.reshape(b * s, d), [y.reshape(b * s, BRANCH_W) for y in ys], w_in, OFF_GT, w_branch, layer_idx)
    x, h2, e_idx, e_w = _merge2(acc.reshape(b, s, d), w_out, layer_idx, x, g1, p['norm2'], sc2, sh2, w_router, b_router)
    e_idx = e_idx.reshape(b * s, LANES)[:, :TOP_K]
    e_w = e_w.reshape(b * s, LANES)[:, :TOP_K]
    if update_ctx:
        ycs = _mixers(ucx, ucx, p, lam_init, False)
        acc_c = _merge1(hc.reshape(b * c, d), [y.reshape(b * c, BRANCH_W) for y in ycs], w_in, OFF_GT, w_branch, layer_idx)
        ctx, h2c, ec_idx, ec_w = _merge2(acc_c.reshape(b, c, d), w_out, layer_idx, ctx, cg1, p['norm2'], csc2, csh2,
                                         w_router, b_router)
        h2 = jnp.concatenate([h2, h2c], axis=0)
        e_idx = jnp.concatenate([e_idx, ec_idx.reshape(b * c, LANES)[:, :TOP_K]], axis=0)
        e_w = jnp.concatenate([e_w, ec_w.reshape(b * c, LANES)[:, :TOP_K]], axis=0)
    n_tok = e_idx.shape[0]
    blk_e, nused, nval, gidx, sidx, slot_w = _moe_plan(e_idx, e_w, MOE_TM)
    yk = _moe_experts(h2, blk_e, nused, nval, gidx, sidx, slot_w, p['e_w1'], p['e_w3'], p['e_w2'], layer_idx)
    x = _moe_combine(x, yk, g2, 0, n_tok)
    if update_ctx:
        ctx = _moe_combine(ctx, yk, cg2, b * s, n_tok)
    return x, ctx


def kernel(x, c, ctx, c_ctx, w_mod, b_mod, norm1, norm2, w_in, win_sink, win_qnorm, win_knorm, diff_qnorm, diff_knorm, diff_lambda, diff_subln, hy_conv_w, hy_conv_b, hy_w1, hy_b1, hy_w2, hy_b2, hy_w3, hy_freq, hy_decay, hy_skip, pool_proj, pool_scale, w_branch, w_out, r_group_w, r_group_b, r_expert_w, r_expert_b, e_w1, e_w3, e_w2):
    depth = w_mod.shape[0]
    b, _, d = x.shape
    cs = jnp.zeros((8, d), F32).at[:b].set(c).at[b].set(c_ctx)
    stacked = dict(w_in=w_in, w_branch=w_branch.astype(BF16), w_out=w_out.astype(BF16),
                   e_w1=e_w1, e_w3=e_w3, e_w2=e_w2)
    for l in range(depth):
        p = dict(norm1=norm1[l], norm2=norm2[l],
                 win_sink=win_sink[l], win_qnorm=win_qnorm[l], win_knorm=win_knorm[l],
                 diff_qnorm=diff_qnorm[l], diff_knorm=diff_knorm[l], diff_lambda=diff_lambda[l],
                 diff_subln=diff_subln[l], hy_conv_w=hy_conv_w[l], hy_conv_b=hy_conv_b[l],
                 hy_w1=hy_w1[l], hy_b1=hy_b1[l], hy_w2=hy_w2[l], hy_b2=hy_b2[l], hy_w3=hy_w3[l],
                 hy_freq=hy_freq[l], hy_decay=hy_decay[l], hy_skip=hy_skip[l],
                 pool_proj=pool_proj[l], pool_scale=pool_scale[l],
                 r_group_w=r_group_w[l], r_group_b=r_group_b[l], r_expert_w=r_expert_w[l],
                 r_expert_b=r_expert_b[l], **stacked)
        mod = _modulation(cs, w_mod, b_mod, l)
        x, ctx = _layer(x, ctx, mod, p, l, l < depth - 1)
    return x
```

```python
import functools
import math

import jax
import jax.numpy as jnp
import numpy as np
from jax import lax
from jax.experimental import pallas as pl
from jax.experimental.pallas import tpu as pltpu

F32 = jnp.float32
BF16 = jnp.bfloat16

GRID_W = 64
HEAD_DIM = 64
ROPE_THETA = 10000.0
EPS = 1e-6
NEG_INF = -1e30
BLOCK = 128
WINDOW = 128
LANES = 128

WIN_HEADS = 8
WIN_KV_HEADS = 2
WIN_GROUP = WIN_HEADS // WIN_KV_HEADS
DIFF_HEADS = 4
HY_WIDTH = 512
HY_EMB = 33
HY_BANDS = 16
POOL_WINDOWS = (2, 4, 8, 16)
N_BRANCH = 4
BRANCH_W = 512

OFF_WQ = 0
OFF_WK = 512
OFF_WV = 640
OFF_DQ = 768
OFF_DK = 1280
OFF_DV = 1792
OFF_HY = 2304
OFF_PL = 3840
OFF_GT = 4352

N_GROUPS = 4
EXPERTS_PER_GROUP = 8
N_EXPERTS = 32
TOP_K = 2
MOE_TM = 256

FFT_NA = 64
FFT_NB = 128
FFT_N = FFT_NA * FFT_NB


def _params(sem, mib=48):
    return pltpu.CompilerParams(dimension_semantics=sem, vmem_limit_bytes=mib * 2**20)


def _dot(a, b):
    return jnp.dot(a, b, preferred_element_type=F32)


def _split(a):
    hi = a.astype(BF16)
    lo = (a - hi.astype(F32)).astype(BF16)
    return hi, lo


def _dot3(a, b):
    ah, al = _split(a)
    bh, bl = _split(b)
    return _dot(ah, bh) + (_dot(ah, bl) + _dot(al, bh))


def _mod_body(c_ref, w_ref, b_ref, o_ref):
    c = c_ref[...]
    a = (c * jax.nn.sigmoid(c)).astype(BF16)
    o_ref[...] = _dot(a, w_ref[0].astype(BF16)) + b_ref[0]


def _modulation(cs, w_mod, b_mod, layer):
    depth, d, n = w_mod.shape
    tn = 1024
    return pl.pallas_call(
        _mod_body,
        grid=(n // tn,),
        in_specs=[pl.BlockSpec((8, d), lambda j: (0, 0)),
                  pl.BlockSpec((1, d, tn), lambda j: (layer, 0, j)),
                  pl.BlockSpec((1, 1, tn), lambda j: (layer, 0, j))],
        out_specs=pl.BlockSpec((8, tn), lambda j: (0, j)),
        out_shape=jax.ShapeDtypeStruct((8, n), F32),
        compiler_params=_params(("parallel",)),
        name="modulation",
    )(cs, w_mod, b_mod.reshape(depth, 1, n))


def _prenorm_body(x_ref, g_ref, sc_ref, sh_ref, o_ref):
    x = x_ref[0]
    ms = jnp.mean(x * x, axis=-1, keepdims=True)
    y = x * lax.rsqrt(ms + EPS) * g_ref[...]
    o_ref[0] = (y * (1.0 + sc_ref[0]) + sh_ref[0]).astype(o_ref.dtype)


def _prenorm(x, gain, scale, shift, out_dtype=BF16):
    b, l, d = x.shape
    ts = min(l, 1024)
    return pl.pallas_call(
        _prenorm_body,
        grid=(b, l // ts),
        in_specs=[pl.BlockSpec((1, ts, d), lambda bi, i: (bi, i, 0)),
                  pl.BlockSpec((1, d), lambda bi, i: (0, 0)),
                  pl.BlockSpec((1, 1, d), lambda bi, i: (bi, 0, 0)),
                  pl.BlockSpec((1, 1, d), lambda bi, i: (bi, 0, 0))],
        out_specs=pl.BlockSpec((1, ts, d), lambda bi, i: (bi, i, 0)),
        out_shape=jax.ShapeDtypeStruct((b, l, d), out_dtype),
        compiler_params=_params(("parallel", "parallel")),
        name="prenorm",
    )(x, gain.reshape(1, d), scale, shift)


def _mm_body(a_ref, w_ref, o_ref):
    o_ref[...] = _dot(a_ref[...], w_ref[...]).astype(o_ref.dtype)


def _cast_body(x_ref, o_ref):
    o_ref[...] = x_ref[...].astype(o_ref.dtype)


def _cast_layer_body(x_ref, o_ref):
    o_ref[...] = x_ref[0].astype(o_ref.dtype)


def _cast_weight(w, layer, tn):
    _, k, n = w.shape
    return pl.pallas_call(
        _cast_layer_body,
        grid=(n // tn,),
        in_specs=[pl.BlockSpec((1, k, tn), lambda j: (layer, 0, j))],
        out_specs=pl.BlockSpec((k, tn), lambda j: (0, j)),
        out_shape=jax.ShapeDtypeStruct((k, n), BF16),
        compiler_params=_params(("parallel",)),
        name="cast_weight",
    )(w)


def _matmul(a, w, n, tm, tn, out_dtype=F32):
    m, k = a.shape
    tm = min(tm, m)
    return pl.pallas_call(
        _mm_body,
        grid=(n // tn, m // tm),
        in_specs=[pl.BlockSpec((tm, k), lambda j, i: (i, 0)),
                  pl.BlockSpec((k, tn), lambda j, i: (0, j))],
        out_specs=pl.BlockSpec((tm, tn), lambda j, i: (i, j)),
        out_shape=jax.ShapeDtypeStruct((m, n), out_dtype),
        compiler_params=_params(("parallel", "parallel")),
        name="in_proj",
    )(a, w)


def _rope_tables(n_tok, rope):
    if not rope:
        return jnp.ones((n_tok, LANES), F32), jnp.zeros((n_tok, LANES), F32)
    rows = n_tok // GRID_W
    row = jnp.repeat(jnp.arange(rows), GRID_W).astype(F32)
    col = jnp.tile(jnp.arange(GRID_W), rows).astype(F32)
    quarter = HEAD_DIM // 4
    inv = ROPE_THETA ** (-jnp.arange(quarter, dtype=F32) / quarter)
    ar = row[:, None] * inv
    ac = col[:, None] * inv
    cos64 = jnp.concatenate([jnp.cos(ar), jnp.cos(ar), jnp.cos(ac), jnp.cos(ac)], axis=-1)
    sin64 = jnp.concatenate([-jnp.sin(ar), jnp.sin(ar), -jnp.sin(ac), jnp.sin(ac)], axis=-1)
    return jnp.tile(cos64, (1, 2)), jnp.tile(sin64, (1, 2))


def _head_ones():
    i = np.arange(LANES)
    return jnp.asarray((i[:, None] // HEAD_DIM) == (i[None, :] // HEAD_DIM), BF16)


def _qk_prep_body(x_ref, g_ref, cos_ref, sin_ref, ones_ref, o_ref, *, scale):
    x = x_ref[0]
    hi, lo = _split(x * x)
    ssum = _dot(hi, ones_ref[...]) + _dot(lo, ones_ref[...])
    y = x * lax.rsqrt(ssum * (1.0 / HEAD_DIM) + EPS) * g_ref[...]
    lane = lax.broadcasted_iota(jnp.int32, y.shape, 1)
    quarter = HEAD_DIM // 4
    partner = jnp.where((lane & quarter) == 0, pltpu.roll(y, LANES - quarter, 1), pltpu.roll(y, quarter, 1))
    o = (y * cos_ref[...] + partner * sin_ref[...]) * scale
    if o_ref.shape[2] == LANES:
        o_ref[0] = o.astype(o_ref.dtype)
    else:
        low = lane < HEAD_DIM
        even = jnp.where(low, o, 0.0)
        odd = jnp.where(low, 0.0, o)
        kv_head = pl.program_id(1) // (WIN_GROUP // 2)
        first = jnp.where(kv_head == 0, even, pltpu.roll(even, HEAD_DIM, 1))
        second = jnp.where(kv_head == 0, pltpu.roll(odd, HEAD_DIM, 1), odd)
        o_ref[0] = jnp.concatenate([first, second], axis=-1).astype(o_ref.dtype)


def _qk_prep(ux, col_off, width, gain, tables, scale, pad_heads=False):
    b, l, _ = ux.shape
    cos_t, sin_t = tables
    c0 = col_off // LANES
    ow = 2 * LANES if pad_heads else LANES
    g = jnp.tile(gain.reshape(1, HEAD_DIM), (1, LANES // HEAD_DIM))
    return pl.pallas_call(
        functools.partial(_qk_prep_body, scale=scale),
        grid=(b, width // LANES),
        in_specs=[pl.BlockSpec((1, l, LANES), lambda bi, j: (bi, 0, c0 + j)),
                  pl.BlockSpec((1, LANES), lambda bi, j: (0, 0)),
                  pl.BlockSpec((l, LANES), lambda bi, j: (0, 0)),
                  pl.BlockSpec((l, LANES), lambda bi, j: (0, 0)),
                  pl.BlockSpec((LANES, LANES), lambda bi, j: (0, 0))],
        out_specs=pl.BlockSpec((1, l, ow), lambda bi, j: (bi, 0, j)),
        out_shape=jax.ShapeDtypeStruct((b, l, width // LANES * ow), BF16),
        compiler_params=_params(("parallel", "parallel")),
        name="qk_prep",
    )(ux, g, cos_t, sin_t, _head_ones())


def _cast_swap_body(x_ref, o_ref, s_ref):
    x = x_ref[0]
    o_ref[0] = x.astype(o_ref.dtype)
    s_ref[0] = pltpu.roll(x, HEAD_DIM, 1).astype(s_ref.dtype)


def _cast_cols(ux, col_off, width, swapped=False):
    b, l, _ = ux.shape
    c0 = col_off // LANES
    blk = pl.BlockSpec((1, l, LANES), lambda bi, j: (bi, 0, j))
    shp = jax.ShapeDtypeStruct((b, l, width), BF16)
    return pl.pallas_call(
        _cast_swap_body if swapped else _cast_body,
        grid=(b, width // LANES),
        in_specs=[pl.BlockSpec((1, l, LANES), lambda bi, j: (bi, 0, c0 + j))],
        out_specs=[blk, blk] if swapped else blk,
        out_shape=[shp, shp] if swapped else shp,
        compiler_params=_params(("parallel", "parallel")),
        name="cast_cols",
    )(ux)


def _nt_dot(a, b):
    return lax.dot_general(a, b, (((1,), (1,)), ((), ())), preferred_element_type=F32)


_WIN_STACK = (0, 2, 1, 3)


def _win_heads(q, sink_ref, h, n):
    heads = [h * WIN_GROUP + g for g in _WIN_STACK]
    qh = jnp.concatenate([q[:, a * LANES:(a + 1) * LANES] for a in heads], axis=0)
    sk = jnp.concatenate([jnp.full((n, 1), sink_ref[a], F32) for a in heads], axis=0)
    return qh, sk


def _win_tiles(o_low, o_high, n):
    low = lax.broadcasted_iota(jnp.int32, (n, LANES), 1) < HEAD_DIM
    return [jnp.where(low, o_low[t * n:(t + 1) * n], o_high[t * n:(t + 1) * n]) for t in range(2)]


WIN_STEP_BLOCKS = 8


def _win_attn_body(sink_ref, q_ref, km_ref, k0_ref, kp_ref, vm_ref, v0_ref, vp_ref, wm_ref, w0_ref, wp_ref,
                   kc_ref, vc_ref, wc_ref, o_ref, *, seq, nsub):
    i = pl.program_id(1)
    kslab = jnp.concatenate([km_ref[0], k0_ref[0], kp_ref[0]], axis=0)
    vslab = jnp.concatenate([vm_ref[0], v0_ref[0], vp_ref[0]], axis=0)
    wslab = jnp.concatenate([wm_ref[0], w0_ref[0], wp_ref[0]], axis=0)
    kc = kc_ref[0]
    qi = lax.broadcasted_iota(jnp.int32, (BLOCK, 3 * BLOCK), 0)
    kj = lax.broadcasted_iota(jnp.int32, (BLOCK, 3 * BLOCK), 1)
    band = jnp.abs(kj - BLOCK - qi) <= WINDOW
    half = WIN_GROUP // 2 * BLOCK
    chains = [(j, h) for j in range(nsub) for h in range(WIN_KV_HEADS)]
    scores = []
    for j, h in chains:
        qh, sk = _win_heads(q_ref[0, j * BLOCK:(j + 1) * BLOCK], sink_ref, h, BLOCK)
        kpos = (i * nsub + j - 1) * BLOCK + kj
        valid = jnp.concatenate([band & (kpos >= 0) & (kpos < seq)] * WIN_GROUP, axis=0)
        s_loc = jnp.where(valid, _nt_dot(qh, kslab[j * BLOCK:(j + 3) * BLOCK]), NEG_INF)
        scores.append((s_loc, _nt_dot(qh, kc), sk))
    maxes = [jnp.maximum(jnp.maximum(jnp.max(s_loc, axis=-1, keepdims=True),
                                     jnp.max(s_ctx, axis=-1, keepdims=True)), sk) for s_loc, s_ctx, sk in scores]
    probs = [(jnp.exp(s_loc - m), jnp.exp(s_ctx - m), jnp.exp(sk - m)) for (s_loc, s_ctx, sk), m in zip(scores, maxes)]
    denoms = [jnp.sum(p_loc, axis=-1, keepdims=True) + jnp.sum(p_ctx, axis=-1, keepdims=True) + p_sink
              for p_loc, p_ctx, p_sink in probs]
    tiles = {}
    for (j, h), (p_loc, p_ctx, _), denom in zip(chains, probs, denoms):
        ws = slice(j * BLOCK, (j + 3) * BLOCK)
        v_nat = (vslab[ws], vc_ref[0])
        v_swp = (wslab[ws], wc_ref[0])
        v_low, v_high = (v_nat, v_swp) if h == 0 else (v_swp, v_nat)
        p_loc = p_loc.astype(BF16)
        p_ctx = p_ctx.astype(BF16)
        o_low = (_dot(p_loc[:half], v_low[0]) + _dot(p_ctx[:half], v_low[1])) / denom[:half]
        o_high = (_dot(p_loc[half:], v_high[0]) + _dot(p_ctx[half:], v_high[1])) / denom[half:]
        tiles[(j, h)] = _win_tiles(o_low, o_high, BLOCK)
    for j in range(nsub):
        row = [t for h in range(WIN_KV_HEADS) for t in tiles[(j, h)]]
        o_ref[0, j * BLOCK:(j + 1) * BLOCK] = jnp.concatenate(row, axis=-1).astype(o_ref.dtype)


def _win_attn(q, k, v, vs, kc, vc, vcs, sink):
    b, s, _ = q.shape
    c = kc.shape[1]
    nb = s // BLOCK
    nsub = math.gcd(nb, WIN_STEP_BLOCKS)
    rows = nsub * BLOCK
    kvw = WIN_KV_HEADS * HEAD_DIM
    cur = lambda bi, i: (bi, i, 0)
    prev = pl.BlockSpec((1, BLOCK, kvw), lambda bi, i: (bi, jnp.maximum(i * nsub - 1, 0), 0))
    main = pl.BlockSpec((1, rows, kvw), cur)
    nxt = pl.BlockSpec((1, BLOCK, kvw), lambda bi, i: (bi, jnp.minimum((i + 1) * nsub, nb - 1), 0))
    cspec = pl.BlockSpec((1, c, kvw), lambda bi, i: (bi, 0, 0))
    return pl.pallas_call(
        functools.partial(_win_attn_body, seq=s, nsub=nsub),
        grid=(b, nb // nsub),
        in_specs=[pl.BlockSpec(memory_space=pltpu.SMEM),
                  pl.BlockSpec((1, rows, WIN_HEADS * LANES), cur),
                  prev, main, nxt, prev, main, nxt, prev, main, nxt,
                  cspec, cspec, cspec],
        out_specs=pl.BlockSpec((1, rows, WIN_HEADS * HEAD_DIM), cur),
        out_shape=jax.ShapeDtypeStruct((b, s, WIN_HEADS * HEAD_DIM), BF16),
        compiler_params=_params(("parallel", "parallel")),
        name="win_attn",
    )(sink.astype(F32), q, k, k, k, v, v, v, vs, vs, vs, kc, vc, vcs)


def _ctx_win_attn_body(sink_ref, q_ref, k_ref, v_ref, w_ref, o_ref):
    q = q_ref[0]
    k = k_ref[0]
    c = q.shape[0]
    half = WIN_GROUP // 2 * c
    tiles = []
    for h in range(WIN_KV_HEADS):
        qh, sk = _win_heads(q, sink_ref, h, c)
        s = _nt_dot(qh, k)
        m = jnp.maximum(jnp.max(s, axis=-1, keepdims=True), sk)
        p = jnp.exp(s - m)
        denom = jnp.sum(p, axis=-1, keepdims=True) + jnp.exp(sk - m)
        p = p.astype(BF16)
        v_low, v_high = (v_ref[0], w_ref[0]) if h == 0 else (w_ref[0], v_ref[0])
        tiles.extend(_win_tiles(_dot(p[:half], v_low) / denom[:half], _dot(p[half:], v_high) / denom[half:], c))
    o_ref[0] = jnp.concatenate(tiles, axis=-1).astype(o_ref.dtype)


def _ctx_win_attn(q, k, v, vs, sink):
    b, c, _ = q.shape
    kvw = WIN_KV_HEADS * HEAD_DIM
    cspec = pl.BlockSpec((1, c, kvw), lambda bi: (bi, 0, 0))
    return pl.pallas_call(
        _ctx_win_attn_body,
        grid=(b,),
        in_specs=[pl.BlockSpec(memory_space=pltpu.SMEM),
                  pl.BlockSpec((1, c, WIN_HEADS * LANES), lambda bi: (bi, 0, 0)),
                  cspec, cspec, cspec],
        out_specs=pl.BlockSpec((1, c, WIN_HEADS * HEAD_DIM), lambda bi: (bi, 0, 0)),
        out_shape=jax.ShapeDtypeStruct((b, c, WIN_HEADS * HEAD_DIM), BF16),
        compiler_params=_params(("parallel",)),
        name="ctx_win_attn",
    )(sink.astype(F32), q, k, v, vs)


DIFF_CHAINS = 2


def _diff_attn_body(lam_ref, q_ref, k_ref, v_ref, g_ref, o_ref, *, lam_init):
    q = q_ref[0]
    k = k_ref[0]
    v = v_ref[0]
    lp = lam_ref[...]
    lam = (jnp.exp(jnp.sum(lp[0:1] * lp[1:2], axis=-1, keepdims=True))
           - jnp.exp(jnp.sum(lp[2:3] * lp[3:4], axis=-1, keepdims=True)) + lam_init)
    lane = lax.broadcasted_iota(jnp.int32, q.shape, 1)
    zero = jnp.zeros_like(q)
    q1 = jnp.where(lane < HEAD_DIM, q, zero)
    q2 = jnp.where(lane < HEAD_DIM, zero, q)
    rows = q.shape[0] // DIFF_CHAINS
    for t in range(DIFF_CHAINS):
        rs = slice(t * rows, (t + 1) * rows)
        s1 = _nt_dot(q1[rs], k)
        s2 = _nt_dot(q2[rs], k)
        p1 = jnp.exp2(s1 - jnp.max(s1, axis=-1, keepdims=True))
        p2 = jnp.exp2(s2 - jnp.max(s2, axis=-1, keepdims=True))
        l1 = jnp.sum(p1, axis=-1, keepdims=True)
        r21 = lam * l1 / jnp.sum(p2, axis=-1, keepdims=True)
        a = (p1 - p2 * r21).astype(BF16)
        o = _dot(a, v) / l1
        ms = jnp.mean(o * o, axis=-1, keepdims=True)
        o_ref[0, rs] = (o * lax.rsqrt(ms + EPS) * g_ref[...] * (1.0 - lam_init)).astype(o_ref.dtype)


def _diff_attn(q, k, v, lam_params, subln, lam_init, tq):
    b, nq, w = q.shape
    nk = k.shape[1]
    tq = min(tq, nq)
    hw = 2 * HEAD_DIM
    return pl.pallas_call(
        functools.partial(_diff_attn_body, lam_init=lam_init),
        grid=(b, DIFF_HEADS, nq // tq),
        in_specs=[pl.BlockSpec((4, HEAD_DIM), lambda bi, h, t: (0, 0)),
                  pl.BlockSpec((1, tq, hw), lambda bi, h, t: (bi, t, h)),
                  pl.BlockSpec((1, nk, hw), lambda bi, h, t: (bi, 0, h)),
                  pl.BlockSpec((1, nk, hw), lambda bi, h, t: (bi, 0, h)),
                  pl.BlockSpec((1, hw), lambda bi, h, t: (0, 0))],
        out_specs=pl.BlockSpec((1, tq, hw), lambda bi, h, t: (bi, t, h)),
        out_shape=jax.ShapeDtypeStruct((b, nq, w), BF16),
        compiler_params=_params(("parallel", "parallel", "parallel"), 56),
        name="diff_attn",
    )(lam_params, q, k, v, subln.reshape(1, hw))


def _hy_pre_body(x0_ref, x1_ref, v_ref, w0_ref, w1_ref, wv_ref, b0_ref, b1_ref, bv_ref, z_ref, x0c_ref):
    n = x0_ref.shape[1]
    row = lax.broadcasted_iota(jnp.int32, (n, 1), 0)

    def sconv(u_ref, w_ref, b_ref):
        u = u_ref[0]
        w = w_ref[...]
        up = jnp.where(row == 0, 0.0, pltpu.roll(u, 1, 0))
        un = jnp.where(row == n - 1, 0.0, pltpu.roll(u, n - 1, 0))
        return ((b_ref[...] + up * w[0:1]) + u * w[1:2]) + un * w[2:3]

    x0c_ref[0] = sconv(x0_ref, w0_ref, b0_ref)
    z_ref[0] = sconv(v_ref, wv_ref, bv_ref) * sconv(x1_ref, w1_ref, b1_ref)


def _hy_pre(ux, conv_w, conv_b):
    b, l, _ = ux.shape
    nc = HY_WIDTH // LANES
    c0 = OFF_HY // LANES
    ub = lambda seg: pl.BlockSpec((1, l, LANES), lambda bi, j: (bi, 0, c0 + seg * nc + j))
    wb = lambda seg: pl.BlockSpec((3, LANES), lambda bi, j: (0, seg * nc + j))
    bb = lambda seg: pl.BlockSpec((1, LANES), lambda bi, j: (0, seg * nc + j))
    out = pl.BlockSpec((1, l, LANES), lambda bi, j: (bi, 0, j))
    shp = jax.ShapeDtypeStruct((b, l, HY_WIDTH), F32)
    cb = conv_b.reshape(1, 3 * HY_WIDTH)
    return pl.pallas_call(
        _hy_pre_body,
        grid=(b, nc),
        in_specs=[ub(0), ub(1), ub(2), wb(0), wb(1), wb(2), bb(0), bb(1), bb(2)],
        out_specs=[out, out],
        out_shape=[shp, shp],
        compiler_params=_params(("parallel", "parallel")),
        name="hy_pre",
    )(ux, ux, ux, conv_w, conv_w, conv_w, cb, cb, cb)


def _hy_features(l):
    n = jnp.arange(l, dtype=F32)
    pos = jnp.concatenate([n, l - n])
    t = pos / max(l - 1, 1)
    w = 2.0 * math.pi * pos / l
    f = jnp.linspace(1e-4, HY_BANDS - 1, HY_BANDS, dtype=F32)
    flag = jnp.ones((2 * l, 1), F32).at[l].set(0.0)
    feat = jnp.concatenate([t[:, None], jnp.cos(w[:, None] * f), -jnp.sin(w[:, None] * f), flag], axis=-1)
    return jnp.pad(feat, ((0, 0), (0, LANES - HY_EMB - 1)))


def _hy_filter_body(f_ref, w1_ref, b1_ref, w2_ref, b2_ref, w3_ref, fr_ref, dec_ref, k_ref, sum_ref):
    first = (pl.program_id(0) == 0) & (pl.program_id(1) == 0)

    @pl.when(first)
    def _():
        sum_ref[...] = jnp.zeros_like(sum_ref)

    f = f_ref[...]
    fr = fr_ref[...]
    h = jnp.sin(fr[0:1] * (_dot3(f, w1_ref[...]) + b1_ref[...]))
    h = jnp.sin(fr[1:2] * (_dot3(h, w2_ref[...]) + b2_ref[...]))
    h = _dot3(h, w3_ref[...]) * jnp.exp(-f[:, 0:1] * jnp.abs(dec_ref[...]))
    h = h * f[:, HY_EMB:HY_EMB + 1]
    k_ref[...] = h
    sum_ref[...] += jnp.sum(jnp.abs(h), axis=0, keepdims=True)


def _hy_filter(l, p):
    tl = min(l, 1024)
    nt = l // tl
    order = p['hy_w1'].shape[1]
    w1 = jnp.zeros((LANES, order), F32).at[:HY_EMB].set(p['hy_w1'])
    const = lambda d, i: (0, 0)
    return pl.pallas_call(
        _hy_filter_body,
        grid=(2, nt),
        in_specs=[pl.BlockSpec((tl, LANES), lambda d, i: (d * nt + i, 0)),
                  pl.BlockSpec((LANES, order), const),
                  pl.BlockSpec((1, order), const),
                  pl.BlockSpec((order, order), const),
                  pl.BlockSpec((1, order), const),
                  pl.BlockSpec((order, HY_WIDTH), lambda d, i: (0, d)),
                  pl.BlockSpec((2, order), const),
                  pl.BlockSpec((1, HY_WIDTH), lambda d, i: (0, d))],
        out_specs=[pl.BlockSpec((tl, HY_WIDTH), lambda d, i: (d * nt + i, 0)),
                   pl.BlockSpec((1, HY_WIDTH), const)],
        out_shape=[jax.ShapeDtypeStruct((2 * l, HY_WIDTH), F32), jax.ShapeDtypeStruct((1, HY_WIDTH), F32)],
        compiler_params=_params(("arbitrary", "arbitrary")),
        name="hy_filter",
    )(_hy_features(l), w1, p['hy_b1'].reshape(1, order), p['hy_w2'], p['hy_b2'].reshape(1, order),
      p['hy_w3'], p['hy_freq'], p['hy_decay'].reshape(1, 2 * HY_WIDTH))


def _fft_consts(n_la):
    ka = np.arange(FFT_NA)[:, None]
    la = np.arange(n_la)[None, :]
    ang = 2.0 * math.pi * ka * la / FFT_NA
    f1 = np.concatenate([np.cos(ang), -np.sin(ang)], axis=0)
    lb = np.arange(FFT_NB)[None, :]
    th = 2.0 * math.pi * ka * lb / FFT_N
    tc, ts = np.cos(th), np.sin(th)
    a2 = 2.0 * math.pi * np.arange(FFT_NB)[:, None] * np.arange(FFT_NB)[None, :] / FFT_NB
    gre, gim = np.cos(a2), -np.sin(a2)
    g2 = np.block([[gre, gim], [-gim, gre]])
    g2i = np.block([[gre, -gim], [gim, gre]])
    lo = np.arange(FFT_NA // 2)[:, None]
    ph = 2.0 * math.pi * lo * np.arange(FFT_NA)[None, :] / FFT_NA
    f1i = np.concatenate([np.cos(ph), -np.sin(ph)], axis=1) / FFT_N
    return f1, tc, ts, g2, g2i, f1i


def _fft_fwd(x_ref, f1_ref, tc_ref, ts_ref, g2_ref, p2_ref, cc, dot):
    rhs = jnp.concatenate([x_ref[0, c] for c in range(cc)], axis=1)
    a = dot(f1_ref[...].astype(p2_ref.dtype), rhs.astype(p2_ref.dtype))
    tc = tc_ref[...]
    ts = ts_ref[...]
    for c in range(cc):
        are = a[0:FFT_NA, c * FFT_NB:(c + 1) * FFT_NB]
        aim = a[FFT_NA:2 * FFT_NA, c * FFT_NB:(c + 1) * FFT_NB]
        p2_ref[c * FFT_NA:(c + 1) * FFT_NA, 0:FFT_NB] = (are * tc + aim * ts).astype(p2_ref.dtype)
        p2_ref[c * FFT_NA:(c + 1) * FFT_NA, FFT_NB:2 * FFT_NB] = (aim * tc - are * ts).astype(p2_ref.dtype)
    return dot(p2_ref[...], g2_ref[...].astype(p2_ref.dtype))


def _fft_spec_body(x_ref, f1_ref, tc_ref, ts_ref, g2_ref, o_ref, p2_ref, *, cc):
    o_ref[...] = _fft_fwd(x_ref, f1_ref, tc_ref, ts_ref, g2_ref, p2_ref, cc, _dot3)


def _fft_conv_body(x_ref, kf_ref, f1_ref, tc_ref, ts_ref, g2_ref, g2i_ref, f1i_ref, o_ref, p2_ref, r2_ref, *, cc):
    z = _fft_fwd(x_ref, f1_ref, tc_ref, ts_ref, g2_ref, p2_ref, cc, _dot)
    kf = kf_ref[...]
    zre, zim = z[:, :FFT_NB], z[:, FFT_NB:]
    kre, kim = kf[:, :FFT_NB], kf[:, FFT_NB:]
    y = jnp.concatenate([zre * kre - zim * kim, zre * kim + zim * kre], axis=1).astype(BF16)
    u = _dot(y, g2i_ref[...].astype(BF16))
    tc = tc_ref[...]
    ts = ts_ref[...]
    for c in range(cc):
        ure = u[c * FFT_NA:(c + 1) * FFT_NA, 0:FFT_NB]
        uim = u[c * FFT_NA:(c + 1) * FFT_NA, FFT_NB:2 * FFT_NB]
        r2_ref[0:FFT_NA, c * FFT_NB:(c + 1) * FFT_NB] = (ure * tc - uim * ts).astype(BF16)
        r2_ref[FFT_NA:2 * FFT_NA, c * FFT_NB:(c + 1) * FFT_NB] = (ure * ts + uim * tc).astype(BF16)
    out = _dot(f1i_ref[...].astype(BF16), r2_ref[...])
    for c in range(cc):
        o_ref[0, c] = out[:, c * FFT_NB:(c + 1) * FFT_NB]


FFT_CC = 64


def _fft_spectrum(kt):
    nch, n_la, _ = kt.shape
    cc = FFT_CC
    f1, tc, ts, g2, _, _ = _fft_consts(n_la)
    const = lambda j: (0, 0)
    return pl.pallas_call(
        functools.partial(_fft_spec_body, cc=cc),
        grid=(nch // cc,),
        in_specs=[pl.BlockSpec((1, cc, n_la, FFT_NB), lambda j: (0, j, 0, 0)),
                  pl.BlockSpec((2 * FFT_NA, n_la), const),
                  pl.BlockSpec((FFT_NA, FFT_NB), const),
                  pl.BlockSpec((FFT_NA, FFT_NB), const),
                  pl.BlockSpec((2 * FFT_NB, 2 * FFT_NB), const)],
        out_specs=pl.BlockSpec((cc * FFT_NA, 2 * FFT_NB), lambda j: (j, 0)),
        out_shape=jax.ShapeDtypeStruct((nch * FFT_NA, 2 * FFT_NB), F32),
        scratch_shapes=[pltpu.VMEM((cc * FFT_NA, 2 * FFT_NB), F32)],
        compiler_params=_params(("parallel",)),
        name="fft_spectrum",
    )(kt[None], jnp.asarray(f1, F32), jnp.asarray(tc, F32), jnp.asarray(ts, F32), jnp.asarray(g2, F32))


def _fft_conv(zt, kf):
    b, nch, n_la, _ = zt.shape
    cc = FFT_CC
    f1, tc, ts, g2, g2i, f1i = _fft_consts(n_la)
    const = lambda j, bi: (0, 0)
    return pl.pallas_call(
        functools.partial(_fft_conv_body, cc=cc),
        grid=(nch // cc, b),
        in_specs=[pl.BlockSpec((1, cc, n_la, FFT_NB), lambda j, bi: (bi, j, 0, 0)),
                  pl.BlockSpec((cc * FFT_NA, 2 * FFT_NB), lambda j, bi: (j, 0)),
                  pl.BlockSpec((2 * FFT_NA, n_la), const),
                  pl.BlockSpec((FFT_NA, FFT_NB), const),
                  pl.BlockSpec((FFT_NA, FFT_NB), const),
                  pl.BlockSpec((2 * FFT_NB, 2 * FFT_NB), const),
                  pl.BlockSpec((2 * FFT_NB, 2 * FFT_NB), const),
                  pl.BlockSpec((FFT_NA // 2, 2 * FFT_NA), const)],
        out_specs=pl.BlockSpec((1, cc, n_la, FFT_NB), lambda j, bi: (bi, j, 0, 0)),
        out_shape=jax.ShapeDtypeStruct((b, nch, n_la, FFT_NB), F32),
        scratch_shapes=[pltpu.VMEM((cc * FFT_NA, 2 * FFT_NB), BF16),
                        pltpu.VMEM((2 * FFT_NA, cc * FFT_NB), BF16)],
        compiler_params=_params(("parallel", "parallel")),
        name="fft_conv",
    )(zt, kf, jnp.asarray(f1, F32), jnp.asarray(tc, F32), jnp.asarray(ts, F32),
      jnp.asarray(g2, F32), jnp.asarray(g2i, F32), jnp.asarray(f1i, F32))


def _dft_conv_body(z_ref, k_ref, fw_ref, fi_ref, o_ref):
    l = z_ref.shape[1]
    fw = fw_ref[...]
    n = fw.shape[1]
    kf = _dot3(fw, k_ref[...])
    zf = _dot3(fw[:, :l], z_ref[0])
    kre, kim = kf[:n], kf[n:]
    zre, zim = zf[:n], zf[n:]
    y = jnp.concatenate([zre * kre - zim * kim, zre * kim + zim * kre], axis=0)
    o_ref[0] = _dot3(fi_ref[...], y)


def _dft_conv(z, k):
    b, l, w = z.shape
    n = 2 * l
    ang = 2.0 * math.pi * np.arange(n)[:, None] * np.arange(n)[None, :] / n
    fw = np.concatenate([np.cos(ang), -np.sin(ang)], axis=0)
    ai = ang[:l]
    fi = np.concatenate([np.cos(ai), -np.sin(ai)], axis=1) / n
    return pl.pallas_call(
        _dft_conv_body,
        grid=(b,),
        in_specs=[pl.BlockSpec((1, l, w), lambda bi: (bi, 0, 0)),
                  pl.BlockSpec((n, w), lambda bi: (0, 0)),
                  pl.BlockSpec((2 * n, n), lambda bi: (0, 0)),
                  pl.BlockSpec((l, 2 * n), lambda bi: (0, 0))],
        out_specs=pl.BlockSpec((1, l, w), lambda bi: (bi, 0, 0)),
        out_shape=jax.ShapeDtypeStruct((b, l, w), F32),
        compiler_params=_params(("parallel",)),
        name="dft_conv",
    )(z, k, jnp.asarray(fw, F32), jnp.asarray(fi, F32))


def _hy_post_body(y_ref, z_ref, x0_ref, ks_ref, sk_ref, o_ref):
    y = y_ref[0] / (ks_ref[...] + EPS) + sk_ref[...] * z_ref[0]
    o_ref[0] = (y * x0_ref[0]).astype(o_ref.dtype)


def _hy_post(yraw, z, x0c, ksum, skip):
    b, l, w = z.shape
    ts = min(l, 1024)
    blk = pl.BlockSpec((1, ts, w), lambda bi, i: (bi, i, 0))
    vec = pl.BlockSpec((1, w), lambda bi, i: (0, 0))
    return pl.pallas_call(
        _hy_post_body,
        grid=(b, l // ts),
        in_specs=[blk, blk, blk, vec, vec],
        out_specs=blk,
        out_shape=jax.ShapeDtypeStruct((b, l, w), BF16),
        compiler_params=_params(("parallel", "parallel")),
        name="hy_post",
    )(yraw, z, x0c, ksum, skip.reshape(1, w))


def _hyena(ux, p):
    b, l, _ = ux.shape
    z, x0c = _hy_pre(ux, p['hy_conv_w'], p['hy_conv_b'])
    k_raw, ksum = _hy_filter(l, p)
    if 2 * l == FFT_N:
        kt = k_raw.T.reshape(HY_WIDTH, FFT_NA, FFT_NB)
        kf = _fft_spectrum(kt)
        zt = jnp.swapaxes(z, 1, 2).reshape(b, HY_WIDTH, l // FFT_NB, FFT_NB)
        yt = _fft_conv(zt, kf)
        yraw = jnp.swapaxes(yt.reshape(b, HY_WIDTH, l), 1, 2)
    else:
        yraw = _dft_conv(z, k_raw)
    return _hy_post(yraw, z, x0c, ksum, p['hy_skip'])


def _pool_body(u_ref, w_ref, sc_ref, o_ref, *, win):
    u = u_ref[0]
    n = u.shape[0]
    lo = win // 2
    hi = win - 1 - lo
    row = lax.broadcasted_iota(jnp.int32, (n, 1), 0)
    acc = u
    for j in range(1, lo + 1):
        acc = acc + jnp.where(row >= j, pltpu.roll(u, j, 0), 0.0)
    for j in range(1, hi + 1):
        acc = acc + jnp.where(row < n - j, pltpu.roll(u, n - j, 0), 0.0)
    cnt = (jnp.minimum(row + hi + 1, n) - jnp.maximum(row - lo, 0)).astype(F32)
    d = acc / cnt - u
    o_ref[0] = (_dot(d.astype(BF16), w_ref[0].astype(BF16)) * sc_ref[...]).astype(o_ref.dtype)


def _pool_mix(ux, w_grp, scale):
    b, l, _ = ux.shape
    c0 = OFF_PL // LANES
    outs = []
    for gi, win in enumerate(POOL_WINDOWS):
        outs.append(pl.pallas_call(
            functools.partial(_pool_body, win=win),
            grid=(b,),
            in_specs=[pl.BlockSpec((1, l, LANES), lambda bi, gi=gi: (bi, 0, c0 + gi)),
                      pl.BlockSpec((1, LANES, LANES), lambda bi, gi=gi: (gi, 0, 0)),
                      pl.BlockSpec((1, LANES), lambda bi, gi=gi: (0, gi))],
            out_specs=pl.BlockSpec((1, l, LANES), lambda bi: (bi, 0, 0)),
            out_shape=jax.ShapeDtypeStruct((b, l, LANES), BF16),
            compiler_params=_params(("parallel",)),
            name=f"pool_mix{win}",
        )(ux, w_grp, scale.reshape(1, -1)))
    return jnp.concatenate(outs, axis=-1)


def _merge1_body(hx_ref, y0, y1, y2, y3, g0, g1, g2, g3, b0, b1, b2, b3, o_ref):
    hx = hx_ref[...]
    acc = None
    for y_ref, g_ref, b_ref in ((y0, g0, b0), (y1, g1, b1), (y2, g2, b2), (y3, g3, b3)):
        t = jax.nn.sigmoid(_dot(hx, g_ref[...])) * _dot(y_ref[...], b_ref[0, 0])
        acc = t if acc is None else acc + t
    o_ref[...] = acc.astype(o_ref.dtype)


def _merge1(hx, ys, w_in, gate_off, w_branch, layer, tm=1024, tn=256):
    m, d = hx.shape
    tm = min(tm, m)
    nj = d // tn
    g0 = gate_off // tn
    row = lambda i, j: (i, 0)
    gspec = lambda br: pl.BlockSpec((d, tn), lambda i, j, br=br: (0, g0 + br * nj + j))
    bspec = lambda br: pl.BlockSpec((1, 1, BRANCH_W, tn), lambda i, j, br=br: (layer, br, 0, j))
    return pl.pallas_call(
        _merge1_body,
        grid=(m // tm, nj),
        in_specs=([pl.BlockSpec((tm, d), row)] + [pl.BlockSpec((tm, BRANCH_W), row)] * N_BRANCH
                  + [gspec(br) for br in range(N_BRANCH)] + [bspec(br) for br in range(N_BRANCH)]),
        out_specs=pl.BlockSpec((tm, tn), lambda i, j: (i, j)),
        out_shape=jax.ShapeDtypeStruct((m, d), BF16),
        compiler_params=_params(("parallel", "parallel")),
        name="merge_gate",
    )(hx, *ys, w_in, w_in, w_in, w_in, w_branch, w_branch, w_branch, w_branch)


def _merge2_body(a_ref, w_ref, x_ref, g_ref, n2_ref, sc_ref, sh_ref, wr_ref, br_ref, o_ref, h_ref, e_ref, ew_ref):
    x = x_ref[0] + g_ref[0] * _dot(a_ref[0], w_ref[0])
    o_ref[0] = x
    _route_rows(x, n2_ref, sc_ref, sh_ref, wr_ref, br_ref, h_ref, e_ref, ew_ref)


def _merge2(acc, w_out, layer, x, gate, norm2, scale2, shift2, w_router, b_router, tm=512):
    b, l, d = x.shape
    tm = min(tm, l)
    nl = l // tm
    blk = pl.BlockSpec((1, tm, d), lambda bi, i: (bi, i, 0))
    mod = pl.BlockSpec((1, 1, d), lambda bi, i: (bi, 0, 0))
    sm = pl.BlockSpec((1, tm, LANES), lambda bi, i: (bi, i, 0))
    const = lambda bi, i: (0, 0)
    return pl.pallas_call(
        _merge2_body,
        grid=(b, nl),
        in_specs=[blk, pl.BlockSpec((1, d, d), lambda bi, i: (layer, 0, 0)), blk, mod,
                  pl.BlockSpec((1, d), const), mod, mod,
                  pl.BlockSpec((d, LANES), const), pl.BlockSpec((1, LANES), const)],
        out_specs=[blk, pl.BlockSpec((tm * (d // LANES), LANES), lambda bi, i: (bi * nl + i, 0)), sm, sm],
        out_shape=[jax.ShapeDtypeStruct((b, l, d), F32),
                   jax.ShapeDtypeStruct((b * l * (d // LANES), LANES), F32),
                   jax.ShapeDtypeStruct((b, l, LANES), jnp.int32),
                   jax.ShapeDtypeStruct((b, l, LANES), F32)],
        compiler_params=_params(("parallel", "parallel"), 56),
        name="merge_out",
    )(acc, w_out, x, gate, norm2.reshape(1, d), scale2, shift2, w_router, b_router)


def _store_token_major(ref, x):
    n, d = x.shape
    pitch = d // LANES
    for j in range(pitch):
        ref[pl.ds(j, n, stride=pitch), :] = x[:, j * LANES:(j + 1) * LANES]


def _load_token_major(ref, n):
    pitch = ref.shape[0] // n
    return jnp.concatenate([ref[pl.ds(j, n, stride=pitch), :] for j in range(pitch)], axis=1)


def _route_rows(x, g_ref, sc_ref, sh_ref, wr_ref, br_ref, h_ref, e_ref, w_ref):
    ms = jnp.mean(x * x, axis=-1, keepdims=True)
    h = (x * lax.rsqrt(ms + EPS) * g_ref[...]) * (1.0 + sc_ref[0]) + sh_ref[0]
    _store_token_major(h_ref, h)
    logits = _dot3(h, wr_ref[...]) + br_ref[...]
    lane = lax.broadcasted_iota(jnp.int32, logits.shape, 1)
    big = jnp.int32(LANES)
    gl = jnp.where(lane < N_GROUPS, logits, -jnp.inf)
    ge = jnp.exp(gl - jnp.max(gl, axis=-1, keepdims=True))
    pg = ge / jnp.sum(ge, axis=-1, keepdims=True)
    pg_top = jnp.max(pg, axis=-1, keepdims=True)
    g_idx = jnp.min(jnp.where((pg == pg_top) & (lane < N_GROUPS), lane, big), axis=-1, keepdims=True)
    first = N_GROUPS + EXPERTS_PER_GROUP * g_idx
    le = jnp.where((lane >= first) & (lane < first + EXPERTS_PER_GROUP), logits, -jnp.inf)
    v1 = jnp.max(le, axis=-1, keepdims=True)
    i1 = jnp.min(jnp.where(le == v1, lane, big), axis=-1, keepdims=True)
    le2 = jnp.where(lane == i1, -jnp.inf, le)
    v2 = jnp.max(le2, axis=-1, keepdims=True)
    i2 = jnp.min(jnp.where(le2 == v2, lane, big), axis=-1, keepdims=True)
    e2 = jnp.exp(v2 - v1)
    den = 1.0 + e2
    e_ref[0] = jnp.where(lane == 0, i1 - N_GROUPS, jnp.where(lane == 1, i2 - N_GROUPS, 0))
    w_ref[0] = jnp.where(lane == 0, (1.0 / den) * pg_top, jnp.where(lane == 1, (e2 / den) * pg_top, 0.0))


MOE_GATHER_CHUNK = 32


def _ffn_up_body(blk_e_ref, nused_ref, nval_ref, first_ref, gi_ref, gn_ref, w1_ref, w3_ref, h_hbm, o_ref,
                 xbuf, w1b, w3b, sem, *, pitch):
    i = pl.program_id(0)
    n = pl.num_programs(0)
    tm = o_ref.shape[0]
    slot = i % 2
    chunk = MOE_GATHER_CHUNK

    def gather(idx_ref, s, nv):
        for c in range(tm // chunk):
            @pl.when(c * chunk < nv)
            def _():
                for r in range(c * chunk, (c + 1) * chunk):
                    pltpu.make_async_copy(h_hbm.at[pl.ds(idx_ref[0, 0, r], pitch)],
                                          xbuf.at[s, pl.ds(r * pitch, pitch)], sem.at[s]).start(priority=r % 2)

    @pl.when(i == 0)
    def _():
        xbuf[...] = jnp.zeros_like(xbuf)
        gather(gi_ref, 0, nval_ref[0])

    @pl.when(i + 1 < n)
    def _():
        gather(gn_ref, 1 - slot, nval_ref[jnp.minimum(i + 1, n - 1)])

    for c in range(tm // chunk):
        @pl.when(c * chunk < nval_ref[i])
        def _():
            pltpu.make_async_copy(h_hbm.at[pl.ds(0, chunk * pitch)],
                                  xbuf.at[slot, pl.ds(c * chunk * pitch, chunk * pitch)], sem.at[slot]).wait()

    @pl.when(first_ref[i] == 1)
    def _():
        w1b[...] = w1_ref[0, 0].astype(BF16)
        w3b[...] = w3_ref[0, 0].astype(BF16)

    @pl.when(i < nused_ref[0])
    def _():
        x = _load_token_major(xbuf.at[slot], tm).astype(BF16)
        a = _dot(x, w1b[...])
        o_ref[...] = ((a * jax.nn.sigmoid(a)) * _dot(x, w3b[...])).astype(o_ref.dtype)

    @pl.when(i >= nused_ref[0])
    def _():
        o_ref[...] = jnp.zeros_like(o_ref)


def _ffn_down_body(blk_e_ref, nused_ref, first_ref, h_ref, si_ref, sw_ref, w2_ref, y_hbm, obuf, w2b, ssem, *, pitch):
    i = pl.program_id(0)
    n = pl.num_programs(0)
    tm = sw_ref.shape[0]
    slot = i % 2

    def wait_scatter(s):
        pltpu.make_async_copy(obuf.at[s], y_hbm.at[pl.ds(0, tm * pitch)], ssem.at[s]).wait()

    @pl.when(i >= 2)
    def _():
        wait_scatter(slot)

    @pl.when(first_ref[i] == 1)
    def _():
        w2b[...] = w2_ref[0, 0].astype(BF16)

    @pl.when(i < nused_ref[0])
    def _():
        _store_token_major(obuf.at[slot], _dot(h_ref[...], w2b[...]) * sw_ref[...])

    @pl.when(i >= nused_ref[0])
    def _():
        obuf[slot] = jnp.zeros(obuf.shape[1:], F32)

    for r in range(tm):
        pltpu.make_async_copy(obuf.at[slot, pl.ds(r * pitch, pitch)], y_hbm.at[pl.ds(si_ref[0, 0, r], pitch)],
                              ssem.at[slot]).start(priority=r % 2)

    @pl.when(i == n - 1)
    def _():
        wait_scatter(slot)

        @pl.when(n >= 2)
        def _():
            wait_scatter(1 - slot)


def _moe_experts(h2t, blk_e, nused, nval, gidx, sidx, slot_w, w1, w3, w2, layer):
    nblk, tm = gidx.shape
    d = w1.shape[2]
    ff = w1.shape[3]
    pitch = d // LANES
    rows = tm * pitch
    smem_blk = lambda f: pl.BlockSpec((1, 1, tm), f, memory_space=pltpu.SMEM)
    first = jnp.concatenate([jnp.ones((1,), jnp.int32), (blk_e[1:] != blk_e[:-1]).astype(jnp.int32)])
    gidx3 = (gidx * pitch).reshape(nblk, 1, tm)
    sidx3 = (sidx * pitch).reshape(nblk, 1, tm)
    up_spec = pltpu.PrefetchScalarGridSpec(
        num_scalar_prefetch=4,
        grid=(nblk,),
        in_specs=[smem_blk(lambda i, be, nu, nv, fi: (i, 0, 0)),
                  smem_blk(lambda i, be, nu, nv, fi: (jnp.minimum(i + 1, nblk - 1), 0, 0)),
                  pl.BlockSpec((1, 1, d, ff), lambda i, be, nu, nv, fi: (layer, be[i], 0, 0)),
                  pl.BlockSpec((1, 1, d, ff), lambda i, be, nu, nv, fi: (layer, be[i], 0, 0)),
                  pl.BlockSpec(memory_space=pl.ANY)],
        out_specs=pl.BlockSpec((tm, ff), lambda i, be, nu, nv, fi: (i, 0)),
        scratch_shapes=[pltpu.VMEM((2, rows, LANES), F32), pltpu.VMEM((d, ff), BF16), pltpu.VMEM((d, ff), BF16),
                        pltpu.SemaphoreType.DMA((2,))],
    )
    hb = pl.pallas_call(
        functools.partial(_ffn_up_body, pitch=pitch),
        grid_spec=up_spec,
        out_shape=jax.ShapeDtypeStruct((nblk * tm, ff), BF16),
        compiler_params=_params(("arbitrary",), 56),
        name="moe_ffn_up",
    )(blk_e, nused, nval, first, gidx3, gidx3, w1, w3, h2t)
    down_spec = pltpu.PrefetchScalarGridSpec(
        num_scalar_prefetch=3,
        grid=(nblk,),
        in_specs=[pl.BlockSpec((tm, ff), lambda i, be, nu, fi: (i, 0)),
                  smem_blk(lambda i, be, nu, fi: (i, 0, 0)),
                  pl.BlockSpec((tm, 1), lambda i, be, nu, fi: (i, 0)),
                  pl.BlockSpec((1, 1, ff, d), lambda i, be, nu, fi: (layer, be[i], 0, 0))],
        out_specs=pl.BlockSpec(memory_space=pl.ANY),
        scratch_shapes=[pltpu.VMEM((2, rows, LANES), F32), pltpu.VMEM((ff, d), BF16), pltpu.SemaphoreType.DMA((2,))],
    )
    return pl.pallas_call(
        functools.partial(_ffn_down_body, pitch=pitch),
        grid_spec=down_spec,
        out_shape=jax.ShapeDtypeStruct((nblk * rows, LANES), F32),
        compiler_params=_params(("arbitrary",), 48),
        name="moe_ffn_down",
    )(blk_e, nused, first, hb, sidx3, slot_w, w2)


def _moe_plan(e_idx, e_w, tm):
    n = e_idx.shape[0]
    m = n * TOP_K
    e_flat = e_idx.reshape(m)
    w_flat = e_w.reshape(m)
    order = jnp.argsort(e_flat).astype(jnp.int32)
    experts = jnp.arange(N_EXPERTS, dtype=jnp.int32)
    counts = jnp.sum((e_flat[:, None] == experts[None, :]).astype(jnp.int32), axis=0)
    padded = (counts + tm - 1) // tm * tm
    pad_end = jnp.cumsum(padded)
    pad_start = pad_end - padded
    cnt_start = jnp.cumsum(counts) - counts
    nblk = -(-(m + N_EXPERTS * (tm - 1)) // tm)
    blk_first = jnp.arange(nblk, dtype=jnp.int32) * tm
    blk_e = jnp.minimum(jnp.sum((pad_end[None, :] <= blk_first[:, None]).astype(jnp.int32), axis=1), N_EXPERTS - 1)
    rank = blk_first[:, None] + jnp.arange(tm, dtype=jnp.int32)[None, :] - pad_start[blk_e][:, None]
    valid = rank < counts[blk_e][:, None]
    src = order[jnp.clip(cnt_start[blk_e][:, None] + rank, 0, m - 1)]
    tok = jnp.where(valid, src // TOP_K, 0)
    slot_w = jnp.where(valid, w_flat[src], 0.0)
    pad_rank = (jnp.cumsum(jnp.logical_not(valid).reshape(-1).astype(jnp.int32)) - 1).reshape(nblk, tm)
    dst = jnp.where(valid, (src % TOP_K) * n + src // TOP_K, m + pad_rank)
    nused = (pad_end[-1] // tm).astype(jnp.int32).reshape(1)
    nval = jnp.sum(valid.astype(jnp.int32), axis=1)
    return (blk_e, nused, nval, tok.astype(jnp.int32), dst.astype(jnp.int32),
            slot_w.reshape(nblk * tm, 1).astype(F32))


def _combine_body(x_ref, y0_ref, y1_ref, g_ref, o_ref):
    tm = x_ref.shape[1]
    o_ref[0] = x_ref[0] + g_ref[0] * (_load_token_major(y0_ref, tm) + _load_token_major(y1_ref, tm))


def _moe_combine(x, yk, gate, row_off, n_tok, tm=512):
    b, l, d = x.shape
    tm = min(tm, l)
    assert l % tm == 0 and row_off % tm == 0 and n_tok % tm == 0
    nl = l // tm
    rows = tm * (d // LANES)
    blk = pl.BlockSpec((1, tm, d), lambda bi, i: (bi, i, 0))
    o0 = row_off // tm
    o1 = (n_tok + row_off) // tm
    return pl.pallas_call(
        _combine_body,
        grid=(b, nl),
        in_specs=[blk,
                  pl.BlockSpec((rows, LANES), lambda bi, i: (o0 + bi * nl + i, 0)),
                  pl.BlockSpec((rows, LANES), lambda bi, i: (o1 + bi * nl + i, 0)),
                  pl.BlockSpec((1, 1, d), lambda bi, i: (bi, 0, 0))],
        out_specs=blk,
        out_shape=jax.ShapeDtypeStruct((b, l, d), F32),
        compiler_params=_params(("parallel", "parallel")),
        name="moe_combine",
    )(x, yk, yk, gate)


def _mixers(ux, ucx, p, lam_init, latent):
    b, l, _ = ux.shape
    rope_q = _rope_tables(l, latent)
    no_rope = _rope_tables(ucx.shape[1], False)
    scale = HEAD_DIM ** -0.5
    qw = _qk_prep(ux, OFF_WQ, 512, p['win_qnorm'], rope_q, scale, pad_heads=True)
    qd = _qk_prep(ux, OFF_DQ, 512, p['diff_qnorm'], rope_q, scale * math.log2(math.e))
    kcw = _qk_prep(ucx, OFF_WK, 128, p['win_knorm'], no_rope, 1.0)
    vcw, vcws = _cast_cols(ucx, OFF_WV, 128, swapped=True)
    kcd = _qk_prep(ucx, OFF_DK, 512, p['diff_knorm'], no_rope, 1.0)
    vcd = _cast_cols(ucx, OFF_DV, 512)
    if latent:
        kw = _qk_prep(ux, OFF_WK, 128, p['win_knorm'], rope_q, 1.0)
        vw, vws = _cast_cols(ux, OFF_WV, 128, swapped=True)
        kd = _qk_prep(ux, OFF_DK, 512, p['diff_knorm'], rope_q, 1.0)
        vd = _cast_cols(ux, OFF_DV, 512)
        y_win = _win_attn(qw, kw, vw, vws, kcw, vcw, vcws, p['win_sink'])
        k_all = jnp.concatenate([kd, kcd], axis=1)
        v_all = jnp.concatenate([vd, vcd], axis=1)
    else:
        y_win = _ctx_win_attn(qw, kcw, vcw, vcws, p['win_sink'])
        k_all, v_all = kcd, vcd
    y_diff = _diff_attn(qd, k_all, v_all, p['diff_lambda'], p['diff_subln'], lam_init, 512)
    y_hy = _hyena(ux, p)
    y_pool = _pool_mix(ux, p['pool_proj'], p['pool_scale'])
    return y_win, y_diff, y_hy, y_pool


def _layer(x, ctx, mod, p, layer_idx, update_ctx):
    b, s, d = x.shape
    c = ctx.shape[1]
    lam_init = 0.8 - 0.6 * math.exp(-0.3 * layer_idx)
    chunks = [mod[:, i * d:(i + 1) * d] for i in range(6)]
    sh1, sc1, g1, sh2, sc2, g2 = [t[:b].reshape(b, 1, d) for t in chunks]
    csh1, csc1, cg1, csh2, csc2, cg2 = [jnp.broadcast_to(t[b:b + 1].reshape(1, 1, d), (b, 1, d)) for t in chunks]

    w_in = _cast_weight(p['w_in'], layer_idx, 7 * LANES)
    w_branch = p['w_branch']
    w_out = p['w_out']
    tn_in = OFF_GT // 2

    hx = _prenorm(x, p['norm1'], sc1, sh1)
    hc = _prenorm(ctx, p['norm1'], csc1, csh1)
    ux = _matmul(hx.reshape(b * s, d), w_in, OFF_GT, 512, tn_in).reshape(b, s, OFF_GT)
    ucx = _matmul(hc.reshape(b * c, d), w_in, OFF_GT, 512, tn_in).reshape(b, c, OFF_GT)

    w_router = jnp.zeros((d, LANES), F32).at[:, :N_GROUPS].set(p['r_group_w'])
    w_router = w_router.at[:, N_GROUPS:N_GROUPS + N_EXPERTS].set(p['r_expert_w'])
    b_router = jnp.zeros((1, LANES), F32).at[0, :N_GROUPS].set(p['r_group_b'])
    b_router = b_router.at[0, N_GROUPS:N_GROUPS + N_EXPERTS].set(p['r_expert_b'])

    ys = _mixers(ux, ucx, p, lam_init, True)
    acc = _merge1(hx.reshape(b * s, d), [y.reshape(b * s, BRANCH_W) for y in ys], w_in, OFF_GT, w_branch, layer_idx)
    x, h2, e_idx, e_w = _merge2(acc.reshape(b, s, d), w_out, layer_idx, x, g1, p['norm2'], sc2, sh2, w_router, b_router)
    e_idx = e_idx.reshape(b * s, LANES)[:, :TOP_K]
    e_w = e_w.reshape(b * s, LANES)[:, :TOP_K]
    if update_ctx:
        ycs = _mixers(ucx, ucx, p, lam_init, False)
        acc_c = _merge1(hc.reshape(b * c, d), [y.reshape(b * c, BRANCH_W) for y in ycs], w_in, OFF_GT, w_branch, layer_idx)
        ctx, h2c, ec_idx, ec_w = _merge2(acc_c.reshape(b, c, d), w_out, layer_idx, ctx, cg1, p['norm2'], csc2, csh2,
                                         w_router, b_router)
        h2 = jnp.concatenate([h2, h2c], axis=0)
        e_idx = jnp.concatenate([e_idx, ec_idx.reshape(b * c, LANES)[:, :TOP_K]], axis=0)
        e_w = jnp.concatenate([e_w, ec_w.reshape(b * c, LANES)[:, :TOP_K]], axis=0)
    n_tok = e_idx.shape[0]
    blk_e, nused, nval, gidx, sidx, slot_w = _moe_plan(e_idx, e_w, MOE_TM)
    yk = _moe_experts(h2, blk_e, nused, nval, gidx, sidx, slot_w, p['e_w1'], p['e_w3'], p['e_w2'], layer_idx)
    x = _moe_combine(x, yk, g2, 0, n_tok)
    if update_ctx:
        ctx = _moe_combine(ctx, yk, cg2, b * s, n_tok)
    return x, ctx


def kernel(x, c, ctx, c_ctx, w_mod, b_mod, norm1, norm2, w_in, win_sink, win_qnorm, win_knorm, diff_qnorm, diff_knorm, diff_lambda, diff_subln, hy_conv_w, hy_conv_b, hy_w1, hy_b1, hy_w2, hy_b2, hy_w3, hy_freq, hy_decay, hy_skip, pool_proj, pool_scale, w_branch, w_out, r_group_w, r_group_b, r_expert_w, r_expert_b, e_w1, e_w3, e_w2):
    depth = w_mod.shape[0]
    b, _, d = x.shape
    cs = jnp.zeros((8, d), F32).at[:b].set(c).at[b].set(c_ctx)
    stacked = dict(w_in=w_in, w_branch=w_branch.astype(BF16), w_out=w_out.astype(BF16),
                   e_w1=e_w1, e_w3=e_w3, e_w2=e_w2)
    for l in range(depth):
        p = dict(norm1=norm1[l], norm2=norm2[l],
                 win_sink=win_sink[l], win_qnorm=win_qnorm[l], win_knorm=win_knorm[l],
                 diff_qnorm=diff_qnorm[l], diff_knorm=diff_knorm[l], diff_lambda=diff_lambda[l],
                 diff_subln=diff_subln[l], hy_conv_w=hy_conv_w[l], hy_conv_b=hy_conv_b[l],
                 hy_w1=hy_w1[l], hy_b1=hy_b1[l], hy_w2=hy_w2[l], hy_b2=hy_b2[l], hy_w3=hy_w3[l],
                 hy_freq=hy_freq[l], hy_decay=hy_decay[l], hy_skip=hy_skip[l],
                 pool_proj=pool_proj[l], pool_scale=pool_scale[l],
                 r_group_w=r_group_w[l], r_group_b=r_group_b[l], r_expert_w=r_expert_w[l],
                 r_expert_b=r_expert_b[l], **stacked)
        mod = _modulation(cs, w_mod, b_mod, l)
        x, ctx = _layer(x, ctx, mod, p, l, l < depth - 1)
    return x
```

```python
import functools
import math

import jax
import jax.numpy as jnp
import numpy as np
from jax import lax
from jax.experimental import pallas as pl
from jax.experimental.pallas import tpu as pltpu

F32 = jnp.float32
BF16 = jnp.bfloat16

GRID_W = 64
HEAD_DIM = 64
ROPE_THETA = 10000.0
EPS = 1e-6
NEG_INF = -1e30
BLOCK = 128
WINDOW = 128
LANES = 128

WIN_HEADS = 8
WIN_KV_HEADS = 2
WIN_GROUP = WIN_HEADS // WIN_KV_HEADS
DIFF_HEADS = 4
HY_WIDTH = 512
HY_EMB = 33
HY_BANDS = 16
POOL_WINDOWS = (2, 4, 8, 16)
N_BRANCH = 4
BRANCH_W = 512

OFF_WQ = 0
OFF_WK = 512
OFF_WV = 640
OFF_DQ = 768
OFF_DK = 1280
OFF_DV = 1792
OFF_HY = 2304
OFF_PL = 3840
OFF_GT = 4352

N_GROUPS = 4
EXPERTS_PER_GROUP = 8
N_EXPERTS = 32
TOP_K = 2
MOE_TM = 256

FFT_NA = 64
FFT_NB = 128
FFT_N = FFT_NA * FFT_NB


def _params(sem, mib=48):
    return pltpu.CompilerParams(dimension_semantics=sem, vmem_limit_bytes=mib * 2**20)


def _dot(a, b):
    return jnp.dot(a, b, preferred_element_type=F32)


def _split(a):
    hi = a.astype(BF16)
    lo = (a - hi.astype(F32)).astype(BF16)
    return hi, lo


def _dot3(a, b):
    ah, al = _split(a)
    bh, bl = _split(b)
    return _dot(ah, bh) + (_dot(ah, bl) + _dot(al, bh))


def _mod_body(c_ref, w_ref, b_ref, o_ref):
    c = c_ref[...]
    a = (c * jax.nn.sigmoid(c)).astype(BF16)
    o_ref[...] = _dot(a, w_ref[0].astype(BF16)) + b_ref[0]


def _modulation(cs, w_mod, b_mod, layer):
    depth, d, n = w_mod.shape
    tn = 1024
    return pl.pallas_call(
        _mod_body,
        grid=(n // tn,),
        in_specs=[pl.BlockSpec((8, d), lambda j: (0, 0)),
                  pl.BlockSpec((1, d, tn), lambda j: (layer, 0, j)),
                  pl.BlockSpec((1, 1, tn), lambda j: (layer, 0, j))],
        out_specs=pl.BlockSpec((8, tn), lambda j: (0, j)),
        out_shape=jax.ShapeDtypeStruct((8, n), F32),
        compiler_params=_params(("parallel",)),
        name="modulation",
    )(cs, w_mod, b_mod.reshape(depth, 1, n))


def _prenorm_body(x_ref, g_ref, sc_ref, sh_ref, o_ref):
    x = x_ref[0]
    ms = jnp.mean(x * x, axis=-1, keepdims=True)
    y = x * lax.rsqrt(ms + EPS) * g_ref[...]
    o_ref[0] = (y * (1.0 + sc_ref[0]) + sh_ref[0]).astype(o_ref.dtype)


def _prenorm(x, gain, scale, shift, out_dtype=BF16):
    b, l, d = x.shape
    ts = min(l, 1024)
    return pl.pallas_call(
        _prenorm_body,
        grid=(b, l // ts),
        in_specs=[pl.BlockSpec((1, ts, d), lambda bi, i: (bi, i, 0)),
                  pl.BlockSpec((1, d), lambda bi, i: (0, 0)),
                  pl.BlockSpec((1, 1, d), lambda bi, i: (bi, 0, 0)),
                  pl.BlockSpec((1, 1, d), lambda bi, i: (bi, 0, 0))],
        out_specs=pl.BlockSpec((1, ts, d), lambda bi, i: (bi, i, 0)),
        out_shape=jax.ShapeDtypeStruct((b, l, d), out_dtype),
        compiler_params=_params(("parallel", "parallel")),
        name="prenorm",
    )(x, gain.reshape(1, d), scale, shift)


def _mm_body(a_ref, w_ref, o_ref):
    o_ref[...] = _dot(a_ref[...], w_ref[...]).astype(o_ref.dtype)


def _cast_body(x_ref, o_ref):
    o_ref[...] = x_ref[...].astype(o_ref.dtype)


def _cast_layer_body(x_ref, o_ref):
    o_ref[...] = x_ref[0].astype(o_ref.dtype)


def _cast_weight(w, layer, tn):
    _, k, n = w.shape
    return pl.pallas_call(
        _cast_layer_body,
        grid=(n // tn,),
        in_specs=[pl.BlockSpec((1, k, tn), lambda j: (layer, 0, j))],
        out_specs=pl.BlockSpec((k, tn), lambda j: (0, j)),
        out_shape=jax.ShapeDtypeStruct((k, n), BF16),
        compiler_params=_params(("parallel",)),
        name="cast_weight",
    )(w)


def _matmul(a, w, n, tm, tn, out_dtype=F32):
    m, k = a.shape
    tm = min(tm, m)
    return pl.pallas_call(
        _mm_body,
        grid=(n // tn, m // tm),
        in_specs=[pl.BlockSpec((tm, k), lambda j, i: (i, 0)),
                  pl.BlockSpec((k, tn), lambda j, i: (0, j))],
        out_specs=pl.BlockSpec((tm, tn), lambda j, i: (i, j)),
        out_shape=jax.ShapeDtypeStruct((m, n), out_dtype),
        compiler_params=_params(("parallel", "parallel")),
        name="in_proj",
    )(a, w)


def _rope_tables(n_tok, rope):
    if not rope:
        return jnp.ones((n_tok, LANES), F32), jnp.zeros((n_tok, LANES), F32)
    rows = n_tok // GRID_W
    row = jnp.repeat(jnp.arange(rows), GRID_W).astype(F32)
    col = jnp.tile(jnp.arange(GRID_W), rows).astype(F32)
    quarter = HEAD_DIM // 4
    inv = ROPE_THETA ** (-jnp.arange(quarter, dtype=F32) / quarter)
    ar = row[:, None] * inv
    ac = col[:, None] * inv
    cos64 = jnp.concatenate([jnp.cos(ar), jnp.cos(ar), jnp.cos(ac), jnp.cos(ac)], axis=-1)
    sin64 = jnp.concatenate([-jnp.sin(ar), jnp.sin(ar), -jnp.sin(ac), jnp.sin(ac)], axis=-1)
    return jnp.tile(cos64, (1, 2)), jnp.tile(sin64, (1, 2))


def _head_ones():
    i = np.arange(LANES)
    return jnp.asarray((i[:, None] // HEAD_DIM) == (i[None, :] // HEAD_DIM), BF16)


def _qk_prep_body(x_ref, cos_ref, sin_ref, ones_ref, o_ref):
    x = x_ref[0]
    hi, lo = _split(x * x)
    ssum = _dot(hi, ones_ref[...]) + _dot(lo, ones_ref[...])
    lane = lax.broadcasted_iota(jnp.int32, x.shape, 1)
    quarter = HEAD_DIM // 4
    partner = jnp.where((lane & quarter) == 0, pltpu.roll(x, LANES - quarter, 1), pltpu.roll(x, quarter, 1))
    o = lax.rsqrt(ssum * (1.0 / HEAD_DIM) + EPS) * (x * cos_ref[...] + partner * sin_ref[...])
    if o_ref.shape[2] == LANES:
        o_ref[0] = o.astype(o_ref.dtype)
    else:
        low = lane < HEAD_DIM
        even = jnp.where(low, o, 0.0)
        odd = jnp.where(low, 0.0, o)
        kv_head = pl.program_id(1) // (WIN_GROUP // 2)
        first = jnp.where(kv_head == 0, even, pltpu.roll(even, HEAD_DIM, 1))
        second = jnp.where(kv_head == 0, pltpu.roll(odd, HEAD_DIM, 1), odd)
        o_ref[0] = jnp.concatenate([first, second], axis=-1).astype(o_ref.dtype)


def _qk_prep(ux, col_off, width, gain, tables, scale, pad_heads=False):
    b, l, _ = ux.shape
    cos_t, sin_t = tables
    c0 = col_off // LANES
    ow = 2 * LANES if pad_heads else LANES
    g = jnp.tile(gain.reshape(1, HEAD_DIM), (1, LANES // HEAD_DIM))
    quarter = HEAD_DIM // 4
    g_partner = g.reshape(-1, 2, quarter)[:, ::-1].reshape(1, LANES)
    cos_t = cos_t * (g * scale)
    sin_t = sin_t * (g_partner * scale)
    return pl.pallas_call(
        _qk_prep_body,
        grid=(b, width // LANES),
        in_specs=[pl.BlockSpec((1, l, LANES), lambda bi, j: (bi, 0, c0 + j)),
                  pl.BlockSpec((l, LANES), lambda bi, j: (0, 0)),
                  pl.BlockSpec((l, LANES), lambda bi, j: (0, 0)),
                  pl.BlockSpec((LANES, LANES), lambda bi, j: (0, 0))],
        out_specs=pl.BlockSpec((1, l, ow), lambda bi, j: (bi, 0, j)),
        out_shape=jax.ShapeDtypeStruct((b, l, width // LANES * ow), BF16),
        compiler_params=_params(("parallel", "parallel")),
        name="qk_prep",
    )(ux, cos_t, sin_t, _head_ones())


def _cast_swap_body(x_ref, o_ref, s_ref):
    x = x_ref[0]
    o_ref[0] = x.astype(o_ref.dtype)
    s_ref[0] = pltpu.roll(x, HEAD_DIM, 1).astype(s_ref.dtype)


def _cast_cols(ux, col_off, width, swapped=False):
    b, l, _ = ux.shape
    c0 = col_off // LANES
    blk = pl.BlockSpec((1, l, LANES), lambda bi, j: (bi, 0, j))
    shp = jax.ShapeDtypeStruct((b, l, width), BF16)
    return pl.pallas_call(
        _cast_swap_body if swapped else _cast_body,
        grid=(b, width // LANES),
        in_specs=[pl.BlockSpec((1, l, LANES), lambda bi, j: (bi, 0, c0 + j))],
        out_specs=[blk, blk] if swapped else blk,
        out_shape=[shp, shp] if swapped else shp,
        compiler_params=_params(("parallel", "parallel")),
        name="cast_cols",
    )(ux)


def _nt_dot(a, b):
    return lax.dot_general(a, b, (((1,), (1,)), ((), ())), preferred_element_type=F32)


_WIN_STACK = (0, 2, 1, 3)


def _win_heads(q, sink_ref, h, n):
    heads = [h * WIN_GROUP + g for g in _WIN_STACK]
    qh = jnp.concatenate([q[:, a * LANES:(a + 1) * LANES] for a in heads], axis=0)
    sk = jnp.concatenate([jnp.full((n, 1), sink_ref[a], F32) for a in heads], axis=0)
    return qh, sk


def _win_tiles(o_low, o_high, n):
    low = lax.broadcasted_iota(jnp.int32, (n, LANES), 1) < HEAD_DIM
    return [jnp.where(low, o_low[t * n:(t + 1) * n], o_high[t * n:(t + 1) * n]) for t in range(2)]


WIN_STEP_BLOCKS = 8


def _win_attn_body(sink_ref, q_ref, km_ref, k0_ref, kp_ref, vm_ref, v0_ref, vp_ref, wm_ref, w0_ref, wp_ref,
                   kc_ref, vc_ref, wc_ref, o_ref, *, seq, nsub):
    i = pl.program_id(1)
    kslab = jnp.concatenate([km_ref[0], k0_ref[0], kp_ref[0]], axis=0)
    vslab = jnp.concatenate([vm_ref[0], v0_ref[0], vp_ref[0]], axis=0)
    wslab = jnp.concatenate([wm_ref[0], w0_ref[0], wp_ref[0]], axis=0)
    kc = kc_ref[0]
    qi = lax.broadcasted_iota(jnp.int32, (BLOCK, 3 * BLOCK), 0)
    kj = lax.broadcasted_iota(jnp.int32, (BLOCK, 3 * BLOCK), 1)
    band = jnp.abs(kj - BLOCK - qi) <= WINDOW
    half = WIN_GROUP // 2 * BLOCK
    chains = [(j, h) for j in range(nsub) for h in range(WIN_KV_HEADS)]
    scores = []
    for j, h in chains:
        qh, sk = _win_heads(q_ref[0, j * BLOCK:(j + 1) * BLOCK], sink_ref, h, BLOCK)
        kpos = (i * nsub + j - 1) * BLOCK + kj
        valid = jnp.concatenate([band & (kpos >= 0) & (kpos < seq)] * WIN_GROUP, axis=0)
        s_loc = jnp.where(valid, _nt_dot(qh, kslab[j * BLOCK:(j + 3) * BLOCK]), NEG_INF)
        scores.append((s_loc, _nt_dot(qh, kc), sk))
    maxes = [jnp.maximum(jnp.maximum(jnp.max(s_loc, axis=-1, keepdims=True),
                                     jnp.max(s_ctx, axis=-1, keepdims=True)), sk) for s_loc, s_ctx, sk in scores]
    probs = [(jnp.exp(s_loc - m), jnp.exp(s_ctx - m), jnp.exp(sk - m)) for (s_loc, s_ctx, sk), m in zip(scores, maxes)]
    denoms = [jnp.sum(p_loc, axis=-1, keepdims=True) + jnp.sum(p_ctx, axis=-1, keepdims=True) + p_sink
              for p_loc, p_ctx, p_sink in probs]
    tiles = {}
    for (j, h), (p_loc, p_ctx, _), denom in zip(chains, probs, denoms):
        ws = slice(j * BLOCK, (j + 3) * BLOCK)
        v_nat = (vslab[ws], vc_ref[0])
        v_swp = (wslab[ws], wc_ref[0])
        v_low, v_high = (v_nat, v_swp) if h == 0 else (v_swp, v_nat)
        p_loc = p_loc.astype(BF16)
        p_ctx = p_ctx.astype(BF16)
        o_low = (_dot(p_loc[:half], v_low[0]) + _dot(p_ctx[:half], v_low[1])) / denom[:half]
        o_high = (_dot(p_loc[half:], v_high[0]) + _dot(p_ctx[half:], v_high[1])) / denom[half:]
        tiles[(j, h)] = _win_tiles(o_low, o_high, BLOCK)
    for j in range(nsub):
        row = [t for h in range(WIN_KV_HEADS) for t in tiles[(j, h)]]
        o_ref[0, j * BLOCK:(j + 1) * BLOCK] = jnp.concatenate(row, axis=-1).astype(o_ref.dtype)


def _win_attn(q, k, v, vs, kc, vc, vcs, sink):
    b, s, _ = q.shape
    c = kc.shape[1]
    nb = s // BLOCK
    nsub = math.gcd(nb, WIN_STEP_BLOCKS)
    rows = nsub * BLOCK
    kvw = WIN_KV_HEADS * HEAD_DIM
    cur = lambda bi, i: (bi, i, 0)
    prev = pl.BlockSpec((1, BLOCK, kvw), lambda bi, i: (bi, jnp.maximum(i * nsub - 1, 0), 0))
    main = pl.BlockSpec((1, rows, kvw), cur)
    nxt = pl.BlockSpec((1, BLOCK, kvw), lambda bi, i: (bi, jnp.minimum((i + 1) * nsub, nb - 1), 0))
    cspec = pl.BlockSpec((1, c, kvw), lambda bi, i: (bi, 0, 0))
    return pl.pallas_call(
        functools.partial(_win_attn_body, seq=s, nsub=nsub),
        grid=(b, nb // nsub),
        in_specs=[pl.BlockSpec(memory_space=pltpu.SMEM),
                  pl.BlockSpec((1, rows, WIN_HEADS * LANES), cur),
                  prev, main, nxt, prev, main, nxt, prev, main, nxt,
                  cspec, cspec, cspec],
        out_specs=pl.BlockSpec((1, rows, WIN_HEADS * HEAD_DIM), cur),
        out_shape=jax.ShapeDtypeStruct((b, s, WIN_HEADS * HEAD_DIM), BF16),
        compiler_params=_params(("parallel", "parallel")),
        name="win_attn",
    )(sink.astype(F32), q, k, k, k, v, v, v, vs, vs, vs, kc, vc, vcs)


def _ctx_win_attn_body(sink_ref, q_ref, k_ref, v_ref, w_ref, o_ref):
    q = q_ref[0]
    k = k_ref[0]
    c = q.shape[0]
    half = WIN_GROUP // 2 * c
    tiles = []
    for h in range(WIN_KV_HEADS):
        qh, sk = _win_heads(q, sink_ref, h, c)
        s = _nt_dot(qh, k)
        m = jnp.maximum(jnp.max(s, axis=-1, keepdims=True), sk)
        p = jnp.exp(s - m)
        denom = jnp.sum(p, axis=-1, keepdims=True) + jnp.exp(sk - m)
        p = p.astype(BF16)
        v_low, v_high = (v_ref[0], w_ref[0]) if h == 0 else (w_ref[0], v_ref[0])
        tiles.extend(_win_tiles(_dot(p[:half], v_low) / denom[:half], _dot(p[half:], v_high) / denom[half:], c))
    o_ref[0] = jnp.concatenate(tiles, axis=-1).astype(o_ref.dtype)


def _ctx_win_attn(q, k, v, vs, sink):
    b, c, _ = q.shape
    kvw = WIN_KV_HEADS * HEAD_DIM
    cspec = pl.BlockSpec((1, c, kvw), lambda bi: (bi, 0, 0))
    return pl.pallas_call(
        _ctx_win_attn_body,
        grid=(b,),
        in_specs=[pl.BlockSpec(memory_space=pltpu.SMEM),
                  pl.BlockSpec((1, c, WIN_HEADS * LANES), lambda bi: (bi, 0, 0)),
                  cspec, cspec, cspec],
        out_specs=pl.BlockSpec((1, c, WIN_HEADS * HEAD_DIM), lambda bi: (bi, 0, 0)),
        out_shape=jax.ShapeDtypeStruct((b, c, WIN_HEADS * HEAD_DIM), BF16),
        compiler_params=_params(("parallel",)),
        name="ctx_win_attn",
    )(sink.astype(F32), q, k, v, vs)


DIFF_CHAINS = 2


def _diff_attn_body(lam_ref, q_ref, k_ref, v_ref, g_ref, o_ref, *, lam_init):
    q = q_ref[0]
    k = k_ref[0]
    v = v_ref[0]
    lp = lam_ref[...]
    lam = (jnp.exp(jnp.sum(lp[0:1] * lp[1:2], axis=-1, keepdims=True))
           - jnp.exp(jnp.sum(lp[2:3] * lp[3:4], axis=-1, keepdims=True)) + lam_init)
    lane = lax.broadcasted_iota(jnp.int32, q.shape, 1)
    zero = jnp.zeros_like(q)
    q1 = jnp.where(lane < HEAD_DIM, q, zero)
    q2 = jnp.where(lane < HEAD_DIM, zero, q)
    rows = q.shape[0] // DIFF_CHAINS
    for t in range(DIFF_CHAINS):
        rs = slice(t * rows, (t + 1) * rows)
        s1 = _nt_dot(q1[rs], k)
        s2 = _nt_dot(q2[rs], k)
        p1 = jnp.exp2(s1 - jnp.max(s1, axis=-1, keepdims=True))
        p2 = jnp.exp2(s2 - jnp.max(s2, axis=-1, keepdims=True))
        l1 = jnp.sum(p1, axis=-1, keepdims=True)
        r21 = lam * l1 / jnp.sum(p2, axis=-1, keepdims=True)
        a = (p1 - p2 * r21).astype(BF16)
        o = _dot(a, v) / l1
        ms = jnp.mean(o * o, axis=-1, keepdims=True)
        o_ref[0, rs] = (o * lax.rsqrt(ms + EPS) * g_ref[...] * (1.0 - lam_init)).astype(o_ref.dtype)


def _diff_attn(q, k, v, lam_params, subln, lam_init, tq):
    b, nq, w = q.shape
    nk = k.shape[1]
    tq = min(tq, nq)
    hw = 2 * HEAD_DIM
    return pl.pallas_call(
        functools.partial(_diff_attn_body, lam_init=lam_init),
        grid=(b, DIFF_HEADS, nq // tq),
        in_specs=[pl.BlockSpec((4, HEAD_DIM), lambda bi, h, t: (0, 0)),
                  pl.BlockSpec((1, tq, hw), lambda bi, h, t: (bi, t, h)),
                  pl.BlockSpec((1, nk, hw), lambda bi, h, t: (bi, 0, h)),
                  pl.BlockSpec((1, nk, hw), lambda bi, h, t: (bi, 0, h)),
                  pl.BlockSpec((1, hw), lambda bi, h, t: (0, 0))],
        out_specs=pl.BlockSpec((1, tq, hw), lambda bi, h, t: (bi, t, h)),
        out_shape=jax.ShapeDtypeStruct((b, nq, w), BF16),
        compiler_params=_params(("parallel", "parallel", "parallel"), 56),
        name="diff_attn",
    )(lam_params, q, k, v, subln.reshape(1, hw))


def _hy_pre_body(x0_ref, x1_ref, v_ref, w0_ref, w1_ref, wv_ref, b0_ref, b1_ref, bv_ref, z_ref, x0c_ref):
    n = x0_ref.shape[1]
    row = lax.broadcasted_iota(jnp.int32, (n, 1), 0)

    def sconv(u_ref, w_ref, b_ref):
        u = u_ref[0]
        w = w_ref[...]
        up = jnp.where(row == 0, 0.0, pltpu.roll(u, 1, 0))
        un = jnp.where(row == n - 1, 0.0, pltpu.roll(u, n - 1, 0))
        return ((b_ref[...] + up * w[0:1]) + u * w[1:2]) + un * w[2:3]

    x0c_ref[0] = sconv(x0_ref, w0_ref, b0_ref)
    z_ref[0] = sconv(v_ref, wv_ref, bv_ref) * sconv(x1_ref, w1_ref, b1_ref)


def _hy_pre(ux, conv_w, conv_b):
    b, l, _ = ux.shape
    nc = HY_WIDTH // LANES
    c0 = OFF_HY // LANES
    ub = lambda seg: pl.BlockSpec((1, l, LANES), lambda bi, j: (bi, 0, c0 + seg * nc + j))
    wb = lambda seg: pl.BlockSpec((3, LANES), lambda bi, j: (0, seg * nc + j))
    bb = lambda seg: pl.BlockSpec((1, LANES), lambda bi, j: (0, seg * nc + j))
    out = pl.BlockSpec((1, l, LANES), lambda bi, j: (bi, 0, j))
    shp = jax.ShapeDtypeStruct((b, l, HY_WIDTH), F32)
    cb = conv_b.reshape(1, 3 * HY_WIDTH)
    return pl.pallas_call(
        _hy_pre_body,
        grid=(b, nc),
        in_specs=[ub(0), ub(1), ub(2), wb(0), wb(1), wb(2), bb(0), bb(1), bb(2)],
        out_specs=[out, out],
        out_shape=[shp, shp],
        compiler_params=_params(("parallel", "parallel")),
        name="hy_pre",
    )(ux, ux, ux, conv_w, conv_w, conv_w, cb, cb, cb)


def _hy_features(l):
    n = jnp.arange(l, dtype=F32)
    pos = jnp.concatenate([n, l - n])
    t = pos / max(l - 1, 1)
    w = 2.0 * math.pi * pos / l
    f = jnp.linspace(1e-4, HY_BANDS - 1, HY_BANDS, dtype=F32)
    flag = jnp.ones((2 * l, 1), F32).at[l].set(0.0)
    feat = jnp.concatenate([t[:, None], jnp.cos(w[:, None] * f), -jnp.sin(w[:, None] * f), flag], axis=-1)
    return jnp.pad(feat, ((0, 0), (0, LANES - HY_EMB - 1)))


def _hy_filter_body(f_ref, w1_ref, b1_ref, w2_ref, b2_ref, w3_ref, fr_ref, dec_ref, k_ref, sum_ref):
    first = (pl.program_id(0) == 0) & (pl.program_id(1) == 0)

    @pl.when(first)
    def _():
        sum_ref[...] = jnp.zeros_like(sum_ref)

    f = f_ref[...]
    fr = fr_ref[...]
    h = jnp.sin(fr[0:1] * (_dot3(f, w1_ref[...]) + b1_ref[...]))
    h = jnp.sin(fr[1:2] * (_dot3(h, w2_ref[...]) + b2_ref[...]))
    h = _dot3(h, w3_ref[...]) * jnp.exp(-f[:, 0:1] * jnp.abs(dec_ref[...]))
    h = h * f[:, HY_EMB:HY_EMB + 1]
    k_ref[...] = h
    sum_ref[...] += jnp.sum(jnp.abs(h), axis=0, keepdims=True)


def _hy_filter(l, p):
    tl = min(l, 1024)
    nt = l // tl
    order = p['hy_w1'].shape[1]
    w1 = jnp.zeros((LANES, order), F32).at[:HY_EMB].set(p['hy_w1'])
    const = lambda d, i: (0, 0)
    return pl.pallas_call(
        _hy_filter_body,
        grid=(2, nt),
        in_specs=[pl.BlockSpec((tl, LANES), lambda d, i: (d * nt + i, 0)),
                  pl.BlockSpec((LANES, order), const),
                  pl.BlockSpec((1, order), const),
                  pl.BlockSpec((order, order), const),
                  pl.BlockSpec((1, order), const),
                  pl.BlockSpec((order, HY_WIDTH), lambda d, i: (0, d)),
                  pl.BlockSpec((2, order), const),
                  pl.BlockSpec((1, HY_WIDTH), lambda d, i: (0, d))],
        out_specs=[pl.BlockSpec((tl, HY_WIDTH), lambda d, i: (d * nt + i, 0)),
                   pl.BlockSpec((1, HY_WIDTH), const)],
        out_shape=[jax.ShapeDtypeStruct((2 * l, HY_WIDTH), F32), jax.ShapeDtypeStruct((1, HY_WIDTH), F32)],
        compiler_params=_params(("arbitrary", "arbitrary")),
        name="hy_filter",
    )(_hy_features(l), w1, p['hy_b1'].reshape(1, order), p['hy_w2'], p['hy_b2'].reshape(1, order),
      p['hy_w3'], p['hy_freq'], p['hy_decay'].reshape(1, 2 * HY_WIDTH))


def _fft_consts(n_la):
    ka = np.arange(FFT_NA)[:, None]
    la = np.arange(n_la)[None, :]
    ang = 2.0 * math.pi * ka * la / FFT_NA
    f1 = np.concatenate([np.cos(ang), -np.sin(ang)], axis=0)
    lb = np.arange(FFT_NB)[None, :]
    th = 2.0 * math.pi * ka * lb / FFT_N
    tc, ts = np.cos(th), np.sin(th)
    a2 = 2.0 * math.pi * np.arange(FFT_NB)[:, None] * np.arange(FFT_NB)[None, :] / FFT_NB
    gre, gim = np.cos(a2), -np.sin(a2)
    g2 = np.block([[gre, gim], [-gim, gre]])
    g2i = np.block([[gre, -gim], [gim, gre]])
    lo = np.arange(FFT_NA // 2)[:, None]
    ph = 2.0 * math.pi * lo * np.arange(FFT_NA)[None, :] / FFT_NA
    f1i = np.concatenate([np.cos(ph), -np.sin(ph)], axis=1) / FFT_N
    return f1, tc, ts, g2, g2i, f1i


def _fft_fwd(x_ref, f1_ref, tc_ref, ts_ref, g2_ref, p2_ref, cc, dot):
    rhs = jnp.concatenate([x_ref[0, c] for c in range(cc)], axis=1)
    a = dot(f1_ref[...].astype(p2_ref.dtype), rhs.astype(p2_ref.dtype))
    tc = tc_ref[...]
    ts = ts_ref[...]
    for c in range(cc):
        are = a[0:FFT_NA, c * FFT_NB:(c + 1) * FFT_NB]
        aim = a[FFT_NA:2 * FFT_NA, c * FFT_NB:(c + 1) * FFT_NB]
        p2_ref[c * FFT_NA:(c + 1) * FFT_NA, 0:FFT_NB] = (are * tc + aim * ts).astype(p2_ref.dtype)
        p2_ref[c * FFT_NA:(c + 1) * FFT_NA, FFT_NB:2 * FFT_NB] = (aim * tc - are * ts).astype(p2_ref.dtype)
    return dot(p2_ref[...], g2_ref[...].astype(p2_ref.dtype))


def _fft_spec_body(x_ref, f1_ref, tc_ref, ts_ref, g2_ref, o_ref, p2_ref, *, cc):
    o_ref[...] = _fft_fwd(x_ref, f1_ref, tc_ref, ts_ref, g2_ref, p2_ref, cc, _dot3)


def _fft_conv_body(x_ref, kf_ref, f1_ref, tc_ref, ts_ref, g2_ref, g2i_ref, f1i_ref, o_ref, p2_ref, r2_ref, *, cc):
    z = _fft_fwd(x_ref, f1_ref, tc_ref, ts_ref, g2_ref, p2_ref, cc, _dot)
    kf = kf_ref[...]
    zre, zim = z[:, :FFT_NB], z[:, FFT_NB:]
    kre, kim = kf[:, :FFT_NB], kf[:, FFT_NB:]
    y = jnp.concatenate([zre * kre - zim * kim, zre * kim + zim * kre], axis=1).astype(BF16)
    u = _dot(y, g2i_ref[...].astype(BF16))
    tc = tc_ref[...]
    ts = ts_ref[...]
    for c in range(cc):
        ure = u[c * FFT_NA:(c + 1) * FFT_NA, 0:FFT_NB]
        uim = u[c * FFT_NA:(c + 1) * FFT_NA, FFT_NB:2 * FFT_NB]
        r2_ref[0:FFT_NA, c * FFT_NB:(c + 1) * FFT_NB] = (ure * tc - uim * ts).astype(BF16)
        r2_ref[FFT_NA:2 * FFT_NA, c * FFT_NB:(c + 1) * FFT_NB] = (ure * ts + uim * tc).astype(BF16)
    out = _dot(f1i_ref[...].astype(BF16), r2_ref[...])
    for c in range(cc):
        o_ref[0, c] = out[:, c * FFT_NB:(c + 1) * FFT_NB]


FFT_CC = 64


def _fft_spectrum(kt):
    nch, n_la, _ = kt.shape
    cc = FFT_CC
    f1, tc, ts, g2, _, _ = _fft_consts(n_la)
    const = lambda j: (0, 0)
    return pl.pallas_call(
        functools.partial(_fft_spec_body, cc=cc),
        grid=(nch // cc,),
        in_specs=[pl.BlockSpec((1, cc, n_la, FFT_NB), lambda j: (0, j, 0, 0)),
                  pl.BlockSpec((2 * FFT_NA, n_la), const),
                  pl.BlockSpec((FFT_NA, FFT_NB), const),
                  pl.BlockSpec((FFT_NA, FFT_NB), const),
                  pl.BlockSpec((2 * FFT_NB, 2 * FFT_NB), const)],
        out_specs=pl.BlockSpec((cc * FFT_NA, 2 * FFT_NB), lambda j: (j, 0)),
        out_shape=jax.ShapeDtypeStruct((nch * FFT_NA, 2 * FFT_NB), F32),
        scratch_shapes=[pltpu.VMEM((cc * FFT_NA, 2 * FFT_NB), F32)],
        compiler_params=_params(("parallel",)),
        name="fft_spectrum",
    )(kt[None], jnp.asarray(f1, F32), jnp.asarray(tc, F32), jnp.asarray(ts, F32), jnp.asarray(g2, F32))


def _fft_conv(zt, kf):
    b, nch, n_la, _ = zt.shape
    cc = FFT_CC
    f1, tc, ts, g2, g2i, f1i = _fft_consts(n_la)
    const = lambda j, bi: (0, 0)
    return pl.pallas_call(
        functools.partial(_fft_conv_body, cc=cc),
        grid=(nch // cc, b),
        in_specs=[pl.BlockSpec((1, cc, n_la, FFT_NB), lambda j, bi: (bi, j, 0, 0)),
                  pl.BlockSpec((cc * FFT_NA, 2 * FFT_NB), lambda j, bi: (j, 0)),
                  pl.BlockSpec((2 * FFT_NA, n_la), const),
                  pl.BlockSpec((FFT_NA, FFT_NB), const),
                  pl.BlockSpec((FFT_NA, FFT_NB), const),
                  pl.BlockSpec((2 * FFT_NB, 2 * FFT_NB), const),
                  pl.BlockSpec((2 * FFT_NB, 2 * FFT_NB), const),
                  pl.BlockSpec((FFT_NA // 2, 2 * FFT_NA), const)],
        out_specs=pl.BlockSpec((1, cc, n_la, FFT_NB), lambda j, bi: (bi, j, 0, 0)),
        out_shape=jax.ShapeDtypeStruct((b, nch, n_la, FFT_NB), F32),
        scratch_shapes=[pltpu.VMEM((cc * FFT_NA, 2 * FFT_NB), BF16),
                        pltpu.VMEM((2 * FFT_NA, cc * FFT_NB), BF16)],
        compiler_params=_params(("parallel", "parallel")),
        name="fft_conv",
    )(zt, kf, jnp.asarray(f1, F32), jnp.asarray(tc, F32), jnp.asarray(ts, F32),
      jnp.asarray(g2, F32), jnp.asarray(g2i, F32), jnp.asarray(f1i, F32))


def _dft_conv_body(z_ref, k_ref, fw_ref, fi_ref, o_ref):
    l = z_ref.shape[1]
    fw = fw_ref[...]
    n = fw.shape[1]
    kf = _dot3(fw, k_ref[...])
    zf = _dot3(fw[:, :l], z_ref[0])
    kre, kim = kf[:n], kf[n:]
    zre, zim = zf[:n], zf[n:]
    y = jnp.concatenate([zre * kre - zim * kim, zre * kim + zim * kre], axis=0)
    o_ref[0] = _dot3(fi_ref[...], y)


def _dft_conv(z, k):
    b, l, w = z.shape
    n = 2 * l
    ang = 2.0 * math.pi * np.arange(n)[:, None] * np.arange(n)[None, :] / n
    fw = np.concatenate([np.cos(ang), -np.sin(ang)], axis=0)
    ai = ang[:l]
    fi = np.concatenate([np.cos(ai), -np.sin(ai)], axis=1) / n
    return pl.pallas_call(
        _dft_conv_body,
        grid=(b,),
        in_specs=[pl.BlockSpec((1, l, w), lambda bi: (bi, 0, 0)),
                  pl.BlockSpec((n, w), lambda bi: (0, 0)),
                  pl.BlockSpec((2 * n, n), lambda bi: (0, 0)),
                  pl.BlockSpec((l, 2 * n), lambda bi: (0, 0))],
        out_specs=pl.BlockSpec((1, l, w), lambda bi: (bi, 0, 0)),
        out_shape=jax.ShapeDtypeStruct((b, l, w), F32),
        compiler_params=_params(("parallel",)),
        name="dft_conv",
    )(z, k, jnp.asarray(fw, F32), jnp.asarray(fi, F32))


def _hy_post_body(y_ref, z_ref, x0_ref, ks_ref, sk_ref, o_ref):
    y = y_ref[0] / (ks_ref[...] + EPS) + sk_ref[...] * z_ref[0]
    o_ref[0] = (y * x0_ref[0]).astype(o_ref.dtype)


def _hy_post(yraw, z, x0c, ksum, skip):
    b, l, w = z.shape
    ts = min(l, 1024)
    blk = pl.BlockSpec((1, ts, w), lambda bi, i: (bi, i, 0))
    vec = pl.BlockSpec((1, w), lambda bi, i: (0, 0))
    return pl.pallas_call(
        _hy_post_body,
        grid=(b, l // ts),
        in_specs=[blk, blk, blk, vec, vec],
        out_specs=blk,
        out_shape=jax.ShapeDtypeStruct((b, l, w), BF16),
        compiler_params=_params(("parallel", "parallel")),
        name="hy_post",
    )(yraw, z, x0c, ksum, skip.reshape(1, w))


def _hyena(ux, p):
    b, l, _ = ux.shape
    z, x0c = _hy_pre(ux, p['hy_conv_w'], p['hy_conv_b'])
    k_raw, ksum = _hy_filter(l, p)
    if 2 * l == FFT_N:
        kt = k_raw.T.reshape(HY_WIDTH, FFT_NA, FFT_NB)
        kf = _fft_spectrum(kt)
        zt = jnp.swapaxes(z, 1, 2).reshape(b, HY_WIDTH, l // FFT_NB, FFT_NB)
        yt = _fft_conv(zt, kf)
        yraw = jnp.swapaxes(yt.reshape(b, HY_WIDTH, l), 1, 2)
    else:
        yraw = _dft_conv(z, k_raw)
    return _hy_post(yraw, z, x0c, ksum, p['hy_skip'])


def _pool_body(u_ref, w_ref, sc_ref, o_ref, *, win):
    u = u_ref[0]
    n = u.shape[0]
    lo = win // 2
    hi = win - 1 - lo
    row = lax.broadcasted_iota(jnp.int32, (n, 1), 0)
    acc = u
    for j in range(1, lo + 1):
        acc = acc + jnp.where(row >= j, pltpu.roll(u, j, 0), 0.0)
    for j in range(1, hi + 1):
        acc = acc + jnp.where(row < n - j, pltpu.roll(u, n - j, 0), 0.0)
    cnt = (jnp.minimum(row + hi + 1, n) - jnp.maximum(row - lo, 0)).astype(F32)
    d = acc / cnt - u
    o_ref[0] = (_dot(d.astype(BF16), w_ref[0].astype(BF16)) * sc_ref[...]).astype(o_ref.dtype)


def _pool_mix(ux, w_grp, scale):
    b, l, _ = ux.shape
    c0 = OFF_PL // LANES
    outs = []
    for gi, win in enumerate(POOL_WINDOWS):
        outs.append(pl.pallas_call(
            functools.partial(_pool_body, win=win),
            grid=(b,),
            in_specs=[pl.BlockSpec((1, l, LANES), lambda bi, gi=gi: (bi, 0, c0 + gi)),
                      pl.BlockSpec((1, LANES, LANES), lambda bi, gi=gi: (gi, 0, 0)),
                      pl.BlockSpec((1, LANES), lambda bi, gi=gi: (0, gi))],
            out_specs=pl.BlockSpec((1, l, LANES), lambda bi: (bi, 0, 0)),
            out_shape=jax.ShapeDtypeStruct((b, l, LANES), BF16),
            compiler_params=_params(("parallel",)),
            name=f"pool_mix{win}",
        )(ux, w_grp, scale.reshape(1, -1)))
    return jnp.concatenate(outs, axis=-1)


def _merge1_body(hx_ref, y0, y1, y2, y3, g0, g1, g2, g3, b0, b1, b2, b3, o_ref):
    hx = hx_ref[...]
    acc = None
    for y_ref, g_ref, b_ref in ((y0, g0, b0), (y1, g1, b1), (y2, g2, b2), (y3, g3, b3)):
        t = jax.nn.sigmoid(_dot(hx, g_ref[...])) * _dot(y_ref[...], b_ref[0, 0])
        acc = t if acc is None else acc + t
    o_ref[...] = acc.astype(o_ref.dtype)


def _merge1(hx, ys, w_in, gate_off, w_branch, layer, tm=1024, tn=256):
    m, d = hx.shape
    tm = min(tm, m)
    nj = d // tn
    g0 = gate_off // tn
    row = lambda i, j: (i, 0)
    gspec = lambda br: pl.BlockSpec((d, tn), lambda i, j, br=br: (0, g0 + br * nj + j))
    bspec = lambda br: pl.BlockSpec((1, 1, BRANCH_W, tn), lambda i, j, br=br: (layer, br, 0, j))
    return pl.pallas_call(
        _merge1_body,
        grid=(m // tm, nj),
        in_specs=([pl.BlockSpec((tm, d), row)] + [pl.BlockSpec((tm, BRANCH_W), row)] * N_BRANCH
                  + [gspec(br) for br in range(N_BRANCH)] + [bspec(br) for br in range(N_BRANCH)]),
        out_specs=pl.BlockSpec((tm, tn), lambda i, j: (i, j)),
        out_shape=jax.ShapeDtypeStruct((m, d), BF16),
        compiler_params=_params(("parallel", "parallel")),
        name="merge_gate",
    )(hx, *ys, w_in, w_in, w_in, w_in, w_branch, w_branch, w_branch, w_branch)


def _merge2_body(a_ref, w_ref, x_ref, g_ref, n2_ref, sc_ref, sh_ref, wr_ref, br_ref, o_ref, h_ref, e_ref, ew_ref):
    x = x_ref[0] + g_ref[0] * _dot(a_ref[0], w_ref[0])
    o_ref[0] = x
    _route_rows(x, n2_ref, sc_ref, sh_ref, wr_ref, br_ref, h_ref, e_ref, ew_ref)


def _merge2(acc, w_out, layer, x, gate, norm2, scale2, shift2, w_router, b_router, tm=512):
    b, l, d = x.shape
    tm = min(tm, l)
    nl = l // tm
    blk = pl.BlockSpec((1, tm, d), lambda bi, i: (bi, i, 0))
    mod = pl.BlockSpec((1, 1, d), lambda bi, i: (bi, 0, 0))
    sm = pl.BlockSpec((1, tm, LANES), lambda bi, i: (bi, i, 0))
    const = lambda bi, i: (0, 0)
    return pl.pallas_call(
        _merge2_body,
        grid=(b, nl),
        in_specs=[blk, pl.BlockSpec((1, d, d), lambda bi, i: (layer, 0, 0)), blk, mod,
                  pl.BlockSpec((1, d), const), mod, mod,
                  pl.BlockSpec((d, LANES), const), pl.BlockSpec((1, LANES), const)],
        out_specs=[blk, pl.BlockSpec((tm * (d // LANES), LANES), lambda bi, i: (bi * nl + i, 0)), sm, sm],
        out_shape=[jax.ShapeDtypeStruct((b, l, d), F32),
                   jax.ShapeDtypeStruct((b * l * (d // LANES), LANES), F32),
                   jax.ShapeDtypeStruct((b, l, LANES), jnp.int32),
                   jax.ShapeDtypeStruct((b, l, LANES), F32)],
        compiler_params=_params(("parallel", "parallel"), 56),
        name="merge_out",
    )(acc, w_out, x, gate, norm2.reshape(1, d), scale2, shift2, w_router, b_router)


def _store_token_major(ref, x):
    n, d = x.shape
    pitch = d // LANES
    for j in range(pitch):
        ref[pl.ds(j, n, stride=pitch), :] = x[:, j * LANES:(j + 1) * LANES]


def _load_token_major(ref, n):
    pitch = ref.shape[0] // n
    return jnp.concatenate([ref[pl.ds(j, n, stride=pitch), :] for j in range(pitch)], axis=1)


def _route_rows(x, g_ref, sc_ref, sh_ref, wr_ref, br_ref, h_ref, e_ref, w_ref):
    ms = jnp.mean(x * x, axis=-1, keepdims=True)
    h = (x * lax.rsqrt(ms + EPS) * g_ref[...]) * (1.0 + sc_ref[0]) + sh_ref[0]
    _store_token_major(h_ref, h)
    logits = _dot3(h, wr_ref[...]) + br_ref[...]
    lane = lax.broadcasted_iota(jnp.int32, logits.shape, 1)
    big = jnp.int32(LANES)
    gl = jnp.where(lane < N_GROUPS, logits, -jnp.inf)
    ge = jnp.exp(gl - jnp.max(gl, axis=-1, keepdims=True))
    pg = ge / jnp.sum(ge, axis=-1, keepdims=True)
    pg_top = jnp.max(pg, axis=-1, keepdims=True)
    g_idx = jnp.min(jnp.where((pg == pg_top) & (lane < N_GROUPS), lane, big), axis=-1, keepdims=True)
    first = N_GROUPS + EXPERTS_PER_GROUP * g_idx
    le = jnp.where((lane >= first) & (lane < first + EXPERTS_PER_GROUP), logits, -jnp.inf)
    v1 = jnp.max(le, axis=-1, keepdims=True)
    i1 = jnp.min(jnp.where(le == v1, lane, big), axis=-1, keepdims=True)
    le2 = jnp.where(lane == i1, -jnp.inf, le)
    v2 = jnp.max(le2, axis=-1, keepdims=True)
    i2 = jnp.min(jnp.where(le2 == v2, lane, big), axis=-1, keepdims=True)
    e2 = jnp.exp(v2 - v1)
    den = 1.0 + e2
    e_ref[0] = jnp.where(lane == 0, i1 - N_GROUPS, jnp.where(lane == 1, i2 - N_GROUPS, 0))
    w_ref[0] = jnp.where(lane == 0, (1.0 / den) * pg_top, jnp.where(lane == 1, (e2 / den) * pg_top, 0.0))


MOE_GATHER_CHUNK = 32


def _ffn_up_body(blk_e_ref, nused_ref, nval_ref, first_ref, gi_ref, gn_ref, w1_ref, w3_ref, h_hbm, o_ref,
                 xbuf, w1b, w3b, sem, *, pitch):
    i = pl.program_id(0)
    n = pl.num_programs(0)
    tm = o_ref.shape[0]
    slot = i % 2
    chunk = MOE_GATHER_CHUNK

    def gather(idx_ref, s, nv):
        for c in range(tm // chunk):
            @pl.when(c * chunk < nv)
            def _():
                for r in range(c * chunk, (c + 1) * chunk):
                    pltpu.make_async_copy(h_hbm.at[pl.ds(idx_ref[0, 0, r], pitch)],
                                          xbuf.at[s, pl.ds(r * pitch, pitch)], sem.at[s]).start(priority=r % 2)

    @pl.when(i == 0)
    def _():
        xbuf[...] = jnp.zeros_like(xbuf)
        gather(gi_ref, 0, nval_ref[0])

    @pl.when(i + 1 < n)
    def _():
        gather(gn_ref, 1 - slot, nval_ref[jnp.minimum(i + 1, n - 1)])

    for c in range(tm // chunk):
        @pl.when(c * chunk < nval_ref[i])
        def _():
            pltpu.make_async_copy(h_hbm.at[pl.ds(0, chunk * pitch)],
                                  xbuf.at[slot, pl.ds(c * chunk * pitch, chunk * pitch)], sem.at[slot]).wait()

    @pl.when(first_ref[i] == 1)
    def _():
        w1b[...] = w1_ref[0, 0].astype(BF16)
        w3b[...] = w3_ref[0, 0].astype(BF16)

    @pl.when(i < nused_ref[0])
    def _():
        x = _load_token_major(xbuf.at[slot], tm).astype(BF16)
        a = _dot(x, w1b[...])
        o_ref[...] = ((a * jax.nn.sigmoid(a)) * _dot(x, w3b[...])).astype(o_ref.dtype)

    @pl.when(i >= nused_ref[0])
    def _():
        o_ref[...] = jnp.zeros_like(o_ref)


def _ffn_down_body(blk_e_ref, nused_ref, first_ref, h_ref, si_ref, sw_ref, w2_ref, y_hbm, obuf, w2b, ssem, *, pitch):
    i = pl.program_id(0)
    n = pl.num_programs(0)
    tm = sw_ref.shape[0]
    slot = i % 2

    def wait_scatter(s):
        pltpu.make_async_copy(obuf.at[s], y_hbm.at[pl.ds(0, tm * pitch)], ssem.at[s]).wait()

    @pl.when(i >= 2)
    def _():
        wait_scatter(slot)

    @pl.when(first_ref[i] == 1)
    def _():
        w2b[...] = w2_ref[0, 0].astype(BF16)

    @pl.when(i < nused_ref[0])
    def _():
        _store_token_major(obuf.at[slot], _dot(h_ref[...], w2b[...]) * sw_ref[...])

    @pl.when(i >= nused_ref[0])
    def _():
        obuf[slot] = jnp.zeros(obuf.shape[1:], F32)

    for r in range(tm):
        pltpu.make_async_copy(obuf.at[slot, pl.ds(r * pitch, pitch)], y_hbm.at[pl.ds(si_ref[0, 0, r], pitch)],
                              ssem.at[slot]).start(priority=r % 2)

    @pl.when(i == n - 1)
    def _():
        wait_scatter(slot)

        @pl.when(n >= 2)
        def _():
            wait_scatter(1 - slot)


def _moe_experts(h2t, blk_e, nused, nval, gidx, sidx, slot_w, w1, w3, w2, layer):
    nblk, tm = gidx.shape
    d = w1.shape[2]
    ff = w1.shape[3]
    pitch = d // LANES
    rows = tm * pitch
    smem_blk = lambda f: pl.BlockSpec((1, 1, tm), f, memory_space=pltpu.SMEM)
    first = jnp.concatenate([jnp.ones((1,), jnp.int32), (blk_e[1:] != blk_e[:-1]).astype(jnp.int32)])
    gidx3 = (gidx * pitch).reshape(nblk, 1, tm)
    sidx3 = (sidx * pitch).reshape(nblk, 1, tm)
    up_spec = pltpu.PrefetchScalarGridSpec(
        num_scalar_prefetch=4,
        grid=(nblk,),
        in_specs=[smem_blk(lambda i, be, nu, nv, fi: (i, 0, 0)),
                  smem_blk(lambda i, be, nu, nv, fi: (jnp.minimum(i + 1, nblk - 1), 0, 0)),
                  pl.BlockSpec((1, 1, d, ff), lambda i, be, nu, nv, fi: (layer, be[i], 0, 0)),
                  pl.BlockSpec((1, 1, d, ff), lambda i, be, nu, nv, fi: (layer, be[i], 0, 0)),
                  pl.BlockSpec(memory_space=pl.ANY)],
        out_specs=pl.BlockSpec((tm, ff), lambda i, be, nu, nv, fi: (i, 0)),
        scratch_shapes=[pltpu.VMEM((2, rows, LANES), F32), pltpu.VMEM((d, ff), BF16), pltpu.VMEM((d, ff), BF16),
                        pltpu.SemaphoreType.DMA((2,))],
    )
    hb = pl.pallas_call(
        functools.partial(_ffn_up_body, pitch=pitch),
        grid_spec=up_spec,
        out_shape=jax.ShapeDtypeStruct((nblk * tm, ff), BF16),
        compiler_params=_params(("arbitrary",), 56),
        name="moe_ffn_up",
    )(blk_e, nused, nval, first, gidx3, gidx3, w1, w3, h2t)
    down_spec = pltpu.PrefetchScalarGridSpec(
        num_scalar_prefetch=3,
        grid=(nblk,),
        in_specs=[pl.BlockSpec((tm, ff), lambda i, be, nu, fi: (i, 0)),
                  smem_blk(lambda i, be, nu, fi: (i, 0, 0)),
                  pl.BlockSpec((tm, 1), lambda i, be, nu, fi: (i, 0)),
                  pl.BlockSpec((1, 1, ff, d), lambda i, be, nu, fi: (layer, be[i], 0, 0))],
        out_specs=pl.BlockSpec(memory_space=pl.ANY),
        scratch_shapes=[pltpu.VMEM((2, rows, LANES), F32), pltpu.VMEM((ff, d), BF16), pltpu.SemaphoreType.DMA((2,))],
    )
    return pl.pallas_call(
        functools.partial(_ffn_down_body, pitch=pitch),
        grid_spec=down_spec,
        out_shape=jax.ShapeDtypeStruct((nblk * rows, LANES), F32),
        compiler_params=_params(("arbitrary",), 48),
        name="moe_ffn_down",
    )(blk_e, nused, first, hb, sidx3, slot_w, w2)


def _moe_plan(e_idx, e_w, tm):
    n = e_idx.shape[0]
    m = n * TOP_K
    e_flat = e_idx.reshape(m)
    w_flat = e_w.reshape(m)
    order = jnp.argsort(e_flat).astype(jnp.int32)
    experts = jnp.arange(N_EXPERTS, dtype=jnp.int32)
    counts = jnp.sum((e_flat[:, None] == experts[None, :]).astype(jnp.int32), axis=0)
    padded = (counts + tm - 1) // tm * tm
    pad_end = jnp.cumsum(padded)
    pad_start = pad_end - padded
    cnt_start = jnp.cumsum(counts) - counts
    nblk = -(-(m + N_EXPERTS * (tm - 1)) // tm)
    blk_first = jnp.arange(nblk, dtype=jnp.int32) * tm
    blk_e = jnp.minimum(jnp.sum((pad_end[None, :] <= blk_first[:, None]).astype(jnp.int32), axis=1), N_EXPERTS - 1)
    rank = blk_first[:, None] + jnp.arange(tm, dtype=jnp.int32)[None, :] - pad_start[blk_e][:, None]
    valid = rank < counts[blk_e][:, None]
    src = order[jnp.clip(cnt_start[blk_e][:, None] + rank, 0, m - 1)]
    tok = jnp.where(valid, src // TOP_K, 0)
    slot_w = jnp.where(valid, w_flat[src], 0.0)
    pad_rank = (jnp.cumsum(jnp.logical_not(valid).reshape(-1).astype(jnp.int32)) - 1).reshape(nblk, tm)
    dst = jnp.where(valid, (src % TOP_K) * n + src // TOP_K, m + pad_rank)
    nused = (pad_end[-1] // tm).astype(jnp.int32).reshape(1)
    nval = jnp.sum(valid.astype(jnp.int32), axis=1)
    return (blk_e, nused, nval, tok.astype(jnp.int32), dst.astype(jnp.int32),
            slot_w.reshape(nblk * tm, 1).astype(F32))


def _combine_body(x_ref, y0_ref, y1_ref, g_ref, o_ref):
    tm = x_ref.shape[1]
    o_ref[0] = x_ref[0] + g_ref[0] * (_load_token_major(y0_ref, tm) + _load_token_major(y1_ref, tm))


def _moe_combine(x, yk, gate, row_off, n_tok, tm=512):
    b, l, d = x.shape
    tm = min(tm, l)
    assert l % tm == 0 and row_off % tm == 0 and n_tok % tm == 0
    nl = l // tm
    rows = tm * (d // LANES)
    blk = pl.BlockSpec((1, tm, d), lambda bi, i: (bi, i, 0))
    o0 = row_off // tm
    o1 = (n_tok + row_off) // tm
    return pl.pallas_call(
        _combine_body,
        grid=(b, nl),
        in_specs=[blk,
                  pl.BlockSpec((rows, LANES), lambda bi, i: (o0 + bi * nl + i, 0)),
                  pl.BlockSpec((rows, LANES), lambda bi, i: (o1 + bi * nl + i, 0)),
                  pl.BlockSpec((1, 1, d), lambda bi, i: (bi, 0, 0))],
        out_specs=blk,
        out_shape=jax.ShapeDtypeStruct((b, l, d), F32),
        compiler_params=_params(("parallel", "parallel")),
        name="moe_combine",
    )(x, yk, yk, gate)


def _mixers(ux, ucx, p, lam_init, latent):
    b, l, _ = ux.shape
    rope_q = _rope_tables(l, latent)
    no_rope = _rope_tables(ucx.shape[1], False)
    scale = HEAD_DIM ** -0.5
    qw = _qk_prep(ux, OFF_WQ, 512, p['win_qnorm'], rope_q, scale, pad_heads=True)
    qd = _qk_prep(ux, OFF_DQ, 512, p['diff_qnorm'], rope_q, scale * math.log2(math.e))
    kcw = _qk_prep(ucx, OFF_WK, 128, p['win_knorm'], no_rope, 1.0)
    vcw, vcws = _cast_cols(ucx, OFF_WV, 128, swapped=True)
    kcd = _qk_prep(ucx, OFF_DK, 512, p['diff_knorm'], no_rope, 1.0)
    vcd = _cast_cols(ucx, OFF_DV, 512)
    if latent:
        kw = _qk_prep(ux, OFF_WK, 128, p['win_knorm'], rope_q, 1.0)
        vw, vws = _cast_cols(ux, OFF_WV, 128, swapped=True)
        kd = _qk_prep(ux, OFF_DK, 512, p['diff_knorm'], rope_q, 1.0)
        vd = _cast_cols(ux, OFF_DV, 512)
        y_win = _win_attn(qw, kw, vw, vws, kcw, vcw, vcws, p['win_sink'])
        k_all = jnp.concatenate([kd, kcd], axis=1)
        v_all = jnp.concatenate([vd, vcd], axis=1)
    else:
        y_win = _ctx_win_attn(qw, kcw, vcw, vcws, p['win_sink'])
        k_all, v_all = kcd, vcd
    y_diff = _diff_attn(qd, k_all, v_all, p['diff_lambda'], p['diff_subln'], lam_init, 512)
    y_hy = _hyena(ux, p)
    y_pool = _pool_mix(ux, p['pool_proj'], p['pool_scale'])
    return y_win, y_diff, y_hy, y_pool


def _layer(x, ctx, mod, p, layer_idx, update_ctx):
    b, s, d = x.shape
    c = ctx.shape[1]
    lam_init = 0.8 - 0.6 * math.exp(-0.3 * layer_idx)
    chunks = [mod[:, i * d:(i + 1) * d] for i in range(6)]
    sh1, sc1, g1, sh2, sc2, g2 = [t[:b].reshape(b, 1, d) for t in chunks]
    csh1, csc1, cg1, csh2, csc2, cg2 = [jnp.broadcast_to(t[b:b + 1].reshape(1, 1, d), (b, 1, d)) for t in chunks]

    w_in = _cast_weight(p['w_in'], layer_idx, 7 * LANES)
    w_branch = p['w_branch']
    w_out = p['w_out']
    tn_in = OFF_GT // 2

    hx = _prenorm(x, p['norm1'], sc1, sh1)
    hc = _prenorm(ctx, p['norm1'], csc1, csh1)
    ux = _matmul(hx.reshape(b * s, d), w_in, OFF_GT, 512, tn_in).reshape(b, s, OFF_GT)
    ucx = _matmul(hc.reshape(b * c, d), w_in, OFF_GT, 512, tn_in).reshape(b, c, OFF_GT)

    w_router = jnp.zeros((d, LANES), F32).at[:, :N_GROUPS].set(p['r_group_w'])
    w_router = w_router.at[:, N_GROUPS:N_GROUPS + N_EXPERTS].set(p['r_expert_w'])
    b_router = jnp.zeros((1, LANES), F32).at[0, :N_GROUPS].set(p['r_group_b'])
    b_router = b_router.at[0, N_GROUPS:N_GROUPS + N_EXPERTS].set(p['r_expert_b'])

    ys = _mixers(ux, ucx, p, lam_init, True)
    acc = _merge1(hx.reshape(b * s, d), [y.reshape(b * s, BRANCH_W) for y in ys], w_in, OFF_GT, w_branch, layer_idx)
    x, h2, e_idx, e_w = _merge2(acc.reshape(b, s, d), w_out, layer_idx, x, g1, p['norm2'], sc2, sh2, w_router, b_router)
    e_idx = e_idx.reshape(b * s, LANES)[:, :TOP_K]
    e_w = e_w.reshape(b * s, LANES)[:, :TOP_K]
    if update_ctx:
        ycs = _mixers(ucx, ucx, p, lam_init, False)
        acc_c = _merge1(hc.reshape(b * c, d), [y.reshape(b * c, BRANCH_W) for y in ycs], w_in, OFF_GT, w_branch, layer_idx)
        ctx, h2c, ec_idx, ec_w = _merge2(acc_c.reshape(b, c, d), w_out, layer_idx, ctx, cg1, p['norm2'], csc2, csh2,
                                         w_router, b_router)
        h2 = jnp.concatenate([h2, h2c], axis=0)
        e_idx = jnp.concatenate([e_idx, ec_idx.reshape(b * c, LANES)[:, :TOP_K]], axis=0)
        e_w = jnp.concatenate([e_w, ec_w.reshape(b * c, LANES)[:, :TOP_K]], axis=0)
    n_tok = e_idx.shape[0]
    blk_e, nused, nval, gidx, sidx, slot_w = _moe_plan(e_idx, e_w, MOE_TM)
    yk = _moe_experts(h2, blk_e, nused, nval, gidx, sidx, slot_w, p['e_w1'], p['e_w3'], p['e_w2'], layer_idx)
    x = _moe_combine(x, yk, g2, 0, n_tok)
    if update_ctx:
        ctx = _moe_combine(ctx, yk, cg2, b * s, n_tok)
    return x, ctx


def kernel(x, c, ctx, c_ctx, w_mod, b_mod, norm1, norm2, w_in, win_sink, win_qnorm, win_knorm, diff_qnorm, diff_knorm, diff_lambda, diff_subln, hy_conv_w, hy_conv_b, hy_w1, hy_b1, hy_w2, hy_b2, hy_w3, hy_freq, hy_decay, hy_skip, pool_proj, pool_scale, w_branch, w_out, r_group_w, r_group_b, r_expert_w, r_expert_b, e_w1, e_w3, e_w2):
    depth = w_mod.shape[0]
    b, _, d = x.shape
    cs = jnp.zeros((8, d), F32).at[:b].set(c).at[b].set(c_ctx)
    stacked = dict(w_in=w_in, w_branch=w_branch.astype(BF16), w_out=w_out.astype(BF16),
                   e_w1=e_w1, e_w3=e_w3, e_w2=e_w2)
    for l in range(depth):
        p = dict(norm1=norm1[l], norm2=norm2[l],
                 win_sink=win_sink[l], win_qnorm=win_qnorm[l], win_knorm=win_knorm[l],
                 diff_qnorm=diff_qnorm[l], diff_knorm=diff_knorm[l], diff_lambda=diff_lambda[l],
                 diff_subln=diff_subln[l], hy_conv_w=hy_conv_w[l], hy_conv_b=hy_conv_b[l],
                 hy_w1=hy_w1[l], hy_b1=hy_b1[l], hy_w2=hy_w2[l], hy_b2=hy_b2[l], hy_w3=hy_w3[l],
                 hy_freq=hy_freq[l], hy_decay=hy_decay[l], hy_skip=hy_skip[l],
                 pool_proj=pool_proj[l], pool_scale=pool_scale[l],
                 r_group_w=r_group_w[l], r_group_b=r_group_b[l], r_expert_w=r_expert_w[l],
                 r_expert_b=r_expert_b[l], **stacked)
        mod = _modulation(cs, w_mod, b_mod, l)
        x, ctx = _layer(x, ctx, mod, p, l, l < depth - 1)
    return x
```
